```python
import math, functools
import jax
import jax.numpy as jnp
from jax import lax
import numpy as np

D_MODEL = 1024
BATCH = 8
SEQ = 2048
DEPTH = 2
DEC_BATCH = 32
DEC_SEQ = 4
PAST_LEN = 16384
PAGE_SIZE = 128

HEAD_DIM = 64
MIX_WIDTH = D_MODEL
ML_WIDTH = D_MODEL // 4
ML_HEADS = ML_WIDTH // HEAD_DIM
ML_CHUNK = 64
LRU_WIDTH = D_MODEL // 4
LRU_BLOCKS = LRU_WIDTH // HEAD_DIM
LRU_C = 8.0
CONV_W = 4
NSA_WIDTH = MIX_WIDTH - ML_WIDTH - LRU_WIDTH
NSA_HEADS = NSA_WIDTH // HEAD_DIM
NSA_KV_HEADS = 2
NSA_REP = NSA_HEADS // NSA_KV_HEADS
NSA_KV_WIDTH = NSA_KV_HEADS * HEAD_DIM
CMP_BLK = 32
CMP_HID = 2 * HEAD_DIM
SLC_BLK = 64
N_SEL = 16
WINDOW = 512
WIN_QBLK = 128
SEL_QBLK = 128
FORCE_BONUS = 100.0
XA_HEADS = 4
XA_HEAD_DIM = 128
XA_INNER = XA_HEADS * XA_HEAD_DIM
N_MEM = 256
N_GROUPS = 4
EXP_PER_GROUP = 4
N_EXPERTS = N_GROUPS * EXP_PER_GROUP
EXP_FF = 256
TOP_FINE = 2
EPS = 1e-6
NEG = -1e30
IN_NAMES = ('ml_q', 'ml_k', 'ml_v', 'ml_o', 'ml_i', 'ml_f', 'lru_x', 'lru_g', 'nsa_q', 'k_cmp', 'v_cmp', 'k_slc', 'v_slc', 'k_win', 'v_win', 'nsa_gate')
IN_SIZES = (ML_WIDTH, ML_WIDTH, ML_WIDTH, ML_WIDTH, ML_HEADS, ML_HEADS, LRU_WIDTH, LRU_WIDTH, NSA_WIDTH) + (NSA_KV_WIDTH,) * 6 + (3 * NSA_HEADS,)
IN_COLS = 4 * ML_WIDTH + 2 * ML_HEADS + 2 * LRU_WIDTH + NSA_WIDTH + 6 * NSA_KV_WIDTH + 3 * NSA_HEADS
F32 = jnp.float32

kernel_name = 'hymba_mlstm_rglru_nsa_hmoe_step'


def rms_norm(x, g):
    xf = x.astype(F32)
    y = xf * lax.rsqrt(jnp.mean(xf * xf, axis=-1, keepdims=True) + EPS)
    return (y * g.astype(F32)).astype(x.dtype)


def headwise_rms(y, g):
    shp = y.shape
    yh = y.astype(F32).reshape(shp[:-1] + (shp[-1] // HEAD_DIM, HEAD_DIM))
    yh = yh * lax.rsqrt(jnp.mean(yh * yh, axis=-1, keepdims=True) + EPS)
    return yh.reshape(shp) * g.astype(F32)


def masked_softmax(s, mask):
    p = jax.nn.softmax(jnp.where(mask, s, NEG), axis=-1)
    return jnp.where(mask, p, 0.0)


def alibi_slopes():
    h = jnp.arange(1, NSA_HEADS + 1, dtype=F32)
    return (2.0 ** (-8.0 * h / NSA_HEADS)).reshape(NSA_KV_HEADS, NSA_REP)


def split_in(u):
    offs = [int(o) for o in np.cumsum(IN_SIZES)[:-1]]
    return dict(zip(IN_NAMES, jnp.split(u, offs, axis=-1)))


def to_chunks(a, nc, L):
    B = a.shape[0]
    a = a.reshape((B, nc, L) + a.shape[2:])
    return jnp.moveaxis(jnp.swapaxes(a, 2, 3), 1, 0)


def mlstm_chunkwise(q, k, v, i_pre, f_pre, C0, n0, m0):
    B, T, H, D = q.shape
    L = T if T <= ML_CHUNK else math.gcd(T, ML_CHUNK)
    nc = T // L
    xs = (to_chunks(q.astype(F32), nc, L), to_chunks(k.astype(F32) * D ** -0.5, nc, L),
          to_chunks(v.astype(F32), nc, L), to_chunks(jax.nn.log_sigmoid(f_pre.astype(F32)), nc, L),
          to_chunks(i_pre.astype(F32), nc, L))
    causal = jnp.tril(jnp.ones((L, L), dtype=bool))

    def step(carry, chunk):
        C, n, m = carry
        qc, kc, vc, lf, ig = chunk
        b = jnp.cumsum(lf, axis=-1)
        log_d = jnp.where(causal, b[..., :, None] - b[..., None, :] + ig[..., None, :], -jnp.inf)
        inter = b + m[..., None]
        m_t = jnp.maximum(inter, jnp.max(log_d, axis=-1))
        w_carry = jnp.exp(inter - m_t)
        s = jnp.einsum('bhtd,bhsd->bhts', qc, kc) * jnp.exp(log_d - m_t[..., None])
        num = jnp.einsum('bhts,bhse->bhte', s, vc) + w_carry[..., None] * jnp.einsum('bhed,bhtd->bhte', C, qc)
        den = jnp.sum(s, axis=-1) + w_carry * jnp.einsum('bhd,bhtd->bht', n, qc)
        h = num / jnp.maximum(jnp.abs(den), jnp.exp(-m_t))[..., None]
        b_end = b[..., -1]
        log_w = b_end[..., None] - b + ig
        m_new = jnp.maximum(b_end + m, jnp.max(log_w, axis=-1))
        w = jnp.exp(log_w - m_new[..., None])
        decay = jnp.exp(b_end + m - m_new)
        C_new = decay[..., None, None] * C + jnp.einsum('bhs,bhse,bhsd->bhed', w, vc, kc)
        n_new = decay[..., None] * n + jnp.einsum('bhs,bhsd->bhd', w, kc)
        return (C_new, n_new, m_new), h

    (C, n, m), h = lax.scan(step, (C0.astype(F32), n0.astype(F32), m0.astype(F32)), xs)
    h = jnp.swapaxes(jnp.moveaxis(h, 0, 1), 2, 3).reshape(B, T, H * D)
    return h, C, n, m


def mlstm_group(pt, p, C0, n0, m0):
    B, T, _ = pt['ml_q'].shape
    shp = (B, T, ML_HEADS, HEAD_DIM)
    h, C, n, m = mlstm_chunkwise(pt['ml_q'].reshape(shp), pt['ml_k'].reshape(shp), pt['ml_v'].reshape(shp),
                                 pt['ml_i'] + p['ml_i_bias'], pt['ml_f'] + p['ml_f_bias'], C0, n0, m0)
    return jax.nn.sigmoid(pt['ml_o'].astype(F32)) * h, C, n, m


def lru_combine(c1, c2):
    a1, b1 = c1
    a2, b2 = c2
    return a1 * a2, a2 * b1 + b2


def rglru_group(pt, p, conv_buf, h0):
    xb = pt['lru_x'].astype(F32)
    B, T, W = xb.shape
    xe = jnp.concatenate([conv_buf.astype(F32), xb], axis=1)
    xc = p['conv_b'].astype(F32) + xe[:, 0:T] * p['conv_w'][0]
    for j in range(1, CONV_W):
        xc = xc + xe[:, j:j + T] * p['conv_w'][j]
    xh = xc.reshape(B, T, LRU_BLOCKS, HEAD_DIM)
    r = jax.nn.sigmoid(jnp.einsum('btkc,kcd->btkd', xh, p['lru_wa'].astype(F32)).reshape(B, T, W) + p['lru_ba'])
    i = jax.nn.sigmoid(jnp.einsum('btkc,kcd->btkd', xh, p['lru_wx'].astype(F32)).reshape(B, T, W) + p['lru_bx'])
    log_a = -LRU_C * r * jax.nn.softplus(-p['lru_lambda'].astype(F32))
    a = jnp.exp(log_a)
    u = jnp.sqrt(-jnp.expm1(2.0 * log_a)) * (i * xc)
    u = u.at[:, 0].add(a[:, 0] * h0.astype(F32))
    _, h = lax.associative_scan(lru_combine, (a, u), axis=1)
    y = h * jax.nn.gelu(pt['lru_g'].astype(F32))
    return y, xe[:, T:], h[:, -1]


def nsa_split(pt):
    B, T, _ = pt['nsa_q'].shape
    q = pt['nsa_q'].reshape(B, T, NSA_KV_HEADS, NSA_REP, HEAD_DIM)
    kv = [pt[nm].reshape(B, T, NSA_KV_HEADS, HEAD_DIM) for nm in ('k_cmp', 'v_cmp', 'k_slc', 'v_slc', 'k_win', 'v_win')]
    return q, kv


def compress(raw, p, c):
    B, NC = raw.shape[:2]
    z = raw + p['cmp_pos'][c][:, None, :]
    z = jnp.swapaxes(z, 2, 3).reshape(B, NC, NSA_KV_HEADS, CMP_BLK * HEAD_DIM)
    return jax.nn.gelu(z @ p['phi_w1'][c] + p['phi_b1'][c]) @ p['phi_w2'][c]


def cmp_attend(q, kc, vc, qpos, slopes):
    nc = kc.shape[1]
    c_end = jnp.arange(nc) * CMP_BLK + (CMP_BLK - 1)
    dist = qpos[:, None] - c_end[None, :]
    s = jnp.einsum('btgrd,bngd->bgrtn', q, kc).astype(F32) * HEAD_DIM ** -0.5
    s = s - slopes[:, :, None, None] * dist.astype(F32)
    pr = masked_softmax(s, dist >= 0)
    return jnp.einsum('bgrtn,bngd->btgrd', pr, vc.astype(F32)), pr


def select_blocks(p_cmp, qpos, total_len):
    imp = jnp.sum(p_cmp, axis=2)
    nc = imp.shape[-1]
    ratio = SLC_BLK // CMP_BLK
    ns = -(-total_len // SLC_BLK)
    imp = jnp.pad(imp, ((0, 0), (0, 0), (0, 0), (0, ns * ratio - nc)))
    imp = imp.reshape(imp.shape[:3] + (ns, ratio)).sum(-1)
    j = jnp.arange(ns)[None, :]
    cur = (qpos // SLC_BLK)[:, None]
    valid = j <= cur
    forced = (j == 0) | (j == cur) | (j == cur - 1)
    score = jnp.where(valid, imp + FORCE_BONUS * forced.astype(F32), NEG)
    top, idx = lax.top_k(score, min(N_SEL, ns))
    return idx, top > 0.5 * NEG


def sel_attend_item(q, kb, vb, kpos, valid, qpos, slopes):
    G, Q, N, SB, D = kb.shape
    R = q.shape[2]
    s = jnp.einsum('qgrd,gqnkd->gqrnk', q, kb).astype(F32) * D ** -0.5
    dist = qpos[None, :, None, None] - kpos
    mask = (dist >= 0) & valid[..., None]
    s = s - slopes[:, None, :, None, None] * dist.astype(F32)[:, :, None]
    pr = masked_softmax(s.reshape(G, Q, R, N * SB), mask.reshape(G, Q, 1, N * SB))
    return jnp.einsum('gqrm,gqmd->qgrd', pr, vb.reshape(G, Q, N * SB, D).astype(F32))


def band_attend(q, k, v, qpos, kpos, slopes):
    s = jnp.einsum('bcqgrd,bckgd->bcgrqk', q, k).astype(F32) * HEAD_DIM ** -0.5
    dist = qpos[:, :, None] - kpos[:, None, :]
    mask = (dist >= 0) & (dist < WINDOW) & (kpos >= 0)[:, None, :]
    s = s - slopes[None, None, :, :, None, None] * dist.astype(F32)[None, :, None, None]
    pr = masked_softmax(s, mask[None, :, None, None])
    return jnp.einsum('bcgrqk,bckgd->bcqgrd', pr, v.astype(F32))


def nsa_merge(pt, o_cmp, o_slc, o_win):
    B, T, _ = pt['nsa_gate'].shape
    g = jax.nn.sigmoid(pt['nsa_gate'].astype(F32)).reshape(B, T, 3, NSA_KV_HEADS, NSA_REP, 1)
    o = g[:, :, 0] * o_cmp + g[:, :, 1] * o_slc + g[:, :, 2] * o_win
    return o.reshape(B, T, NSA_WIDTH)


def nsa_prompt(pt, p, slopes):
    q, (k_cmp, v_cmp, k_slc, v_slc, k_win, v_win) = nsa_split(pt)
    B, T, G, R, D = q.shape
    qpos = jnp.arange(T)
    nc = T // CMP_BLK
    kc = compress(k_cmp.reshape(B, nc, CMP_BLK, G, D), p, 0)
    vc = compress(v_cmp.reshape(B, nc, CMP_BLK, G, D), p, 1)
    o_cmp, p_cmp = cmp_attend(q, kc, vc, qpos, slopes)
    idx, valid = select_blocks(p_cmp, qpos, T)
    n_sel = idx.shape[-1]
    ns = T // SLC_BLK
    kblk = k_slc.reshape(B, ns, SLC_BLK, G, D).transpose(0, 3, 1, 2, 4)
    vblk = v_slc.reshape(B, ns, SLC_BLK, G, D).transpose(0, 3, 1, 2, 4)
    nq = T // SEL_QBLK

    def by_qblock(a):
        return a.reshape(B, G, nq, SEL_QBLK, n_sel).transpose(0, 2, 1, 3, 4).reshape(B * nq, G, SEL_QBLK, n_sel)

    gi = jnp.arange(G)[:, None, None]
    ar = jnp.arange(SLC_BLK)

    def one_qblock(args):
        item, qi, ii, vi = args
        b = item // nq
        qp = (item % nq) * SEL_QBLK + jnp.arange(SEL_QBLK)
        kpos = ii[..., None] * SLC_BLK + ar
        return sel_attend_item(qi, kblk[b, gi, ii], vblk[b, gi, ii], kpos, vi, qp, slopes)

    o_slc = lax.map(one_qblock, (jnp.arange(B * nq), q.reshape(B * nq, SEL_QBLK, G, R, D),
                                 by_qblock(idx), by_qblock(valid))).reshape(B, T, G, R, D)
    nw = T // WIN_QBLK
    span = WINDOW + WIN_QBLK
    kidx = jnp.arange(nw)[:, None] * WIN_QBLK + jnp.arange(span)[None, :]
    pad = ((0, 0), (WINDOW, 0), (0, 0), (0, 0))
    kw = jnp.pad(k_win, pad)[:, kidx]
    vw = jnp.pad(v_win, pad)[:, kidx]
    o_win = band_attend(q.reshape(B, nw, WIN_QBLK, G, R, D), kw, vw, qpos.reshape(nw, WIN_QBLK),
                        kidx - WINDOW, slopes).reshape(B, T, G, R, D)
    y = nsa_merge(pt, o_cmp, o_slc, o_win)
    kv_rows = jnp.stack([k_cmp, v_cmp, k_slc, v_slc], axis=2)
    win_rows = jnp.stack([k_win, v_win], axis=2)[:, T - min(WINDOW, T):]
    return y, kv_rows, win_rows


def nsa_sample(pt, p, slopes, kv_pool, l, page_table, win_buf):
    q, (k_cmp, v_cmp, k_slc, v_slc, k_win, v_win) = nsa_split(pt)
    B, T, G, R, D = q.shape
    past = page_table.shape[1] * PAGE_SIZE
    total = past + T
    qpos = past + jnp.arange(T)
    raw = kv_pool[l, page_table, :, 0:2].reshape(B, past, 2, G, D)
    raw = jnp.concatenate([raw.astype(k_cmp.dtype), jnp.stack([k_cmp, v_cmp], axis=2)], axis=1)
    nc = total // CMP_BLK
    raw = raw[:, :nc * CMP_BLK].reshape(B, nc, CMP_BLK, 2, G, D)
    kc = compress(raw[:, :, :, 0], p, 0)
    vc = compress(raw[:, :, :, 1], p, 1)
    o_cmp, p_cmp = cmp_attend(q, kc, vc, qpos, slopes)
    idx, valid = select_blocks(p_cmp, qpos, total)
    ns_past = past // SLC_BLK
    ns_new = -(-T // SLC_BLK)
    new_blk = jnp.pad(jnp.stack([k_slc, v_slc], axis=2), ((0, 0), (0, ns_new * SLC_BLK - T), (0, 0), (0, 0), (0, 0)))
    new_blk = new_blk.reshape(B, ns_new, SLC_BLK, 2, G, D).transpose(0, 4, 1, 2, 3, 5)
    bi = jnp.arange(B)[:, None, None, None]
    gi = jnp.arange(G)[None, :, None, None]
    ar = jnp.arange(SLC_BLK)
    start = jnp.minimum(idx, ns_past - 1) * SLC_BLK
    phys = page_table[bi, start // PAGE_SIZE][..., None]
    rows = (start % PAGE_SIZE)[..., None] + ar
    kb_past = kv_pool[l, phys, rows, 2, gi[..., None]]
    vb_past = kv_pool[l, phys, rows, 3, gi[..., None]]
    fresh = new_blk[bi, gi, jnp.clip(idx - ns_past, 0, ns_new - 1)]
    in_past = (idx < ns_past)[..., None, None]
    kb = jnp.where(in_past, kb_past.astype(fresh.dtype), fresh[..., 0, :])
    vb = jnp.where(in_past, vb_past.astype(fresh.dtype), fresh[..., 1, :])
    kpos = idx[..., None] * SLC_BLK + ar
    o_slc = jax.vmap(sel_attend_item, in_axes=(0, 0, 0, 0, 0, None, None))(q, kb, vb, kpos, valid, qpos, slopes)
    wb = win_buf.shape[1]
    kw = jnp.concatenate([win_buf[:, :, 0].astype(k_win.dtype), k_win], axis=1)
    vw = jnp.concatenate([win_buf[:, :, 1].astype(v_win.dtype), v_win], axis=1)
    kpos_w = (past - wb + jnp.arange(wb + T))[None]
    o_win = band_attend(q[:, None], kw[:, None], vw[:, None], qpos[None], kpos_w, slopes)[:, 0]
    y = nsa_merge(pt, o_cmp, o_slc, o_win)
    kv_rows = jnp.stack([k_cmp, v_cmp, k_slc, v_slc], axis=2)
    win_new = jnp.stack([kw, vw], axis=2)[:, T:]
    return y, kv_rows, win_new


def mix_out(ya, yb, yc, p, dtype):
    y = jnp.concatenate([ya.astype(F32), yb.astype(F32), yc.astype(F32)], axis=-1)
    return headwise_rms(y, p['mix_norm']).astype(dtype) @ p['w_out']


def memory_kv(mem, p):
    B, M, _ = mem.shape
    return (rms_norm(mem, p['norm_mem']) @ p['xa_wkv']).reshape(B, M, 2, XA_HEADS, XA_HEAD_DIM)


def cross_attend(xn, mkv, p):
    B, T, _ = xn.shape
    q = (xn @ p['xa_wq']).reshape(B, T, XA_HEADS, XA_HEAD_DIM)
    s = jnp.einsum('bthd,bmhd->bhtm', q, mkv[:, :, 0]).astype(F32) * XA_HEAD_DIM ** -0.5
    a = jax.nn.softmax(s, axis=-1)
    o = jnp.einsum('bhtm,bmhd->bthd', a, mkv[:, :, 1].astype(F32))
    return o.reshape(B, T, XA_INNER).astype(xn.dtype) @ p['xa_wo']


def hier_moe(xn, p):
    shp = xn.shape
    z = xn.reshape(-1, shp[-1])
    pg = jax.nn.softmax((z @ p['router_gw'] + p['router_gb']).astype(F32), axis=-1)
    gw, gsel = lax.top_k(pg, 1)
    gsel_1h = jax.nn.one_hot(gsel[:, 0], N_GROUPS, dtype=F32)
    le = (z @ p['router_ew'] + p['router_eb']).astype(F32).reshape(-1, N_GROUPS, EXP_PER_GROUP)
    le_sel = jnp.einsum('ng,nge->ne', gsel_1h, le)
    tv, ti = lax.top_k(le_sel, TOP_FINE)
    fine = jnp.einsum('nk,nke->ne', jax.nn.softmax(tv, axis=-1), jax.nn.one_hot(ti, EXP_PER_GROUP, dtype=F32))
    comb = ((gw * gsel_1h)[:, :, None] * fine[:, None, :]).reshape(-1, N_EXPERTS)
    hg = jnp.einsum('nd,edf->nef', z, p['exp_w_gate'])
    hu = jnp.einsum('nd,edf->nef', z, p['exp_w_up'])
    h = jax.nn.silu(hg) * hu * comb[:, :, None].astype(z.dtype)
    return jnp.einsum('nef,efd->nd', h, p['exp_w_down']).reshape(shp)


def trunk_layer(x, p, nsa_fn, ml_state, lru_state, mkv):
    h = rms_norm(x, p['norm_mix'])
    pt = split_in(h @ p['w_in'])
    ya, C, n, m = mlstm_group(pt, p, ml_state[0], ml_state[1], ml_state[2])
    yb, conv, h_last = rglru_group(pt, p, lru_state[0], lru_state[1])
    yc, kv_rows, win_rows = nsa_fn(pt)
    x = x + mix_out(ya, yb, yc, p, x.dtype)
    x = x + cross_attend(rms_norm(x, p['norm_xa']), mkv, p)
    x = x + hier_moe(rms_norm(x, p['norm_ffn']), p)
    return x, (kv_rows, win_rows, C, n, m, h_last, conv)


def setup_inputs(seed: int = 0) -> dict:
    key = jax.random.key(seed)
    ks = jax.random.split(key, 64)
    counter = [0]

    def nk():
        counter[0] += 1
        return ks[counter[0] - 1]

    def nrm(shape, scale):
        return scale * jax.random.normal(nk(), shape, F32)

    def gain(shape):
        return 1.0 + nrm(shape, 0.02)

    G, HD = NSA_KV_HEADS, HEAD_DIM
    n_pages = PAST_LEN // PAGE_SIZE
    n_phys = (DEC_BATCH * n_pages * 5) // 4
    wb = min(WINDOW, PAST_LEN)
    page_table = jax.random.permutation(nk(), n_phys)[:DEC_BATCH * n_pages].reshape(DEC_BATCH, n_pages).astype(jnp.int32)
    a_c = jax.random.uniform(nk(), (DEPTH, LRU_WIDTH), F32, 0.9, 0.999)
    a0 = a_c ** (1.0 / LRU_C)
    lru_lambda = jnp.log(a0) - jnp.log1p(-a0)
    return {
        'x_prompt': nrm((BATCH, SEQ, D_MODEL), 1.0),
        'x_sample': nrm((DEC_BATCH, DEC_SEQ, D_MODEL), 1.0),
        'cache_nsa_kv': nrm((DEPTH, n_phys, PAGE_SIZE, 4, G, HD), 1.0),
        'state_nsa_win': nrm((DEPTH, DEC_BATCH, wb, 2, G, HD), 1.0),
        'state_mlstm_C': nrm((DEPTH, DEC_BATCH, ML_HEADS, HD, HD), 1.0),
        'state_mlstm_n': nrm((DEPTH, DEC_BATCH, ML_HEADS, HD), 1.0),
        'state_mlstm_m': nrm((DEPTH, DEC_BATCH, ML_HEADS), 1.0),
        'state_rglru_h': nrm((DEPTH, DEC_BATCH, LRU_WIDTH), 0.5),
        'state_rglru_conv': nrm((DEPTH, DEC_BATCH, CONV_W - 1, LRU_WIDTH), 1.0),
        'cache_mem_kv': nrm((DEPTH, DEC_BATCH, N_MEM, 2, XA_HEADS, XA_HEAD_DIM), 1.0),
        'page_table': page_table,
        'mem_prompt': nrm((BATCH, N_MEM, D_MODEL), 1.0),
        'norm_mix': gain((DEPTH, D_MODEL)),
        'w_in': nrm((DEPTH, D_MODEL, IN_COLS), D_MODEL ** -0.5),
        'ml_i_bias': nrm((DEPTH, ML_HEADS), 0.1),
        'ml_f_bias': 3.0 + 3.0 * jax.random.uniform(nk(), (DEPTH, ML_HEADS), F32),
        'conv_w': nrm((DEPTH, CONV_W, LRU_WIDTH), CONV_W ** -0.5),
        'conv_b': nrm((DEPTH, LRU_WIDTH), 0.02),
        'lru_wa': nrm((DEPTH, LRU_BLOCKS, HD, HD), HD ** -0.5),
        'lru_ba': nrm((DEPTH, LRU_WIDTH), 0.02),
        'lru_wx': nrm((DEPTH, LRU_BLOCKS, HD, HD), HD ** -0.5),
        'lru_bx': nrm((DEPTH, LRU_WIDTH), 0.02),
        'lru_lambda': lru_lambda,
        'phi_w1': nrm((DEPTH, 2, CMP_BLK * HD, CMP_HID), (CMP_BLK * HD) ** -0.5),
        'phi_b1': nrm((DEPTH, 2, CMP_HID), 0.02),
        'phi_w2': nrm((DEPTH, 2, CMP_HID, HD), CMP_HID ** -0.5),
        'cmp_pos': nrm((DEPTH, 2, CMP_BLK, HD), 0.1),
        'mix_norm': gain((DEPTH, MIX_WIDTH)),
        'w_out': nrm((DEPTH, MIX_WIDTH, D_MODEL), MIX_WIDTH ** -0.5),
        'norm_xa': gain((DEPTH, D_MODEL)),
        'norm_mem': gain((DEPTH, D_MODEL)),
        'xa_wq': nrm((DEPTH, D_MODEL, XA_INNER), D_MODEL ** -0.5),
        'xa_wkv': nrm((DEPTH, D_MODEL, 2 * XA_INNER), D_MODEL ** -0.5),
        'xa_wo': nrm((DEPTH, XA_INNER, D_MODEL), XA_INNER ** -0.5),
        'norm_ffn': gain((DEPTH, D_MODEL)),
        'router_gw': nrm((DEPTH, D_MODEL, N_GROUPS), D_MODEL ** -0.5),
        'router_gb': nrm((DEPTH, N_GROUPS), 0.01),
        'router_ew': nrm((DEPTH, D_MODEL, N_EXPERTS), D_MODEL ** -0.5),
        'router_eb': nrm((DEPTH, N_EXPERTS), 0.01),
        'exp_w_gate': nrm((DEPTH, N_EXPERTS, D_MODEL, EXP_FF), D_MODEL ** -0.5),
        'exp_w_up': nrm((DEPTH, N_EXPERTS, D_MODEL, EXP_FF), D_MODEL ** -0.5),
        'exp_w_down': nrm((DEPTH, N_EXPERTS, EXP_FF, D_MODEL), EXP_FF ** -0.5),
        'final_norm': gain((D_MODEL,)),
    }


def reference(x_prompt, x_sample, cache_nsa_kv, state_nsa_win, state_mlstm_C, state_mlstm_n, state_mlstm_m,
              state_rglru_h, state_rglru_conv, cache_mem_kv, page_table, mem_prompt, norm_mix, w_in, ml_i_bias,
              ml_f_bias, conv_w, conv_b, lru_wa, lru_ba, lru_wx, lru_bx, lru_lambda, phi_w1, phi_b1, phi_w2, cmp_pos,
              mix_norm, w_out, norm_xa, norm_mem, xa_wq, xa_wkv, xa_wo, norm_ffn, router_gw, router_gb, router_ew,
              router_eb, exp_w_gate, exp_w_up, exp_w_down, final_norm):
    weights = dict(norm_mix=norm_mix, w_in=w_in, ml_i_bias=ml_i_bias, ml_f_bias=ml_f_bias, conv_w=conv_w,
                   conv_b=conv_b, lru_wa=lru_wa, lru_ba=lru_ba, lru_wx=lru_wx, lru_bx=lru_bx, lru_lambda=lru_lambda,
                   phi_w1=phi_w1, phi_b1=phi_b1, phi_w2=phi_w2, cmp_pos=cmp_pos, mix_norm=mix_norm, w_out=w_out,
                   norm_xa=norm_xa, norm_mem=norm_mem, xa_wq=xa_wq, xa_wkv=xa_wkv, xa_wo=xa_wo, norm_ffn=norm_ffn,
                   router_gw=router_gw, router_gb=router_gb, router_ew=router_ew, router_eb=router_eb,
                   exp_w_gate=exp_w_gate, exp_w_up=exp_w_up, exp_w_down=exp_w_down)
    slopes = alibi_slopes()

    x = x_prompt
    B = x.shape[0]
    outs_p, mem_p = [], []
    for l in range(DEPTH):
        p = {name: w[l] for name, w in weights.items()}
        ml0 = (jnp.zeros((B, ML_HEADS, HEAD_DIM, HEAD_DIM), F32), jnp.zeros((B, ML_HEADS, HEAD_DIM), F32),
               jnp.zeros((B, ML_HEADS), F32))
        lru0 = (jnp.zeros((B, CONV_W - 1, LRU_WIDTH), F32), jnp.zeros((B, LRU_WIDTH), F32))
        mkv = memory_kv(mem_prompt, p)
        x, st = trunk_layer(x, p, functools.partial(nsa_prompt, p=p, slopes=slopes), ml0, lru0, mkv)
        outs_p.append(st)
        mem_p.append(mkv)
    y_prompt = rms_norm(x, final_norm)

    x = x_sample
    outs_s = []
    for l in range(DEPTH):
        p = {name: w[l] for name, w in weights.items()}
        nsa_fn = functools.partial(nsa_sample, p=p, slopes=slopes, kv_pool=cache_nsa_kv, l=l,
                                   page_table=page_table, win_buf=state_nsa_win[l])
        x, st = trunk_layer(x, p, nsa_fn, (state_mlstm_C[l], state_mlstm_n[l], state_mlstm_m[l]),
                            (state_rglru_conv[l], state_rglru_h[l]), cache_mem_kv[l])
        outs_s.append(st)
    y_sample = rms_norm(x, final_norm)

    sp = [jnp.stack(a) for a in zip(*outs_p)]
    ss = [jnp.stack(a) for a in zip(*outs_s)]
    mem_kv_prompt = jnp.stack(mem_p)
    return (y_prompt, y_sample, sp[0], ss[0], sp[1], ss[1], sp[2], ss[2], sp[3], ss[3], sp[4], ss[4],
            sp[5], ss[5], sp[6], ss[6], mem_kv_prompt)
```

```python
import functools
import math

import jax
import jax.numpy as jnp
import numpy as np
from jax import lax
from jax.experimental import pallas as pl
from jax.experimental.pallas import tpu as pltpu

F32 = jnp.float32
BF16 = jnp.bfloat16
I32 = jnp.int32

D_MODEL = 1024
DEPTH = 2
PAGE = 128
HD = 64
ML_W = 256
ML_H = 4
LRU_W = 256
LRU_C = 8.0
CONV_W = 4
NSA_W = 512
NSA_H = 8
NSA_G = 2
NSA_R = 4
CMP_BLK = 32
CMP_HID = 128
SLC_BLK = 64
N_SEL = 16
WINDOW = 512
FORCE_BONUS = 100.0
XA_H = 4
XA_HD = 128
XA_INNER = 512
N_MEM = 256
N_GROUPS = 4
EXP_PER_GROUP = 4
N_EXP = 16
EXP_FF = 256
EPS = 1e-6
NEG = -1e30
SLOPES = tuple(2.0 ** (-(h + 1)) for h in range(NSA_H))

LANES = 128
SUBLANES = 8
VMEM_LIMIT = 56 * 1024 * 1024

IN_ML = (0, 1024)
IN_LRU = (1024, 1536)
IN_Q = (1536, 2048)
IN_KV = (2048, 2560)
IN_WIN = (2560, 2816)
IN_SMALL = (2816, 2944)
IN_COLS_P = 2944


def _cparams(sem):
    return pltpu.CompilerParams(dimension_semantics=sem, vmem_limit_bytes=VMEM_LIMIT)


def _dot(a, b):
    return jnp.dot(a, b, preferred_element_type=F32)


def _dot_nt(a, b):
    return lax.dot_general(a, b, (((1,), (1,)), ((), ())), preferred_element_type=F32)


def _dot_tn(a, b):
    return lax.dot_general(a, b, (((0,), (0,)), ((), ())), preferred_element_type=F32)


def _split2(x):
    hi = x.astype(BF16)
    lo = (x - hi.astype(F32)).astype(BF16)
    return hi, lo


def _dot_split(x, w_bf):
    hi, lo = _split2(x)
    return _dot(hi, w_bf) + _dot(lo, w_bf)


def _sigmoid(x):
    return 1.0 / (1.0 + jnp.exp(-x))


def _gelu(x):
    return 0.5 * x * (1.0 + jnp.tanh(0.7978845608028654 * (x + 0.044715 * (x * x * x))))


def _softplus(x):
    return jnp.maximum(x, 0.0) + jnp.log1p(jnp.exp(-jnp.abs(x)))


def _log_sigmoid(x):
    return -_softplus(-x)


def _rms(x, g):
    return x * lax.rsqrt(jnp.mean(x * x, axis=-1, keepdims=True) + EPS) * g


def _masked_softmax(s, mask):
    sm = jnp.where(mask, s, NEG)
    e = jnp.exp(sm - jnp.max(sm, axis=-1, keepdims=True))
    p = e / jnp.sum(e, axis=-1, keepdims=True)
    return jnp.where(mask, p, 0.0)


def _in_proj_body(x_ref, g_ref, w_ref, wst_ref, ml_ref, lru_ref, q_ref, kv_ref, win_ref, sm_ref,
                  smt_ref, kvsb_ref, winb_ref):
    hb = _rms(x_ref[...], g_ref[...]).astype(BF16)

    def mm(rng):
        return _dot(hb, w_ref[:, rng[0]:rng[1]])

    ml_ref[...] = mm(IN_ML)
    lru_ref[...] = mm(IN_LRU)
    q_ref[...] = mm(IN_Q).astype(BF16)
    kv = mm(IN_KV)
    kv_ref[...] = kv
    kvsb_ref[...] = kv[:, 256:512].astype(BF16)
    win = mm(IN_WIN)
    win_ref[...] = win
    winb_ref[...] = win.astype(BF16)
    sm_ref[...] = mm(IN_SMALL)
    smt_ref[...] = _dot_nt(wst_ref[...], hb)


def _in_proj(x2, g, w_p, w_st, tm):
    n = x2.shape[0]
    row = lambda w: pl.BlockSpec((tm, w), lambda i: (i, 0))
    full = lambda a: pl.BlockSpec(a.shape, lambda i: (0,) * a.ndim)
    shapes = [(1024, F32), (512, F32), (512, BF16), (512, F32), (256, F32), (128, F32)]
    out_shape = [jax.ShapeDtypeStruct((n, w), dt) for w, dt in shapes]
    out_specs = [row(w) for w, _ in shapes]
    out_shape += [jax.ShapeDtypeStruct((8, n), F32), jax.ShapeDtypeStruct((n, 256), BF16),
                  jax.ShapeDtypeStruct((n, 256), BF16)]
    out_specs += [pl.BlockSpec((8, tm), lambda i: (0, i)), row(256), row(256)]
    return pl.pallas_call(
        _in_proj_body, grid=(n // tm,),
        in_specs=[row(D_MODEL), full(g), full(w_p), full(w_st)],
        out_specs=out_specs, out_shape=out_shape,
        compiler_params=_cparams(("parallel",)), name="in_proj",
    )(x2, g, w_p, w_st)


def _norm_mm_body(x_ref, g_ref, w_ref, o_ref):
    o_ref[...] = _dot(_rms(x_ref[...], g_ref[...]).astype(BF16), w_ref[...]).astype(o_ref.dtype)


def _norm_mm(x2, g, w_bf, tm, out_dtype):
    n, k = x2.shape
    m = w_bf.shape[1]
    return pl.pallas_call(
        _norm_mm_body, grid=(n // tm,),
        in_specs=[pl.BlockSpec((tm, k), lambda i: (i, 0)), pl.BlockSpec((1, k), lambda i: (0, 0)),
                  pl.BlockSpec((k, m), lambda i: (0, 0))],
        out_specs=pl.BlockSpec((tm, m), lambda i: (i, 0)),
        out_shape=jax.ShapeDtypeStruct((n, m), out_dtype),
        compiler_params=_cparams(("parallel",)), name="norm_mm",
    )(x2, g, w_bf)


def _mm_res_body(a_ref, w_ref, x_ref, o_ref):
    o_ref[...] = x_ref[...] + _dot(a_ref[...], w_ref[...])


def _mm_res(a_bf, w_bf, x2, tm):
    n, k = a_bf.shape
    m = w_bf.shape[1]
    return pl.pallas_call(
        _mm_res_body, grid=(n // tm,),
        in_specs=[pl.BlockSpec((tm, k), lambda i: (i, 0)), pl.BlockSpec((k, m), lambda i: (0, 0)),
                  pl.BlockSpec((tm, m), lambda i: (i, 0))],
        out_specs=pl.BlockSpec((tm, m), lambda i: (i, 0)),
        out_shape=jax.ShapeDtypeStruct((n, m), F32),
        compiler_params=_cparams(("parallel",)), name="mm_res",
    )(a_bf, w_bf, x2)


def _final_norm_body(x_ref, g_ref, o_ref):
    o_ref[...] = _rms(x_ref[...], g_ref[...])


def _final_norm(x2, g, tm):
    n, k = x2.shape
    return pl.pallas_call(
        _final_norm_body, grid=(n // tm,),
        in_specs=[pl.BlockSpec((tm, k), lambda i: (i, 0)), pl.BlockSpec((1, k), lambda i: (0, 0))],
        out_specs=pl.BlockSpec((tm, k), lambda i: (i, 0)),
        out_shape=jax.ShapeDtypeStruct((n, k), F32),
        compiler_params=_cparams(("parallel",)), name="final_norm",
    )(x2, g)


def _mlstm_body(u_ref, sm_ref, smt_ref, bcol_ref, brow_ref, c0_ref, n0_ref, m0_ref,
                y_ref, c_ref, n_ref, m_ref, c_s, n_s, m_s, *, L, t_real):
    ci = pl.program_id(1)

    @pl.when(ci == 0)
    def _():
        c_s[...] = c0_ref[...]
        n_s[...] = n0_ref[...]
        m_s[...] = m0_ref[...]

    sm = sm_ref[...] + bcol_ref[...]
    smt = smt_ref[...] + brow_ref[...]
    row = lax.broadcasted_iota(I32, (L, L), 0)
    col = lax.broadcasted_iota(I32, (L, L), 1)
    causal = col <= row
    lane = lax.broadcasted_iota(I32, (1, LANES), 1)
    real_col = lax.broadcasted_iota(I32, (L, 1), 0) < t_real
    real_row = lax.broadcasted_iota(I32, (1, L), 1) < t_real
    m_all = m_s[...]
    m_next = m_all
    for h in range(ML_H):
        q = u_ref[:, 64 * h:64 * h + 64]
        k = u_ref[:, 256 + 64 * h:256 + 64 * h + 64] * (HD ** -0.5)
        v = u_ref[:, 512 + 64 * h:512 + 64 * h + 64]
        o = u_ref[:, 768 + 64 * h:768 + 64 * h + 64]
        ig_col = jnp.where(real_col, sm[:, h:h + 1], NEG)
        lf_col = jnp.where(real_col, _log_sigmoid(sm[:, 4 + h:5 + h]), 0.0)
        ig_row = jnp.where(real_row, smt[h:h + 1, :], NEG)
        lf_row = jnp.where(real_row, _log_sigmoid(smt[4 + h:5 + h, :]), 0.0)
        b_col = jnp.sum(jnp.where(causal, lf_row, 0.0), axis=1, keepdims=True)
        b_row = jnp.sum(jnp.where(row <= col, lf_col, 0.0), axis=0, keepdims=True)
        m_prev = m_all[:, h:h + 1]
        log_d = jnp.where(causal, b_col - b_row + ig_row, NEG)
        inter = b_col + m_prev
        m_t = jnp.maximum(inter, jnp.max(log_d, axis=1, keepdims=True))
        w_carry = jnp.exp(inter - m_t)
        qb = q.astype(BF16)
        vb = v.astype(BF16)
        s = _dot_nt(qb, k.astype(BF16)) * jnp.exp(log_d - m_t)
        c_old = c_s[h]
        n_old = n_s[h]
        num = _dot(s.astype(BF16), vb) + w_carry * _dot_nt(qb, c_old.astype(BF16))
        den = jnp.sum(s, axis=1, keepdims=True) + w_carry * jnp.sum(q * n_old, axis=1, keepdims=True)
        hh = num / jnp.maximum(jnp.abs(den), jnp.exp(-m_t))
        y_ref[:, 64 * h:64 * h + 64] = _sigmoid(o) * hh
        b_end = b_col[L - 1:L, :]
        log_w_row = b_end - b_row + ig_row
        m_new = jnp.maximum(b_end + m_prev, jnp.max(log_w_row, axis=1, keepdims=True))
        w_col = jnp.exp(b_end - b_col + ig_col - m_new)
        decay = jnp.exp(b_end + m_prev - m_new)
        c_s[h] = decay * c_old + _dot_tn((v * w_col).astype(BF16), k.astype(BF16))
        n_s[h] = decay * n_old + jnp.sum(k * w_col, axis=0, keepdims=True)
        m_next = jnp.where(lane == h, m_new, m_next)
    m_s[...] = m_next

    @pl.when(ci == pl.num_programs(1) - 1)
    def _():
        c_ref[...] = c_s[...]
        n_ref[...] = n_s[...]
        m_ref[...] = m_s[...]


def _mlstm(u_ml, u_sm, u_smt, bcol, brow, c0, n0, m0, L, t_real):
    b, t, _ = u_ml.shape
    nc = t // L
    assert t_real == L or nc == 1
    return pl.pallas_call(
        functools.partial(_mlstm_body, L=L, t_real=t_real), grid=(b, nc),
        in_specs=[pl.BlockSpec((None, L, 1024), lambda i, c: (i, c, 0)),
                  pl.BlockSpec((None, L, 128), lambda i, c: (i, c, 0)),
                  pl.BlockSpec((None, 8, L), lambda i, c: (i, 0, c)),
                  pl.BlockSpec((1, 128), lambda i, c: (0, 0)),
                  pl.BlockSpec((8, 1), lambda i, c: (0, 0)),
                  pl.BlockSpec((None, 4, 64, 64), lambda i, c: (i, 0, 0, 0)),
                  pl.BlockSpec((None, 4, 1, 64), lambda i, c: (i, 0, 0, 0)),
                  pl.BlockSpec((None, 1, 128), lambda i, c: (i, 0, 0))],
        out_specs=[pl.BlockSpec((None, L, 256), lambda i, c: (i, c, 0)),
                   pl.BlockSpec((None, 4, 64, 64), lambda i, c: (i, 0, 0, 0)),
                   pl.BlockSpec((None, 4, 1, 64), lambda i, c: (i, 0, 0, 0)),
                   pl.BlockSpec((None, 1, 128), lambda i, c: (i, 0, 0))],
        out_shape=[jax.ShapeDtypeStruct((b, t, 256), F32), jax.ShapeDtypeStruct((b, 4, 64, 64), F32),
                   jax.ShapeDtypeStruct((b, 4, 1, 64), F32), jax.ShapeDtypeStruct((b, 1, 128), F32)],
        scratch_shapes=[pltpu.VMEM((4, 64, 64), F32), pltpu.VMEM((4, 1, 64), F32),
                        pltpu.VMEM((1, 128), F32)],
        compiler_params=_cparams(("parallel", "arbitrary")), name="mlstm",
    )(u_ml, u_sm, u_smt, bcol, brow, c0, n0, m0)


def _lru_body(u_ref, cb_ref, h0_ref, cw_ref, cbias_ref, wa_ref, ba_ref, wx_ref, bx_ref, lam_ref,
              y_ref, tail_ref, hl_ref, tail_s, h_s, a_s, hs_s, *, tt, nb):
    i = pl.program_id(0)

    @pl.when(i == 0)
    def _():
        tail_s[...] = cb_ref[...]
        h_s[...] = h0_ref[...]

    x = u_ref[:, :, 0:LRU_W]
    g = u_ref[:, :, LRU_W:2 * LRU_W]
    xe = jnp.concatenate([tail_s[...], x], axis=0)
    xc = cbias_ref[...] + xe[0:tt] * cw_ref[0:1, :]
    for j in range(1, CONV_W):
        xc = xc + xe[j:j + tt] * cw_ref[j:j + 1, :]
    xc2 = xc.reshape(tt * nb, LRU_W)
    xcb = xc2.astype(BF16)
    r = _sigmoid(_dot(xcb, wa_ref[...]) + ba_ref[...])
    ig = _sigmoid(_dot(xcb, wx_ref[...]) + bx_ref[...])
    log_a = -LRU_C * r * _softplus(-lam_ref[...])
    a = jnp.exp(log_a)
    mult = jnp.sqrt(jnp.tanh(-log_a) * (a * a + 1.0))
    a_s[...] = a.reshape(tt, nb, LRU_W)
    hs_s[...] = (mult * (ig * xc2)).reshape(tt, nb, LRU_W)

    def step(t, h):
        h = a_s[t] * h + hs_s[t]
        hs_s[t] = h
        return h

    h_last = lax.fori_loop(0, tt, step, h_s[...], unroll=min(8, tt))
    h_s[...] = h_last
    y_ref[...] = hs_s[...] * _gelu(g)
    tail_s[...] = xe[tt:tt + CONV_W - 1]

    @pl.when(i == pl.num_programs(0) - 1)
    def _():
        tail_ref[...] = tail_s[...]
        hl_ref[...] = h_s[...]


def _lru(u_tm, cb_tm, h0, cw, cbias, wa_bd, ba, wx_bd, bx, lam, tt):
    t, nb, _ = u_tm.shape
    full = lambda a: pl.BlockSpec(a.shape, lambda i: (0,) * a.ndim)
    return pl.pallas_call(
        functools.partial(_lru_body, tt=tt, nb=nb), grid=(t // tt,),
        in_specs=[pl.BlockSpec((tt, nb, 512), lambda i: (i, 0, 0)), full(cb_tm), full(h0), full(cw),
                  full(cbias), full(wa_bd), full(ba), full(wx_bd), full(bx), full(lam)],
        out_specs=[pl.BlockSpec((tt, nb, 256), lambda i: (i, 0, 0)),
                   pl.BlockSpec((3, nb, 256), lambda i: (0, 0, 0)),
                   pl.BlockSpec((nb, 256), lambda i: (0, 0))],
        out_shape=[jax.ShapeDtypeStruct((t, nb, 256), F32), jax.ShapeDtypeStruct((3, nb, 256), F32),
                   jax.ShapeDtypeStruct((nb, 256), F32)],
        scratch_shapes=[pltpu.VMEM((3, nb, 256), F32), pltpu.VMEM((nb, 256), F32),
                        pltpu.VMEM((tt, nb, 256), F32), pltpu.VMEM((tt, nb, 256), F32)],
        compiler_params=_cparams(("arbitrary",)), name="rglru",
    )(u_tm, cb_tm, h0, cw, cbias, wa_bd, ba, wx_bd, bx, lam)


def _compress_body(pt_ref, pool_ref, bdk_ref, bdv_ref, posk_ref, posv_ref, b1k_ref, b1v_ref,
                   w2k_ref, w2v_ref, kc_ref, vc_ref, buf, sem, *, P):
    s = pl.program_id(0)
    ns = pl.num_programs(0)
    nblk = P * (PAGE // CMP_BLK)

    def page_copy(step, slot, p, c):
        return pltpu.make_async_copy(
            pool_ref.at[pt_ref[step * P + p], :, pl.ds(LANES * c, LANES)],
            buf.at[slot, c, pl.ds(p * PAGE, PAGE), :], sem.at[slot])

    def start_all(step, slot):
        def body(p, carry):
            page_copy(step, slot, p, 0).start()
            page_copy(step, slot, p, 1).start()
            return carry
        lax.fori_loop(0, P, body, 0)

    def wait_all(step, slot):
        def body(p, carry):
            page_copy(step, slot, p, 0).wait()
            page_copy(step, slot, p, 1).wait()
            return carry
        lax.fori_loop(0, P, body, 0)

    @pl.when(s == 0)
    def _():
        start_all(0, 0)

    slot = lax.rem(s, 2)

    @pl.when(s + 1 < ns)
    def _():
        start_all(s + 1, 1 - slot)

    wait_all(s, slot)

    acc_k = jnp.zeros((nblk, 256), F32)
    acc_v = jnp.zeros((nblk, 256), F32)
    for r in range(CMP_BLK):
        rows = pl.ds(r, nblk, stride=CMP_BLK)
        xk = (buf[slot, 0, rows, :] + posk_ref[r:r + 1, :]).astype(BF16)
        xv = (buf[slot, 1, rows, :] + posv_ref[r:r + 1, :]).astype(BF16)
        acc_k = acc_k + _dot(xk, bdk_ref[r])
        acc_v = acc_v + _dot(xv, bdv_ref[r])
    kc_ref[...] = _dot(_gelu(acc_k + b1k_ref[...]).astype(BF16), w2k_ref[...])
    vc_ref[...] = _dot(_gelu(acc_v + b1v_ref[...]).astype(BF16), w2v_ref[...])


def _compress(pt_flat, pool3, cw, P):
    n_pages = pt_flat.shape[0]
    nblk = P * (PAGE // CMP_BLK)
    full = lambda a: pl.BlockSpec(a.shape, lambda i, pt: (0,) * a.ndim)
    ws = (cw["bdk"], cw["bdv"], cw["posk"], cw["posv"], cw["b1k"], cw["b1v"], cw["w2k"], cw["w2v"])
    grid_spec = pltpu.PrefetchScalarGridSpec(
        num_scalar_prefetch=1, grid=(n_pages // P,),
        in_specs=[pl.BlockSpec(memory_space=pl.ANY)] + [full(a) for a in ws],
        out_specs=[pl.BlockSpec((nblk, 128), lambda i, pt: (i, 0)),
                   pl.BlockSpec((nblk, 128), lambda i, pt: (i, 0))],
        scratch_shapes=[pltpu.VMEM((2, 2, P * PAGE, LANES), F32), pltpu.SemaphoreType.DMA((2,))])
    return pl.pallas_call(
        functools.partial(_compress_body, P=P), grid_spec=grid_spec,
        out_shape=[jax.ShapeDtypeStruct((n_pages * 4, 128), F32)] * 2,
        compiler_params=_cparams(("arbitrary",)), name="compress",
    )(pt_flat, pool3, *ws)


def _half_mask(g):
    lane = lax.broadcasted_iota(I32, (1, LANES), 1)
    return (lane >= 64 * g) & (lane < 64 * g + 64)


def _gate_expand(gates, gexp_ref):
    hi, lo = _split2(gates)
    return [_dot(hi, gexp_ref[c]) + _dot(lo, gexp_ref[c]) for c in range(3)]


def _nsa_prompt_body(q_ref, sm_ref, kc_ref, vc_ref, kvs_ref, win_ref, eexp_ref, gexp_ref, o_ref, *, T):
    QT = 128
    q0 = pl.program_id(1) * QT
    qpos = q0 + lax.broadcasted_iota(I32, (QT, 1), 0)
    lane = lax.broadcasted_iota(I32, (1, LANES), 1)
    nce = T // CMP_BLK // 2
    blk = jnp.where(lane < nce, 2 * lane, 2 * (lane - nce) + 1)
    c_end = jnp.where(lane < 2 * nce, blk * CMP_BLK + (CMP_BLK - 1), 1 << 30)
    dist_c = qpos - c_end
    mask_c = dist_c >= 0
    dist_cf = dist_c.astype(F32)
    zpad = jnp.zeros((LANES - 2 * nce, LANES), F32)
    kc = jnp.concatenate([kc_ref[...], zpad], axis=0).astype(BF16)
    vc = jnp.concatenate([vc_ref[...], zpad], axis=0).astype(BF16)

    kpos = lax.broadcasted_iota(I32, (1, T), 1)
    dist_s = qpos - kpos
    causal_s = dist_s >= 0
    dist_sf = dist_s.astype(F32)
    w0 = jnp.maximum(q0 - WINDOW, 0)
    w0 = pl.multiple_of(w0, QT)
    wk = win_ref[pl.ds(w0, WINDOW + QT), 0:128]
    wv = win_ref[pl.ds(w0, WINDOW + QT), 128:256]
    dist_w = qpos - (w0 + lax.broadcasted_iota(I32, (1, WINDOW + QT), 1))
    mask_w = (dist_w >= 0) & (dist_w < WINDOW)
    dist_wf = dist_w.astype(F32)
    ks = kvs_ref[:, 0:128]
    vs = kvs_ref[:, 128:256]

    jrow = lax.broadcasted_iota(I32, (T // SLC_BLK, 1), 0)
    qpos_row = q0 + lane
    cur = qpos_row // SLC_BLK

    o_cmp = [[None] * NSA_R for _ in range(NSA_G)]
    o_slc = [[None] * NSA_R for _ in range(NSA_G)]
    o_win = [[None] * NSA_R for _ in range(NSA_G)]
    for g in range(NSA_G):
        hm = _half_mask(g)
        qs = [jnp.where(hm, q_ref[:, 128 * r:128 * r + 128], 0).astype(BF16) for r in range(NSA_R)]
        imp = jnp.zeros((QT, LANES), F32)
        for r in range(NSA_R):
            slope = SLOPES[g * NSA_R + r]
            s = _dot_nt(qs[r], kc) * (HD ** -0.5) - slope * dist_cf
            p = _masked_softmax(s, mask_c)
            imp = imp + p
            o_cmp[g][r] = _dot(p.astype(BF16), vc)
        imp_t = imp.T
        pooled = imp_t[0:nce] + imp_t[nce:2 * nce]
        valid = jrow <= cur
        forced = (jrow == 0) | (jrow == cur) | (jrow == cur - 1)
        score = jnp.where(valid, pooled + FORCE_BONUS * forced.astype(F32), NEG)
        ns = T // SLC_BLK
        cnt = jnp.zeros((ns, LANES), F32)
        for i2 in range(ns):
            si = score[i2:i2 + 1, :]
            beats = (si > score) | ((si == score) & (i2 < jrow))
            cnt = cnt + beats.astype(F32)
        sel_t = ((cnt < N_SEL) & (score > 0.5 * NEG)).astype(F32)
        sel = jnp.concatenate([sel_t, jnp.zeros((LANES - ns, LANES), F32)], axis=0).T
        sel_keys = _dot(sel.astype(BF16), eexp_ref[...]) > 0.5
        mask_s = causal_s & sel_keys
        for r in range(NSA_R):
            slope = SLOPES[g * NSA_R + r]
            s = _dot_nt(qs[r], ks) * (HD ** -0.5) - slope * dist_sf
            p = _masked_softmax(s, mask_s)
            o_slc[g][r] = _dot(p.astype(BF16), vs)
            s = _dot_nt(qs[r], wk) * (HD ** -0.5) - slope * dist_wf
            p = _masked_softmax(s, mask_w)
            o_win[g][r] = _dot(p.astype(BF16), wv)

    gates = _sigmoid(sm_ref[...])
    gc, gs, gw = _gate_expand(gates, gexp_ref)
    left = lane < 64
    for r in range(NSA_R):
        sl = slice(128 * r, 128 * r + 128)
        oc = jnp.where(left, o_cmp[0][r], o_cmp[1][r])
        os_ = jnp.where(left, o_slc[0][r], o_slc[1][r])
        ow = jnp.where(left, o_win[0][r], o_win[1][r])
        o_ref[:, sl] = gc[:, sl] * oc + gs[:, sl] * os_ + gw[:, sl] * ow


def _nsa_prompt(q, sm, kc_eo, vc_eo, kvs_bf, win_bf, eexp, gexp):
    b, t, _ = q.shape
    return pl.pallas_call(
        functools.partial(_nsa_prompt_body, T=t), grid=(b, t // 128),
        in_specs=[pl.BlockSpec((None, 128, 512), lambda i, j: (i, j, 0)),
                  pl.BlockSpec((None, 128, 128), lambda i, j: (i, j, 0)),
                  pl.BlockSpec((None, t // CMP_BLK, 128), lambda i, j: (i, 0, 0)),
                  pl.BlockSpec((None, t // CMP_BLK, 128), lambda i, j: (i, 0, 0)),
                  pl.BlockSpec((None, t, 256), lambda i, j: (i, 0, 0)),
                  pl.BlockSpec((None, t, 256), lambda i, j: (i, 0, 0)),
                  pl.BlockSpec(eexp.shape, lambda i, j: (0, 0)),
                  pl.BlockSpec(gexp.shape, lambda i, j: (0, 0, 0))],
        out_specs=pl.BlockSpec((None, 128, 512), lambda i, j: (i, j, 0)),
        out_shape=jax.ShapeDtypeStruct((b, t, 512), F32),
        compiler_params=_cparams(("parallel", "arbitrary")), name="nsa_prompt",
    )(q, sm, kc_eo, vc_eo, kvs_bf, win_bf, eexp, gexp)


def _nsa_sample_sel_body(q_ref, kc_ref, vc_ref, wbuf_ref, wnew_ref, oc_ref, ow_ref, idx_ref, *, past, nce):
    R8 = 8
    qpos = past + lax.broadcasted_iota(I32, (R8, 1), 0)
    lane_c = lax.broadcasted_iota(I32, (1, 2 * nce), 1)
    blk = jnp.where(lane_c < nce, 2 * lane_c, 2 * (lane_c - nce) + 1)
    dist_c = qpos - (blk * CMP_BLK + (CMP_BLK - 1))
    mask_c = dist_c >= 0
    dist_cf = dist_c.astype(F32)
    kc = kc_ref[...].astype(BF16)
    vc = vc_ref[...].astype(BF16)

    wb = wbuf_ref.shape[0]
    wk = wbuf_ref[:, 0:128].astype(BF16)
    wv = wbuf_ref[:, 128:256].astype(BF16)
    nk = wnew_ref[:, 0:128].astype(BF16)
    nv = wnew_ref[:, 128:256].astype(BF16)
    dist_w1 = qpos - (past - wb + lax.broadcasted_iota(I32, (1, wb), 1))
    mask_w1 = (dist_w1 >= 0) & (dist_w1 < WINDOW)
    dist_w2 = qpos - (past + lax.broadcasted_iota(I32, (1, R8), 1))
    mask_w2 = (dist_w2 >= 0) & (dist_w2 < WINDOW)

    ns_l = 3 * LANES
    n_sel_blk = nce + 1
    lane_s = lax.broadcasted_iota(I32, (1, ns_l), 1)
    cur = qpos // SLC_BLK
    lane16 = lax.broadcasted_iota(I32, (1, LANES), 1)
    left = lane16 < 64

    oc = [[None] * NSA_R for _ in range(NSA_G)]
    ow = [[None] * NSA_R for _ in range(NSA_G)]
    for g in range(NSA_G):
        hm = _half_mask(g)
        imp = jnp.zeros((R8, 2 * nce), F32)
        for r in range(NSA_R):
            slope = SLOPES[g * NSA_R + r]
            qh = jnp.where(hm, q_ref[:, 128 * r:128 * r + 128], 0).astype(BF16)
            s = _dot_nt(qh, kc) * (HD ** -0.5) - slope * dist_cf
            p = _masked_softmax(s, mask_c)
            imp = imp + p
            oc[g][r] = _dot(p.astype(BF16), vc)
            s1 = jnp.where(mask_w1, _dot_nt(qh, wk) * (HD ** -0.5) - slope * dist_w1.astype(F32), NEG)
            s2 = jnp.where(mask_w2, _dot_nt(qh, nk) * (HD ** -0.5) - slope * dist_w2.astype(F32), NEG)
            mx = jnp.maximum(jnp.max(s1, axis=1, keepdims=True), jnp.max(s2, axis=1, keepdims=True))
            e1 = jnp.exp(s1 - mx)
            e2 = jnp.exp(s2 - mx)
            den = jnp.sum(e1, axis=1, keepdims=True) + jnp.sum(e2, axis=1, keepdims=True)
            p1 = jnp.where(mask_w1, e1 / den, 0.0)
            p2 = jnp.where(mask_w2, e2 / den, 0.0)
            ow[g][r] = _dot(p1.astype(BF16), wv) + _dot(p2.astype(BF16), nv)
        pooled = imp[:, 0:nce] + imp[:, nce:2 * nce]
        pooled = jnp.concatenate([pooled, jnp.zeros((R8, ns_l - nce), F32)], axis=1)
        valid = lane_s <= cur
        forced = (lane_s == 0) | (lane_s == cur) | (lane_s == cur - 1)
        score = jnp.where(valid, pooled + FORCE_BONUS * forced.astype(F32), NEG)
        score = jnp.where(lane_s < n_sel_blk, score, -jnp.inf)
        idx_acc = jnp.zeros((R8, LANES), F32)
        lane_sf = lane_s.astype(F32)
        for n in range(N_SEL):
            mx = jnp.max(score, axis=1, keepdims=True)
            pick = jnp.min(jnp.where(score == mx, lane_sf, float(ns_l)), axis=1, keepdims=True)
            pick_v = jnp.where(mx > 0.5 * NEG, pick, -1.0)
            idx_acc = jnp.where(lane16 == n, pick_v, idx_acc)
            score = jnp.where(lane_sf == pick, -jnp.inf, score)
        idx_ref[g] = idx_acc.astype(I32)
    for r in range(NSA_R):
        sl = slice(128 * r, 128 * r + 128)
        oc_ref[:, sl] = jnp.where(left, oc[0][r], oc[1][r])
        ow_ref[:, sl] = jnp.where(left, ow[0][r], ow[1][r])


def _nsa_sample_sel(q8, kc_eo, vc_eo, wbuf, wnew8, past):
    b = q8.shape[0]
    nce2 = kc_eo.shape[1]
    wb = wbuf.shape[1]
    return pl.pallas_call(
        functools.partial(_nsa_sample_sel_body, past=past, nce=nce2 // 2), grid=(b,),
        in_specs=[pl.BlockSpec((None, 8, 512), lambda i: (i, 0, 0)),
                  pl.BlockSpec((None, nce2, 128), lambda i: (i, 0, 0)),
                  pl.BlockSpec((None, nce2, 128), lambda i: (i, 0, 0)),
                  pl.BlockSpec((None, wb, 256), lambda i: (i, 0, 0)),
                  pl.BlockSpec((None, 8, 256), lambda i: (i, 0, 0))],
        out_specs=[pl.BlockSpec((None, 8, 512), lambda i: (i, 0, 0)),
                   pl.BlockSpec((None, 8, 512), lambda i: (i, 0, 0)),
                   pl.BlockSpec((None, 2, 8, 128), lambda i: (i, 0, 0, 0))],
        out_shape=[jax.ShapeDtypeStruct((b, 8, 512), F32), jax.ShapeDtypeStruct((b, 8, 512), F32),
                   jax.ShapeDtypeStruct((b, 2, 8, 128), I32)],
        compiler_params=_cparams(("parallel",)), name="nsa_sample_sel",
    )(q8, kc_eo, vc_eo, wbuf, wnew8)


def _nsa_sample_slc_body(idx_ref, pt_ref, pool_ref, q_ref, knew_ref, sm_ref, oc_ref, ow_ref, gexp_ref,
                         o_ref, buf, sem, *, T, past, n_pages):
    b = pl.program_id(0)
    ns_past = past // SLC_BLK
    per_page = PAGE // SLC_BLK

    def blk_copy(g, t, n):
        j = idx_ref[((b * NSA_G + g) * 8 + t) * N_SEL + n]
        jc = jnp.clip(j, 0, ns_past - 1)
        phys = pt_ref[b * n_pages + jc // per_page]
        slot = (g * T + t) * N_SEL + n
        return pltpu.make_async_copy(
            pool_ref.at[phys, pl.ds(lax.rem(jc, per_page) * SLC_BLK, SLC_BLK), pl.ds(256, 256)],
            buf.at[slot], sem.at[0])

    for g in range(NSA_G):
        for t in range(T):
            for n in range(N_SEL):
                blk_copy(g, t, n).start()
    for g in range(NSA_G):
        for t in range(T):
            for n in range(N_SEL):
                blk_copy(g, t, n).wait()

    R8 = 8
    NK = N_SEL * SLC_BLK
    lane_k = lax.broadcasted_iota(I32, (1, NK), 1)
    slot_k = lane_k // SLC_BLK
    off_k = lane_k - slot_k * SLC_BLK
    lane = lax.broadcasted_iota(I32, (1, LANES), 1)
    rowi = lax.broadcasted_iota(I32, (R8, 1), 0)
    knew = knew_ref[:, 0:128].astype(BF16)
    vnew = knew_ref[:, 128:256].astype(BF16)
    kpos_new = past + lax.broadcasted_iota(I32, (1, R8), 1)
    o_acc = [jnp.zeros((R8, LANES), F32) for _ in range(NSA_R)]
    for g in range(NSA_G):
        hm = _half_mask(g)
        slope = jnp.zeros((R8, 1), F32)
        for r in range(NSA_R):
            slope = jnp.where(rowi == r, SLOPES[g * NSA_R + r], slope)
        for t in range(T):
            qpos = past + t
            qm = jnp.zeros((R8, LANES), F32)
            for r in range(NSA_R):
                qrow = q_ref[t:t + 1, 128 * r:128 * r + 128].astype(F32)
                qm = jnp.where(rowi == r, qrow, qm)
            qm = jnp.where(hm, qm, 0.0).astype(BF16)
            base = (g * T + t) * N_SEL
            kb = buf[pl.ds(base, N_SEL)].reshape(NK, 256)
            kpos = off_k
            ok_i = jnp.zeros((1, NK), I32)
            has_new = jnp.int32(0)
            for n in range(N_SEL):
                j = idx_ref[((b * NSA_G + g) * 8 + t) * N_SEL + n]
                in_slot = slot_k == n
                kpos = jnp.where(in_slot, j * SLC_BLK + off_k, kpos)
                ok_i = jnp.where(in_slot, ((j >= 0) & (j < ns_past)).astype(I32), ok_i)
                has_new = has_new + (j == ns_past).astype(I32)
            dist1 = qpos - kpos
            mask1 = (ok_i > 0) & (dist1 >= 0)
            dist2 = qpos - kpos_new
            mask2 = (dist2 >= 0) & ((jnp.zeros((1, R8), I32) + has_new) > 0)
            s1 = jnp.where(mask1, _dot_nt(qm, kb[:, 0:128].astype(BF16)) * (HD ** -0.5)
                           - slope * dist1.astype(F32), NEG)
            s2 = jnp.where(mask2, _dot_nt(qm, knew) * (HD ** -0.5) - slope * dist2.astype(F32), NEG)
            mx = jnp.maximum(jnp.max(s1, axis=1, keepdims=True), jnp.max(s2, axis=1, keepdims=True))
            e1 = jnp.exp(s1 - mx)
            e2 = jnp.exp(s2 - mx)
            den = jnp.sum(e1, axis=1, keepdims=True) + jnp.sum(e2, axis=1, keepdims=True)
            p1 = jnp.where(mask1, e1 / den, 0.0)
            p2 = jnp.where(mask2, e2 / den, 0.0)
            o = _dot(p1.astype(BF16), kb[:, 128:256].astype(BF16)) + _dot(p2.astype(BF16), vnew)
            for r in range(NSA_R):
                o_acc[r] = jnp.where((rowi == t) & hm, o[r:r + 1, :], o_acc[r])
    gates = _sigmoid(sm_ref[...])
    gc, gs, gw = _gate_expand(gates, gexp_ref)
    for r in range(NSA_R):
        sl = slice(128 * r, 128 * r + 128)
        o_ref[:, sl] = gc[:, sl] * oc_ref[:, sl] + gs[:, sl] * o_acc[r] + gw[:, sl] * ow_ref[:, sl]


def _nsa_sample_slc(idx_flat, pt_flat, pool3, q8, kvnew8, sm8, oc, ow, gexp, T, past):
    b = q8.shape[0]
    n_pages = pt_flat.shape[0] // b
    row = lambda w: pl.BlockSpec((None, 8, w), lambda i, a, c: (i, 0, 0))
    grid_spec = pltpu.PrefetchScalarGridSpec(
        num_scalar_prefetch=2, grid=(b,),
        in_specs=[pl.BlockSpec(memory_space=pl.ANY), row(512), row(256), row(128), row(512), row(512),
                  pl.BlockSpec(gexp.shape, lambda i, a, c: (0, 0, 0))],
        out_specs=row(512),
        scratch_shapes=[pltpu.VMEM((NSA_G * T * N_SEL, SLC_BLK, 256), F32), pltpu.SemaphoreType.DMA((1,))])
    return pl.pallas_call(
        functools.partial(_nsa_sample_slc_body, T=T, past=past, n_pages=n_pages), grid_spec=grid_spec,
        out_shape=jax.ShapeDtypeStruct((b, 8, 512), F32),
        compiler_params=_cparams(("arbitrary",)), name="nsa_sample_slc",
    )(idx_flat, pt_flat, pool3, q8, kvnew8, sm8, oc, ow, gexp)


def _mix_out_body(ya_ref, yb_ref, yc_ref, x_ref, g_ref, w_ref, pool_ref, poolt_ref, o_ref):
    y = jnp.concatenate([ya_ref[...], yb_ref[...], yc_ref[...]], axis=-1)
    ms = _dot_split(y * y, pool_ref[...])
    rb = _dot_split(lax.rsqrt(ms + EPS), poolt_ref[...])
    yn = (y * rb * g_ref[...]).astype(BF16)
    o_ref[...] = x_ref[...] + _dot(yn, w_ref[...])


def _mix_out(ya, yb, yc, x2, g, w_bf, pool, poolt, tm):
    n = x2.shape[0]
    row = lambda w: pl.BlockSpec((tm, w), lambda i: (i, 0))
    full = lambda a: pl.BlockSpec(a.shape, lambda i: (0,) * a.ndim)
    return pl.pallas_call(
        _mix_out_body, grid=(n // tm,),
        in_specs=[row(256), row(256), row(512), row(1024), full(g), full(w_bf), full(pool), full(poolt)],
        out_specs=row(1024), out_shape=jax.ShapeDtypeStruct((n, 1024), F32),
        compiler_params=_cparams(("parallel",)), name="mix_out",
    )(ya, yb, yc, x2, g, w_bf, pool, poolt)


def _xattn_body(q_ref, kv_ref, o_ref):
    for h in range(XA_H):
        k = kv_ref[:, 128 * h:128 * h + 128].astype(BF16)
        v = kv_ref[:, XA_INNER + 128 * h:XA_INNER + 128 * h + 128].astype(BF16)
        s = _dot_nt(q_ref[:, 128 * h:128 * h + 128], k) * (XA_HD ** -0.5)
        e = jnp.exp(s - jnp.max(s, axis=-1, keepdims=True))
        a = e / jnp.sum(e, axis=-1, keepdims=True)
        o_ref[:, 128 * h:128 * h + 128] = _dot(a.astype(BF16), v).astype(BF16)


def _xattn(q, mkv, tq):
    b, t, _ = q.shape
    return pl.pallas_call(
        _xattn_body, grid=(b, t // tq),
        in_specs=[pl.BlockSpec((None, tq, 512), lambda i, j: (i, j, 0)),
                  pl.BlockSpec((None, N_MEM, 1024), lambda i, j: (i, 0, 0))],
        out_specs=pl.BlockSpec((None, tq, 512), lambda i, j: (i, j, 0)),
        out_shape=jax.ShapeDtypeStruct((b, t, 512), BF16),
        compiler_params=_cparams(("parallel", "arbitrary")), name="xattn",
    )(q, mkv)


def _moe_body(x_ref, g_ref, rwh_ref, rwl_ref, rb_ref, wg_ref, wu_ref, wd_ref, o_ref, zn_s, comb_s, acc_s):
    e = pl.program_id(1)
    lane = lax.broadcasted_iota(I32, (1, LANES), 1)

    @pl.when(e == 0)
    def _():
        z = _rms(x_ref[...], g_ref[...])
        zh, zl = _split2(z)
        zn_s[...] = zh
        logits = _dot(zh, rwh_ref[...]) + _dot(zh, rwl_ref[...]) + _dot(zl, rwh_ref[...]) + rb_ref[...]
        is_g = (lane >= N_EXP) & (lane < N_EXP + N_GROUPS)
        gl = jnp.where(is_g, logits, -jnp.inf)
        gmax = jnp.max(gl, axis=-1, keepdims=True)
        gw = 1.0 / jnp.sum(jnp.exp(gl - gmax), axis=-1, keepdims=True)
        lanef = lane.astype(F32)
        grpf = (lane // EXP_PER_GROUP).astype(F32)
        gsel = jnp.min(jnp.where(gl == gmax, lanef, 1e6), axis=-1, keepdims=True) - N_EXP
        in_grp = (lane < N_EXP) & (grpf == gsel)
        le = jnp.where(in_grp, logits, -jnp.inf)
        v1 = jnp.max(le, axis=-1, keepdims=True)
        i1 = jnp.min(jnp.where(le == v1, lanef, 1e6), axis=-1, keepdims=True)
        le2 = jnp.where(lanef == i1, -jnp.inf, le)
        v2 = jnp.max(le2, axis=-1, keepdims=True)
        i2 = jnp.min(jnp.where(le2 == v2, lanef, 1e6), axis=-1, keepdims=True)
        e2 = jnp.exp(v2 - v1)
        w1 = 1.0 / (1.0 + e2)
        w2 = e2 / (1.0 + e2)
        comb_s[...] = gw * (jnp.where(lanef == i1, w1, 0.0) + jnp.where(lanef == i2, w2, 0.0))
        acc_s[...] = jnp.zeros_like(acc_s)

    zn = zn_s[...]
    hg = _dot(zn, wg_ref[...])
    hu = _dot(zn, wu_ref[...])
    ce = jnp.sum(jnp.where(lane == e, comb_s[...], 0.0), axis=-1, keepdims=True)
    h = (hg * _sigmoid(hg)) * hu * ce
    acc_s[...] += _dot(h.astype(BF16), wd_ref[...])

    @pl.when(e == pl.num_programs(1) - 1)
    def _():
        o_ref[...] = x_ref[...] + acc_s[...]


def _moe(x2, g, rwh, rwl, rb, wg, wu, wd, tm):
    n = x2.shape[0]
    full = lambda a: pl.BlockSpec(a.shape, lambda i, e: (0,) * a.ndim)
    return pl.pallas_call(
        _moe_body, grid=(n // tm, N_EXP),
        in_specs=[pl.BlockSpec((tm, 1024), lambda i, e: (i, 0)), full(g), full(rwh), full(rwl), full(rb),
                  pl.BlockSpec((None, 1024, EXP_FF), lambda i, e: (e, 0, 0)),
                  pl.BlockSpec((None, 1024, EXP_FF), lambda i, e: (e, 0, 0)),
                  pl.BlockSpec((None, EXP_FF, 1024), lambda i, e: (e, 0, 0))],
        out_specs=pl.BlockSpec((tm, 1024), lambda i, e: (i, 0)),
        out_shape=jax.ShapeDtypeStruct((n, 1024), F32),
        scratch_shapes=[pltpu.VMEM((tm, 1024), BF16), pltpu.VMEM((tm, 128), F32), pltpu.VMEM((tm, 1024), F32)],
        compiler_params=_cparams(("parallel", "arbitrary")), name="moe",
    )(x2, g, rwh, rwl, rb, wg, wu, wd)


def _q_perm():
    idx = []
    for r in range(NSA_R):
        for g in range(NSA_G):
            h = g * NSA_R + r
            idx.extend(range(64 * h, 64 * h + 64))
    return np.asarray(idx)


def _prep_layer(w, l):
    p = {}
    win = w["w_in"][l]
    qp = _q_perm()
    o_q = 1032 + 512
    cols = [win[:, 0:1024], win[:, 1032:1032 + 512], win[:, o_q:o_q + 512][:, qp],
            win[:, o_q + 512:o_q + 512 + 768], win[:, 1024:1032], win[:, o_q + 1280:o_q + 1304],
            jnp.zeros((D_MODEL, 128 - 32), F32)]
    p["w_in"] = jnp.concatenate(cols, axis=1).astype(BF16)
    p["w_st"] = win[:, 1024:1032].T.astype(BF16)
    p["norm_mix"] = w["norm_mix"][l][None]
    bias8 = jnp.concatenate([w["ml_i_bias"][l], w["ml_f_bias"][l]])
    p["ml_bcol"] = jnp.zeros((1, 128), F32).at[0, 0:8].set(bias8)
    p["ml_brow"] = bias8[:, None]
    p["conv_w"] = w["conv_w"][l]
    p["conv_b"] = w["conv_b"][l][None]
    bd = lambda m: jax.scipy.linalg.block_diag(*[m[i] for i in range(m.shape[0])])
    p["lru_wa"] = bd(w["lru_wa"][l]).astype(BF16)
    p["lru_wx"] = bd(w["lru_wx"][l]).astype(BF16)
    p["lru_ba"] = w["lru_ba"][l][None]
    p["lru_bx"] = w["lru_bx"][l][None]
    p["lru_lambda"] = w["lru_lambda"][l][None]
    cw = {}
    for c, nm in ((0, "k"), (1, "v")):
        w1 = w["phi_w1"][l, c].reshape(CMP_BLK, HD, CMP_HID)
        z = jnp.zeros_like(w1)
        cw["bd" + nm] = jnp.concatenate([jnp.concatenate([w1, z], axis=2),
                                         jnp.concatenate([z, w1], axis=2)], axis=1).astype(BF16)
        cw["pos" + nm] = jnp.tile(w["cmp_pos"][l, c], (1, 2))
        cw["b1" + nm] = jnp.tile(w["phi_b1"][l, c], 2)[None]
        w2 = w["phi_w2"][l, c]
        z2 = jnp.zeros_like(w2)
        cw["w2" + nm] = jnp.concatenate([jnp.concatenate([w2, z2], axis=1),
                                         jnp.concatenate([z2, w2], axis=1)], axis=0).astype(BF16)
    p["cmp"] = cw
    perm = np.concatenate([np.arange(512), 512 + qp])
    p["mix_norm"] = w["mix_norm"][l][perm][None]
    p["w_out"] = w["w_out"][l][perm, :].astype(BF16)
    p["norm_xa"] = w["norm_xa"][l][None]
    p["norm_mem"] = w["norm_mem"][l][None]
    p["xa_wq"] = w["xa_wq"][l].astype(BF16)
    p["xa_wkv"] = w["xa_wkv"][l].astype(BF16)
    p["xa_wo"] = w["xa_wo"][l].astype(BF16)
    p["norm_ffn"] = w["norm_ffn"][l][None]
    rw = jnp.concatenate([w["router_ew"][l], w["router_gw"][l], jnp.zeros((D_MODEL, 128 - 20), F32)], axis=1)
    rwh = rw.astype(BF16)
    p["rwh"] = rwh
    p["rwl"] = (rw - rwh.astype(F32)).astype(BF16)
    p["rb"] = jnp.concatenate([w["router_eb"][l], w["router_gb"][l], jnp.zeros((128 - 20,), F32)])[None]
    p["wg"] = w["exp_w_gate"][l].astype(BF16)
    p["wu"] = w["exp_w_up"][l].astype(BF16)
    p["wd"] = w["exp_w_down"][l].astype(BF16)
    return p


def _constants(T):
    c = {}
    head = np.arange(1024) // HD
    pool = np.zeros((1024, 128), np.float32)
    pool[np.arange(1024), head] = 1.0 / HD
    poolt = np.zeros((128, 1024), np.float32)
    poolt[head, np.arange(1024)] = 1.0
    c["pool"] = jnp.asarray(pool, BF16)
    c["poolt"] = jnp.asarray(poolt, BF16)
    gexp = np.zeros((3, 128, 512), np.float32)
    for cc in range(3):
        for g in range(NSA_G):
            for r in range(NSA_R):
                gexp[cc, 8 + cc * 8 + g * 4 + r, 128 * r + 64 * g:128 * r + 64 * g + 64] = 1.0
    c["gexp"] = jnp.asarray(gexp, BF16)
    eexp = np.zeros((128, T), np.float32)
    eexp[np.arange(T) // SLC_BLK, np.arange(T)] = 1.0
    c["eexp"] = jnp.asarray(eexp, BF16)
    return c


def _even_odd(a):
    return jnp.concatenate([a[:, 0::2], a[:, 1::2]], axis=1)


def _pad_rows(a, rows):
    return jnp.pad(a, ((0, 0), (0, rows - a.shape[1]), (0, 0)))


def _dense_tail(x2, p, B, T, mkv, tm, tq, tm_moe):
    n = x2.shape[0]
    q = _norm_mm(x2, p["norm_xa"], p["xa_wq"], tm, BF16).reshape(B, T, XA_INNER)
    if T < tq:
        o = _xattn(_pad_rows(q, tq), mkv, tq)[:, 0:T]
    else:
        o = _xattn(q, mkv, tq)
    x2 = _mm_res(o.reshape(n, XA_INNER), p["xa_wo"], x2, tm)
    return _moe(x2, p["norm_ffn"], p["rwh"], p["rwl"], p["rb"], p["wg"], p["wu"], p["wd"], tm_moe)


def _layer_prompt(x2, p, c, B, T, mem2):
    n = B * T
    ml, lru, q, kv, win, sm, smt, kvs_bf, win_bf = _in_proj(x2, p["norm_mix"], p["w_in"], p["w_st"], 256)
    smt_b = smt.reshape(8, B, T).transpose(1, 0, 2)
    ya, C, nn, mm = _mlstm(ml.reshape(B, T, 1024), sm.reshape(B, T, 128), smt_b, p["ml_bcol"], p["ml_brow"],
                           jnp.zeros((B, 4, 64, 64), F32), jnp.zeros((B, 4, 1, 64), F32),
                           jnp.zeros((B, 1, 128), F32), 128, 128)
    u_tm = lru.reshape(B, T, 512).transpose(1, 0, 2)
    yb_tm, tail_tm, h_last = _lru(u_tm, jnp.zeros((3, B, 256), F32), jnp.zeros((B, 256), F32), p["conv_w"],
                                  p["conv_b"], p["lru_wa"], p["lru_ba"], p["lru_wx"], p["lru_bx"],
                                  p["lru_lambda"], 256)
    yb = yb_tm.transpose(1, 0, 2).reshape(n, 256)
    n_pages = n // PAGE
    kc, vc = _compress(jnp.arange(n_pages, dtype=I32), kv.reshape(n_pages, PAGE, 512), p["cmp"], 16)
    nc = T // CMP_BLK
    kc_eo = _even_odd(kc.reshape(B, nc, 128))
    vc_eo = _even_odd(vc.reshape(B, nc, 128))
    yc = _nsa_prompt(q.reshape(B, T, 512), sm.reshape(B, T, 128), kc_eo, vc_eo, kvs_bf.reshape(B, T, 256),
                     win_bf.reshape(B, T, 256), c["eexp"], c["gexp"])
    x2 = _mix_out(ya.reshape(n, 256), yb, yc.reshape(n, 512), x2, p["mix_norm"], p["w_out"], c["pool"],
                  c["poolt"], 256)
    mkv = _norm_mm(mem2, p["norm_mem"], p["xa_wkv"], 256, F32)
    x2 = _dense_tail(x2, p, B, T, mkv.reshape(B, N_MEM, 1024), 256, 512, 1024)
    st = (kv.reshape(B, T, 4, NSA_G, HD), win.reshape(B, T, 2, NSA_G, HD)[:, T - WINDOW:],
          C, nn.reshape(B, 4, 64), mm[:, 0, 0:4], h_last, tail_tm.transpose(1, 0, 2))
    return x2, st, mkv.reshape(B, N_MEM, 2, XA_H, XA_HD)


def _layer_sample(x2, p, c, B, T, pool3, pt_flat, win_buf, C0, n0, m0, conv0, h0, mkv, past):
    n = B * T
    _pad_rows8 = lambda a: _pad_rows(a, 8)
    ml, lru, q, kv, win, sm, smt, _, _ = _in_proj(x2, p["norm_mix"], p["w_in"], p["w_st"], n)
    sm8 = _pad_rows8(sm.reshape(B, T, 128))
    smt_b = jnp.pad(smt.reshape(8, B, T).transpose(1, 0, 2), ((0, 0), (0, 0), (0, 8 - T)))
    m0p = jnp.pad(m0[:, None, :], ((0, 0), (0, 0), (0, 128 - ML_H)))
    ya, C, nn, mm = _mlstm(_pad_rows8(ml.reshape(B, T, 1024)), sm8, smt_b, p["ml_bcol"], p["ml_brow"],
                           C0, n0[:, :, None, :], m0p, 8, T)
    ya = ya[:, 0:T]
    u_tm = lru.reshape(B, T, 512).transpose(1, 0, 2)
    yb_tm, tail_tm, h_last = _lru(u_tm, conv0.transpose(1, 0, 2), h0, p["conv_w"], p["conv_b"], p["lru_wa"],
                                  p["lru_ba"], p["lru_wx"], p["lru_bx"], p["lru_lambda"], T)
    yb = yb_tm.transpose(1, 0, 2).reshape(n, 256)
    kc, vc = _compress(pt_flat, pool3, p["cmp"], 64)
    nc = past // CMP_BLK
    kc_eo = _even_odd(kc.reshape(B, nc, 128))
    vc_eo = _even_odd(vc.reshape(B, nc, 128))
    wb = win_buf.shape[1]
    q8 = _pad_rows8(q.reshape(B, T, 512).astype(F32))
    win3 = win.reshape(B, T, 256)
    oc, ow, idx = _nsa_sample_sel(q8, kc_eo, vc_eo, win_buf.reshape(B, wb, 256), _pad_rows8(win3), past)
    idx_flat = idx[:, :, :, 0:N_SEL].reshape(-1)
    kv3 = kv.reshape(B, T, 512)
    yc8 = _nsa_sample_slc(idx_flat, pt_flat, pool3, q8, _pad_rows8(kv3[:, :, 256:512]), sm8, oc, ow,
                          c["gexp"], T, past)
    yc = yc8[:, 0:T].reshape(n, 512)
    x2 = _mix_out(ya.reshape(n, 256), yb, yc, x2, p["mix_norm"], p["w_out"], c["pool"], c["poolt"], n)
    x2 = _dense_tail(x2, p, B, T, mkv.reshape(B, N_MEM, 1024), n, 16, n)
    win_new = jnp.concatenate([win_buf.reshape(B, wb, 256), win3], axis=1)[:, T:]
    st = (kv.reshape(B, T, 4, NSA_G, HD), win_new.reshape(B, wb, 2, NSA_G, HD), C, nn.reshape(B, 4, 64),
          mm[:, 0, 0:4], h_last, tail_tm.transpose(1, 0, 2))
    return x2, st


def kernel(x_prompt, x_sample, cache_nsa_kv, state_nsa_win, state_mlstm_C, state_mlstm_n, state_mlstm_m, state_rglru_h, state_rglru_conv, cache_mem_kv, page_table, mem_prompt, norm_mix, w_in, ml_i_bias, ml_f_bias, conv_w, conv_b, lru_wa, lru_ba, lru_wx, lru_bx, lru_lambda, phi_w1, phi_b1, phi_w2, cmp_pos, mix_norm, w_out, norm_xa, norm_mem, xa_wq, xa_wkv, xa_wo, norm_ffn, router_gw, router_gb, router_ew, router_eb, exp_w_gate, exp_w_up, exp_w_down, final_norm):
    w = dict(norm_mix=norm_mix, w_in=w_in, ml_i_bias=ml_i_bias, ml_f_bias=ml_f_bias, conv_w=conv_w,
             conv_b=conv_b, lru_wa=lru_wa, lru_ba=lru_ba, lru_wx=lru_wx, lru_bx=lru_bx, lru_lambda=lru_lambda,
             phi_w1=phi_w1, phi_b1=phi_b1, phi_w2=phi_w2, cmp_pos=cmp_pos, mix_norm=mix_norm, w_out=w_out,
             norm_xa=norm_xa, norm_mem=norm_mem, xa_wq=xa_wq, xa_wkv=xa_wkv, xa_wo=xa_wo, norm_ffn=norm_ffn,
             router_gw=router_gw, router_gb=router_gb, router_ew=router_ew, router_eb=router_eb,
             exp_w_gate=exp_w_gate, exp_w_up=exp_w_up, exp_w_down=exp_w_down)
    depth = w_in.shape[0]
    B, T, _ = x_prompt.shape
    BS, TS, _ = x_sample.shape
    n_pages = page_table.shape[1]
    past = n_pages * PAGE
    n_phys = cache_nsa_kv.shape[1]
    consts = _constants(T)
    layers = [_prep_layer(w, l) for l in range(depth)]
    fin = final_norm[None]

    x2 = x_prompt.reshape(B * T, D_MODEL)
    mem2 = mem_prompt.reshape(B * N_MEM, D_MODEL)
    outs_p, mem_p = [], []
    for l in range(depth):
        x2, st, mkv = _layer_prompt(x2, layers[l], consts, B, T, mem2)
        outs_p.append(st)
        mem_p.append(mkv)
    y_prompt = _final_norm(x2, fin, 256).reshape(B, T, D_MODEL)

    xs = x_sample.reshape(BS * TS, D_MODEL)
    pool3 = cache_nsa_kv.reshape(depth * n_phys, PAGE, 512)
    outs_s = []
    for l in range(depth):
        pt_flat = page_table.reshape(-1) + l * n_phys
        xs, st = _layer_sample(xs, layers[l], consts, BS, TS, pool3, pt_flat, state_nsa_win[l],
                               state_mlstm_C[l], state_mlstm_n[l], state_mlstm_m[l], state_rglru_conv[l],
                               state_rglru_h[l], cache_mem_kv[l], past)
        outs_s.append(st)
    y_sample = _final_norm(xs, fin, BS * TS).reshape(BS, TS, D_MODEL)

    sp = [jnp.stack(a) for a in zip(*outs_p)]
    ss = [jnp.stack(a) for a in zip(*outs_s)]
    return (y_prompt, y_sample, sp[0], ss[0], sp[1], ss[1], sp[2], ss[2], sp[3], ss[3], sp[4], ss[4],
            sp[5], ss[5], sp[6], ss[6], jnp.stack(mem_p))
```

```python
import functools
import math

import jax
import jax.numpy as jnp
import numpy as np
from jax import lax
from jax.experimental import pallas as pl
from jax.experimental.pallas import tpu as pltpu

F32 = jnp.float32
BF16 = jnp.bfloat16
I32 = jnp.int32

D_MODEL = 1024
DEPTH = 2
PAGE = 128
HD = 64
ML_W = 256
ML_H = 4
LRU_W = 256
LRU_C = 8.0
CONV_W = 4
NSA_W = 512
NSA_H = 8
NSA_G = 2
NSA_R = 4
CMP_BLK = 32
CMP_HID = 128
SLC_BLK = 64
N_SEL = 16
WINDOW = 512
FORCE_BONUS = 100.0
XA_H = 4
XA_HD = 128
XA_INNER = 512
N_MEM = 256
N_GROUPS = 4
EXP_PER_GROUP = 4
N_EXP = 16
EXP_FF = 256
EPS = 1e-6
NEG = -1e30
SLOPES = tuple(2.0 ** (-(h + 1)) for h in range(NSA_H))

LANES = 128
SUBLANES = 8
VMEM_LIMIT = 56 * 1024 * 1024

IN_ML = (0, 1024)
IN_LRU = (1024, 1536)
IN_Q = (1536, 2048)
IN_KV = (2048, 2560)
IN_WIN = (2560, 2816)
IN_SMALL = (2816, 2944)
IN_COLS_P = 2944

AUG_HI = 0
AUG_LO = 1
AUG_ONE = 2
AUG_SEL = 8
KEY_CHUNK = 256


def _cparams(sem):
    return pltpu.CompilerParams(dimension_semantics=sem, vmem_limit_bytes=VMEM_LIMIT)


def _dot(a, b):
    return jnp.dot(a, b, preferred_element_type=F32)


def _dot_nt(a, b):
    return lax.dot_general(a, b, (((1,), (1,)), ((), ())), preferred_element_type=F32)


def _dot_tn(a, b):
    return lax.dot_general(a, b, (((0,), (0,)), ((), ())), preferred_element_type=F32)


def _split2(x):
    hi = x.astype(BF16)
    lo = (x - hi.astype(F32)).astype(BF16)
    return hi, lo


def _dot_split(x, w_bf):
    hi, lo = _split2(x)
    return _dot(hi, w_bf) + _dot(lo, w_bf)


def _sigmoid(x):
    return 1.0 / (1.0 + jnp.exp(-x))


def _gelu(x):
    return 0.5 * x * (1.0 + jnp.tanh(0.7978845608028654 * (x + 0.044715 * (x * x * x))))


def _softplus(x):
    return jnp.maximum(x, 0.0) + jnp.log1p(jnp.exp(-jnp.abs(x)))


def _log_sigmoid(x):
    return -_softplus(-x)


def _rms(x, g):
    return x * lax.rsqrt(jnp.mean(x * x, axis=-1, keepdims=True) + EPS) * g


def _masked_softmax(s, mask):
    sm = jnp.where(mask, s, NEG)
    e = jnp.exp(sm - jnp.max(sm, axis=-1, keepdims=True))
    p = e / jnp.sum(e, axis=-1, keepdims=True)
    return jnp.where(mask, p, 0.0)


def _in_proj_body(x_ref, g_ref, w_ref, wst_ref, ml_ref, lru_ref, q_ref, kv_ref, win_ref, sm_ref,
                  smt_ref, kvsb_ref, winb_ref):
    hb = _rms(x_ref[...], g_ref[...]).astype(BF16)

    def mm(rng):
        return _dot(hb, w_ref[:, rng[0]:rng[1]])

    ml_ref[...] = mm(IN_ML)
    lru_ref[...] = mm(IN_LRU)
    q_ref[...] = mm(IN_Q).astype(BF16)
    kv = mm(IN_KV)
    kv_ref[...] = kv
    kvsb_ref[...] = kv[:, 256:512].astype(BF16)
    win = mm(IN_WIN)
    win_ref[...] = win
    winb_ref[...] = win.astype(BF16)
    sm_ref[...] = mm(IN_SMALL)
    smt_ref[...] = _dot_nt(wst_ref[...], hb)


def _in_proj(x2, g, w_p, w_st, tm):
    n = x2.shape[0]
    row = lambda w: pl.BlockSpec((tm, w), lambda i: (i, 0))
    full = lambda a: pl.BlockSpec(a.shape, lambda i: (0,) * a.ndim)
    shapes = [(1024, F32), (512, F32), (512, BF16), (512, F32), (256, F32), (128, F32)]
    out_shape = [jax.ShapeDtypeStruct((n, w), dt) for w, dt in shapes]
    out_specs = [row(w) for w, _ in shapes]
    out_shape += [jax.ShapeDtypeStruct((8, n), F32), jax.ShapeDtypeStruct((n, 256), BF16),
                  jax.ShapeDtypeStruct((n, 256), BF16)]
    out_specs += [pl.BlockSpec((8, tm), lambda i: (0, i)), row(256), row(256)]
    return pl.pallas_call(
        _in_proj_body, grid=(n // tm,),
        in_specs=[row(D_MODEL), full(g), full(w_p), full(w_st)],
        out_specs=out_specs, out_shape=out_shape,
        compiler_params=_cparams(("parallel",)), name="in_proj",
    )(x2, g, w_p, w_st)


def _norm_mm_body(x_ref, g_ref, w_ref, o_ref):
    o_ref[...] = _dot(_rms(x_ref[...], g_ref[...]).astype(BF16), w_ref[...]).astype(o_ref.dtype)


def _norm_mm(x2, g, w_bf, tm, out_dtype):
    n, k = x2.shape
    m = w_bf.shape[1]
    return pl.pallas_call(
        _norm_mm_body, grid=(n // tm,),
        in_specs=[pl.BlockSpec((tm, k), lambda i: (i, 0)), pl.BlockSpec((1, k), lambda i: (0, 0)),
                  pl.BlockSpec((k, m), lambda i: (0, 0))],
        out_specs=pl.BlockSpec((tm, m), lambda i: (i, 0)),
        out_shape=jax.ShapeDtypeStruct((n, m), out_dtype),
        compiler_params=_cparams(("parallel",)), name="norm_mm",
    )(x2, g, w_bf)


def _mm_res_body(a_ref, w_ref, x_ref, o_ref):
    o_ref[...] = x_ref[...] + _dot(a_ref[...], w_ref[...])


def _mm_res(a_bf, w_bf, x2, tm):
    n, k = a_bf.shape
    m = w_bf.shape[1]
    return pl.pallas_call(
        _mm_res_body, grid=(n // tm,),
        in_specs=[pl.BlockSpec((tm, k), lambda i: (i, 0)), pl.BlockSpec((k, m), lambda i: (0, 0)),
                  pl.BlockSpec((tm, m), lambda i: (i, 0))],
        out_specs=pl.BlockSpec((tm, m), lambda i: (i, 0)),
        out_shape=jax.ShapeDtypeStruct((n, m), F32),
        compiler_params=_cparams(("parallel",)), name="mm_res",
    )(a_bf, w_bf, x2)


def _final_norm_body(x_ref, g_ref, o_ref):
    o_ref[...] = _rms(x_ref[...], g_ref[...])


def _final_norm(x2, g, tm):
    n, k = x2.shape
    return pl.pallas_call(
        _final_norm_body, grid=(n // tm,),
        in_specs=[pl.BlockSpec((tm, k), lambda i: (i, 0)), pl.BlockSpec((1, k), lambda i: (0, 0))],
        out_specs=pl.BlockSpec((tm, k), lambda i: (i, 0)),
        out_shape=jax.ShapeDtypeStruct((n, k), F32),
        compiler_params=_cparams(("parallel",)), name="final_norm",
    )(x2, g)


def _mlstm_body(u_ref, sm_ref, smt_ref, bcol_ref, brow_ref, c0_ref, n0_ref, m0_ref,
                y_ref, c_ref, n_ref, m_ref, c_s, n_s, m_s, *, L, t_real):
    ci = pl.program_id(1)

    @pl.when(ci == 0)
    def _():
        c_s[...] = c0_ref[...]
        n_s[...] = n0_ref[...]
        m_s[...] = m0_ref[...]

    sm = sm_ref[...] + bcol_ref[...]
    smt = smt_ref[...] + brow_ref[...]
    row = lax.broadcasted_iota(I32, (L, L), 0)
    col = lax.broadcasted_iota(I32, (L, L), 1)
    causal = col <= row
    lane = lax.broadcasted_iota(I32, (1, LANES), 1)
    real_col = lax.broadcasted_iota(I32, (L, 1), 0) < t_real
    real_row = lax.broadcasted_iota(I32, (1, L), 1) < t_real
    m_all = m_s[...]
    m_next = m_all
    for h in range(ML_H):
        q = u_ref[:, 64 * h:64 * h + 64]
        k = u_ref[:, 256 + 64 * h:256 + 64 * h + 64] * (HD ** -0.5)
        v = u_ref[:, 512 + 64 * h:512 + 64 * h + 64]
        o = u_ref[:, 768 + 64 * h:768 + 64 * h + 64]
        ig_col = jnp.where(real_col, sm[:, h:h + 1], NEG)
        lf_col = jnp.where(real_col, _log_sigmoid(sm[:, 4 + h:5 + h]), 0.0)
        ig_row = jnp.where(real_row, smt[h:h + 1, :], NEG)
        lf_row = jnp.where(real_row, _log_sigmoid(smt[4 + h:5 + h, :]), 0.0)
        b_col = jnp.sum(jnp.where(causal, lf_row, 0.0), axis=1, keepdims=True)
        b_row = jnp.sum(jnp.where(row <= col, lf_col, 0.0), axis=0, keepdims=True)
        m_prev = m_all[:, h:h + 1]
        log_d = jnp.where(causal, b_col - b_row + ig_row, NEG)
        inter = b_col + m_prev
        m_t = jnp.maximum(inter, jnp.max(log_d, axis=1, keepdims=True))
        w_carry = jnp.exp(inter - m_t)
        qb = q.astype(BF16)
        vb = v.astype(BF16)
        s = _dot_nt(qb, k.astype(BF16)) * jnp.exp(log_d - m_t)
        c_old = c_s[h]
        n_old = n_s[h]
        num = _dot(s.astype(BF16), vb) + w_carry * _dot_nt(qb, c_old.astype(BF16))
        den = jnp.sum(s, axis=1, keepdims=True) + w_carry * jnp.sum(q * n_old, axis=1, keepdims=True)
        hh = num / jnp.maximum(jnp.abs(den), jnp.exp(-m_t))
        y_ref[:, 64 * h:64 * h + 64] = _sigmoid(o) * hh
        b_end = b_col[L - 1:L, :]
        log_w_row = b_end - b_row + ig_row
        m_new = jnp.maximum(b_end + m_prev, jnp.max(log_w_row, axis=1, keepdims=True))
        w_col = jnp.exp(b_end - b_col + ig_col - m_new)
        decay = jnp.exp(b_end + m_prev - m_new)
        c_s[h] = decay * c_old + _dot_tn((v * w_col).astype(BF16), k.astype(BF16))
        n_s[h] = decay * n_old + jnp.sum(k * w_col, axis=0, keepdims=True)
        m_next = jnp.where(lane == h, m_new, m_next)
    m_s[...] = m_next

    @pl.when(ci == pl.num_programs(1) - 1)
    def _():
        c_ref[...] = c_s[...]
        n_ref[...] = n_s[...]
        m_ref[...] = m_s[...]


def _mlstm(u_ml, u_sm, u_smt, bcol, brow, c0, n0, m0, L, t_real):
    b, t, _ = u_ml.shape
    nc = t // L
    assert t_real == L or nc == 1
    return pl.pallas_call(
        functools.partial(_mlstm_body, L=L, t_real=t_real), grid=(b, nc),
        in_specs=[pl.BlockSpec((None, L, 1024), lambda i, c: (i, c, 0)),
                  pl.BlockSpec((None, L, 128), lambda i, c: (i, c, 0)),
                  pl.BlockSpec((None, 8, L), lambda i, c: (i, 0, c)),
                  pl.BlockSpec((1, 128), lambda i, c: (0, 0)),
                  pl.BlockSpec((8, 1), lambda i, c: (0, 0)),
                  pl.BlockSpec((None, 4, 64, 64), lambda i, c: (i, 0, 0, 0)),
                  pl.BlockSpec((None, 4, 1, 64), lambda i, c: (i, 0, 0, 0)),
                  pl.BlockSpec((None, 1, 128), lambda i, c: (i, 0, 0))],
        out_specs=[pl.BlockSpec((None, L, 256), lambda i, c: (i, c, 0)),
                   pl.BlockSpec((None, 4, 64, 64), lambda i, c: (i, 0, 0, 0)),
                   pl.BlockSpec((None, 4, 1, 64), lambda i, c: (i, 0, 0, 0)),
                   pl.BlockSpec((None, 1, 128), lambda i, c: (i, 0, 0))],
        out_shape=[jax.ShapeDtypeStruct((b, t, 256), F32), jax.ShapeDtypeStruct((b, 4, 64, 64), F32),
                   jax.ShapeDtypeStruct((b, 4, 1, 64), F32), jax.ShapeDtypeStruct((b, 1, 128), F32)],
        scratch_shapes=[pltpu.VMEM((4, 64, 64), F32), pltpu.VMEM((4, 1, 64), F32),
                        pltpu.VMEM((1, 128), F32)],
        compiler_params=_cparams(("parallel", "arbitrary")), name="mlstm",
    )(u_ml, u_sm, u_smt, bcol, brow, c0, n0, m0)


def _lru_body(u_ref, cb_ref, h0_ref, cw_ref, cbias_ref, wa_ref, ba_ref, wx_ref, bx_ref, lam_ref,
              y_ref, tail_ref, hl_ref, tail_s, h_s, a_s, hs_s, *, tt, nb):
    i = pl.program_id(0)

    @pl.when(i == 0)
    def _():
        tail_s[...] = cb_ref[...]
        h_s[...] = h0_ref[...]

    x = u_ref[:, :, 0:LRU_W]
    g = u_ref[:, :, LRU_W:2 * LRU_W]
    xe = jnp.concatenate([tail_s[...], x], axis=0)
    xc = cbias_ref[...] + xe[0:tt] * cw_ref[0:1, :]
    for j in range(1, CONV_W):
        xc = xc + xe[j:j + tt] * cw_ref[j:j + 1, :]
    xc2 = xc.reshape(tt * nb, LRU_W)
    xcb = xc2.astype(BF16)
    r = _sigmoid(_dot(xcb, wa_ref[...]) + ba_ref[...])
    ig = _sigmoid(_dot(xcb, wx_ref[...]) + bx_ref[...])
    log_a = -LRU_C * r * _softplus(-lam_ref[...])
    a = jnp.exp(log_a)
    mult = jnp.sqrt(jnp.tanh(-log_a) * (a * a + 1.0))
    a_s[...] = a.reshape(tt, nb, LRU_W)
    hs_s[...] = (mult * (ig * xc2)).reshape(tt, nb, LRU_W)

    def step(t, h):
        h = a_s[t] * h + hs_s[t]
        hs_s[t] = h
        return h

    h_last = lax.fori_loop(0, tt, step, h_s[...], unroll=min(8, tt))
    h_s[...] = h_last
    y_ref[...] = hs_s[...] * _gelu(g)
    tail_s[...] = xe[tt:tt + CONV_W - 1]

    @pl.when(i == pl.num_programs(0) - 1)
    def _():
        tail_ref[...] = tail_s[...]
        hl_ref[...] = h_s[...]


def _lru(u_tm, cb_tm, h0, cw, cbias, wa_bd, ba, wx_bd, bx, lam, tt):
    t, nb, _ = u_tm.shape
    full = lambda a: pl.BlockSpec(a.shape, lambda i: (0,) * a.ndim)
    return pl.pallas_call(
        functools.partial(_lru_body, tt=tt, nb=nb), grid=(t // tt,),
        in_specs=[pl.BlockSpec((tt, nb, 512), lambda i: (i, 0, 0)), full(cb_tm), full(h0), full(cw),
                  full(cbias), full(wa_bd), full(ba), full(wx_bd), full(bx), full(lam)],
        out_specs=[pl.BlockSpec((tt, nb, 256), lambda i: (i, 0, 0)),
                   pl.BlockSpec((3, nb, 256), lambda i: (0, 0, 0)),
                   pl.BlockSpec((nb, 256), lambda i: (0, 0))],
        out_shape=[jax.ShapeDtypeStruct((t, nb, 256), F32), jax.ShapeDtypeStruct((3, nb, 256), F32),
                   jax.ShapeDtypeStruct((nb, 256), F32)],
        scratch_shapes=[pltpu.VMEM((3, nb, 256), F32), pltpu.VMEM((nb, 256), F32),
                        pltpu.VMEM((tt, nb, 256), F32), pltpu.VMEM((tt, nb, 256), F32)],
        compiler_params=_cparams(("arbitrary",)), name="rglru",
    )(u_tm, cb_tm, h0, cw, cbias, wa_bd, ba, wx_bd, bx, lam)


def _compress_rows_body(pt_ref, pool_ref, bdk_ref, bdv_ref, posk_ref, posv_ref, b1k_ref, b1v_ref,
                        w2k_ref, w2v_ref, kc_ref, vc_ref, buf, sem, *, P):
    s = pl.program_id(0)
    ns = pl.num_programs(0)
    nblk = P * (PAGE // CMP_BLK)

    def page_copy(step, slot, p, c):
        return pltpu.make_async_copy(
            pool_ref.at[pt_ref[step * P + p], :, pl.ds(LANES * c, LANES)],
            buf.at[slot, c, pl.ds(p * PAGE, PAGE), :], sem.at[slot])

    def start_all(step, slot):
        def body(p, carry):
            page_copy(step, slot, p, 0).start()
            page_copy(step, slot, p, 1).start()
            return carry
        lax.fori_loop(0, P, body, 0)

    def wait_all(step, slot):
        def body(p, carry):
            page_copy(step, slot, p, 0).wait()
            page_copy(step, slot, p, 1).wait()
            return carry
        lax.fori_loop(0, P, body, 0)

    @pl.when(s == 0)
    def _():
        start_all(0, 0)

    slot = lax.rem(s, 2)

    @pl.when(s + 1 < ns)
    def _():
        start_all(s + 1, 1 - slot)

    wait_all(s, slot)

    acc_k = jnp.zeros((nblk, 256), F32)
    acc_v = jnp.zeros((nblk, 256), F32)
    for r in range(CMP_BLK):
        rows = pl.ds(r, nblk, stride=CMP_BLK)
        xk = (buf[slot, 0, rows, :] + posk_ref[r:r + 1, :]).astype(BF16)
        xv = (buf[slot, 1, rows, :] + posv_ref[r:r + 1, :]).astype(BF16)
        acc_k = acc_k + _dot(xk, bdk_ref[r])
        acc_v = acc_v + _dot(xv, bdv_ref[r])
    kc_ref[...] = _dot(_gelu(acc_k + b1k_ref[...]).astype(BF16), w2k_ref[...])
    vc_ref[...] = _dot(_gelu(acc_v + b1v_ref[...]).astype(BF16), w2v_ref[...])


def _compress_rows(pt_flat, pool3, cw, P):
    n_pages = pt_flat.shape[0]
    nblk = P * (PAGE // CMP_BLK)
    full = lambda a: pl.BlockSpec(a.shape, lambda i, pt: (0,) * a.ndim)
    ws = (cw["bdk"], cw["bdv"], cw["posk"], cw["posv"], cw["b1k"], cw["b1v"], cw["w2k"], cw["w2v"])
    grid_spec = pltpu.PrefetchScalarGridSpec(
        num_scalar_prefetch=1, grid=(n_pages // P,),
        in_specs=[pl.BlockSpec(memory_space=pl.ANY)] + [full(a) for a in ws],
        out_specs=[pl.BlockSpec((nblk, 128), lambda i, pt: (i, 0)),
                   pl.BlockSpec((nblk, 128), lambda i, pt: (i, 0))],
        scratch_shapes=[pltpu.VMEM((2, 2, P * PAGE, LANES), F32), pltpu.SemaphoreType.DMA((2,))])
    return pl.pallas_call(
        functools.partial(_compress_rows_body, P=P), grid_spec=grid_spec,
        out_shape=[jax.ShapeDtypeStruct((n_pages * 4, 128), F32)] * 2,
        compiler_params=_cparams(("arbitrary",)), name="compress_rows",
    )(pt_flat, pool3, *ws)


def _compress_pages_body(pt_ref, pool_ref, w_ref, pos_ref, b1_ref, w2_ref, o_ref, buf, sem, *, P, nstep):
    c = pl.program_id(0)
    s = pl.program_id(1)
    lin = c * nstep + s
    PG = NSA_G * P

    def tile_copy(cc, step, slot, p, g):
        return pltpu.make_async_copy(
            pool_ref.at[(pt_ref[step * P + p] * 4 + cc) * NSA_G + g],
            buf.at[slot, :, NSA_G * p + g, :], sem.at[slot])

    def start_all(cc, step, slot):
        def body(p, carry):
            tile_copy(cc, step, slot, p, 0).start()
            tile_copy(cc, step, slot, p, 1).start()
            return carry
        lax.fori_loop(0, P, body, 0)

    def wait_all(cc, step, slot):
        def body(p, carry):
            tile_copy(cc, step, slot, p, 0).wait()
            tile_copy(cc, step, slot, p, 1).wait()
            return carry
        lax.fori_loop(0, P, body, 0)

    @pl.when(lin == 0)
    def _():
        start_all(0, 0, 0)

    slot = lax.rem(lin, 2)
    nxt = lin + 1

    @pl.when(nxt < 2 * nstep)
    def _():
        start_all(lax.div(nxt, nstep), lax.rem(nxt, nstep), 1 - slot)

    wait_all(c, s, slot)

    acc = jnp.zeros((PG, 4 * CMP_HID), F32)
    for dp in range(HD // 2):
        lhs = jnp.concatenate([buf[slot, 2 * dp], buf[slot, 2 * dp + 1]], axis=1) + pos_ref[dp:dp + 1, :]
        acc = acc + _dot(lhs.astype(BF16), w_ref[dp])
    h = _gelu(acc + b1_ref[...])
    o_ref[...] = _dot(h.astype(BF16), w2_ref[...])


def _compress_pages(pt_flat, pool_t, cw, P):
    n_pages = pt_flat.shape[0]
    nstep = n_pages // P
    PG = NSA_G * P
    grid_spec = pltpu.PrefetchScalarGridSpec(
        num_scalar_prefetch=1, grid=(2, nstep),
        in_specs=[pl.BlockSpec(memory_space=pl.ANY),
                  pl.BlockSpec((None, HD // 2, 256, 512), lambda c, s, pt: (c, 0, 0, 0)),
                  pl.BlockSpec((None, HD // 2, 256), lambda c, s, pt: (c, 0, 0)),
                  pl.BlockSpec((None, 1, 512), lambda c, s, pt: (c, 0, 0)),
                  pl.BlockSpec((None, 512, 256), lambda c, s, pt: (c, 0, 0))],
        out_specs=pl.BlockSpec((None, PG, 256), lambda c, s, pt: (c, s, 0)),
        scratch_shapes=[pltpu.VMEM((2, HD, PG, LANES), F32), pltpu.SemaphoreType.DMA((2,))])
    return pl.pallas_call(
        functools.partial(_compress_pages_body, P=P, nstep=nstep), grid_spec=grid_spec,
        out_shape=jax.ShapeDtypeStruct((2, n_pages * NSA_G, 256), F32),
        compiler_params=_cparams(("arbitrary", "arbitrary")), name="compress_pages",
    )(pt_flat, pool_t, cw["pw"], cw["ppos"], cw["pb1"], cw["pw2"])


def _half_mask(g):
    lane = lax.broadcasted_iota(I32, (1, LANES), 1)
    return (lane >= 64 * g) & (lane < 64 * g + 64)


def _gate_expand(gates, gexp_ref):
    hi, lo = _split2(gates)
    return [_dot(hi, gexp_ref[c]) + _dot(lo, gexp_ref[c]) for c in range(3)]


def _nsa_prompt_body(q_ref, sm_ref, kc_ref, vc_ref, kvs_ref, kaug_ref, win_ref, gexp_ref, o_ref, s_scr, *, T):
    QT = 128
    M = NSA_R * QT
    CK = KEY_CHUNK
    q0 = pl.program_id(1) * QT
    lane = lax.broadcasted_iota(I32, (1, LANES), 1)
    left = lane < 64
    row4 = lax.broadcasted_iota(I32, (M, 1), 0)
    qpos4 = q0 + (row4 & (QT - 1))
    qpos = q0 + lax.broadcasted_iota(I32, (QT, 1), 0)

    nce = T // CMP_BLK // 2
    blk = jnp.where(lane < nce, 2 * lane, 2 * (lane - nce) + 1)
    c_end = jnp.where(lane < 2 * nce, blk * CMP_BLK + (CMP_BLK - 1), 1 << 30)
    dist_c = qpos4 - c_end
    mask_c = dist_c >= 0
    dist_cf = dist_c.astype(F32)
    zpad = jnp.zeros((LANES - 2 * nce, LANES), F32)
    kc = jnp.concatenate([kc_ref[...], zpad], axis=0).astype(BF16)
    vc = jnp.concatenate([vc_ref[...], zpad], axis=0).astype(BF16)

    ns = T // SLC_BLK
    jrow = lax.broadcasted_iota(I32, (ns, 1), 0)
    cur = (q0 + lane) // SLC_BLK
    in_sel = (lane >= AUG_SEL) & (lane < AUG_SEL + ns)
    nk = q0 // CK + 1
    w0 = pl.multiple_of(jnp.maximum(q0 - WINDOW, 0), QT)
    WK = WINDOW + QT
    dist_w = qpos4 - (w0 + lax.broadcasted_iota(I32, (1, WK), 1))
    mask_w = (dist_w >= 0) & (dist_w < WINDOW)
    kwin = jnp.concatenate([win_ref[pl.ds(w0, WK), 0:128], kaug_ref[pl.ds(w0, WK), :]], axis=1)
    shift = -(q0 + QT - 1).astype(F32)

    def key_chunk(c):
        r0 = pl.multiple_of(c * CK, CK)
        return jnp.concatenate([kvs_ref[pl.ds(r0, CK), 0:128], kaug_ref[pl.ds(r0, CK), :]], axis=1)

    o_cmp, o_slc, o_win = [], [], []
    for g in range(NSA_G):
        hm = _half_mask(g)
        lsel = jnp.where(lane == (64 if g == 0 else 0), 1.0, 0.0)
        qs = [jnp.where(hm, q_ref[:, 128 * r:128 * r + 128], 0) for r in range(NSA_R)]
        q4 = jnp.concatenate(qs, axis=0)
        slope4 = jnp.zeros((M, 1), F32)
        for r in range(NSA_R):
            slope4 = jnp.where((row4 >= QT * r) & (row4 < QT * (r + 1)), SLOPES[g * NSA_R + r], slope4)

        s = _dot_nt(q4, kc) * (HD ** -0.5) - slope4 * dist_cf
        p = _masked_softmax(s, mask_c)
        o_cmp.append(_dot(p.astype(BF16), vc))
        imp = p[0:QT] + p[QT:2 * QT] + p[2 * QT:3 * QT] + p[3 * QT:4 * QT]

        imp_t = imp.T
        pooled = imp_t[0:nce] + imp_t[nce:2 * nce]
        valid = jrow <= cur
        forced = (jrow == 0) | (jrow == cur) | (jrow == cur - 1)
        score = jnp.where(valid, pooled + FORCE_BONUS * forced.astype(F32), NEG)
        cnt = jnp.zeros((ns, LANES), F32)
        for i2 in range(ns):
            si = score[i2:i2 + 1, :]
            beats = (si > score) | ((si == score) & (i2 < jrow))
            cnt = cnt + beats.astype(F32)
        sel_t = ((cnt < N_SEL) & (score > 0.5 * NEG)).astype(F32)
        selp = jnp.concatenate([jnp.zeros((AUG_SEL, LANES), F32), sel_t,
                                jnp.zeros((LANES - AUG_SEL - ns, LANES), F32)], axis=0).T
        sel_bias = jnp.where(in_sel, (selp - 1.0) * (-NEG), 0.0)

        def aug(r, with_sel):
            sl = SLOPES[g * NSA_R + r]
            base = sel_bias if with_sel else jnp.zeros((QT, LANES), F32)
            a = jnp.where(lane == AUG_HI, sl * SLC_BLK, jnp.where(lane == AUG_LO, sl,
                          jnp.where(lane == AUG_ONE, sl * shift, base)))
            return a.astype(BF16)

        qsc = [(qs[r].astype(F32) * (HD ** -0.5)).astype(BF16) for r in range(NSA_R)]
        q4s = jnp.concatenate([jnp.concatenate([qsc[r], aug(r, True)], axis=1) for r in range(NSA_R)], axis=0)
        q4w = jnp.concatenate([jnp.concatenate([qsc[r], aug(r, False)], axis=1) for r in range(NSA_R)], axis=0)

        def fold(sc):
            return jnp.maximum(sc[:, 0:128], sc[:, 128:256])

        def body1(c, mx):
            sc = _dot_nt(q4s, key_chunk(c))
            s_scr[c] = sc
            return jnp.maximum(mx, fold(sc))

        mx = lax.fori_loop(0, nk - 1, body1, jnp.full((M, LANES), NEG, F32))
        cl = nk - 1
        kpos_l = cl * CK + lax.broadcasted_iota(I32, (1, CK), 1)
        sc = jnp.where(kpos_l <= qpos4, _dot_nt(q4s, key_chunk(cl)), NEG)
        s_scr[cl] = sc
        m = jnp.max(jnp.maximum(mx, fold(sc)), axis=1, keepdims=True)

        def body2(c, acc):
            e = jnp.exp(s_scr[c] - m).astype(BF16)
            r0 = pl.multiple_of(c * CK, CK)
            vg = jnp.where(hm, kvs_ref[pl.ds(r0, CK), 128:256], 1)
            return acc + _dot(e, vg)

        acc = lax.fori_loop(0, nk, body2, jnp.zeros((M, LANES), F32))
        o_slc.append(acc / jnp.sum(acc * lsel, axis=1, keepdims=True))

        sw = jnp.where(mask_w, _dot_nt(q4w, kwin), NEG)
        e = jnp.exp(sw - jnp.max(sw, axis=1, keepdims=True)).astype(BF16)
        accw = _dot(e, jnp.where(hm, win_ref[pl.ds(w0, WK), 128:256], 1))
        o_win.append(accw / jnp.sum(accw * lsel, axis=1, keepdims=True))

    gates = _sigmoid(sm_ref[...])
    gc, gs, gw = _gate_expand(gates, gexp_ref)
    for r in range(NSA_R):
        sl = slice(128 * r, 128 * r + 128)
        rows = slice(QT * r, QT * (r + 1))
        oc = jnp.where(left, o_cmp[0][rows], o_cmp[1][rows])
        os_ = jnp.where(left, o_slc[0][rows], o_slc[1][rows])
        ow = jnp.where(left, o_win[0][rows], o_win[1][rows])
        o_ref[:, sl] = gc[:, sl] * oc + gs[:, sl] * os_ + gw[:, sl] * ow


def _nsa_prompt(q, sm, kc_eo, vc_eo, kvs_bf, win_bf, kaug, gexp):
    b, t, _ = q.shape
    return pl.pallas_call(
        functools.partial(_nsa_prompt_body, T=t), grid=(b, t // 128),
        in_specs=[pl.BlockSpec((None, 128, 512), lambda i, j: (i, j, 0)),
                  pl.BlockSpec((None, 128, 128), lambda i, j: (i, j, 0)),
                  pl.BlockSpec((None, t // CMP_BLK, 128), lambda i, j: (i, 0, 0)),
                  pl.BlockSpec((None, t // CMP_BLK, 128), lambda i, j: (i, 0, 0)),
                  pl.BlockSpec((None, t, 256), lambda i, j: (i, 0, 0)),
                  pl.BlockSpec(kaug.shape, lambda i, j: (0, 0)),
                  pl.BlockSpec((None, t, 256), lambda i, j: (i, 0, 0)),
                  pl.BlockSpec(gexp.shape, lambda i, j: (0, 0, 0))],
        out_specs=pl.BlockSpec((None, 128, 512), lambda i, j: (i, j, 0)),
        out_shape=jax.ShapeDtypeStruct((b, t, 512), F32),
        scratch_shapes=[pltpu.VMEM((t // KEY_CHUNK, NSA_R * 128, KEY_CHUNK), F32)],
        compiler_params=_cparams(("parallel", "arbitrary")), name="nsa_prompt",
    )(q, sm, kc_eo, vc_eo, kvs_bf, kaug, win_bf, gexp)


def _nsa_sample_sel_body(q_ref, kc_ref, vc_ref, wbuf_ref, wnew_ref, oc_ref, ow_ref, idx_ref, *, past, nce):
    R8 = 8
    qpos = past + lax.broadcasted_iota(I32, (R8, 1), 0)
    lane_c = lax.broadcasted_iota(I32, (1, 2 * nce), 1)
    blk = jnp.where(lane_c < nce, 2 * lane_c, 2 * (lane_c - nce) + 1)
    dist_c = qpos - (blk * CMP_BLK + (CMP_BLK - 1))
    mask_c = dist_c >= 0
    dist_cf = dist_c.astype(F32)
    kc = kc_ref[...].astype(BF16)
    vc = vc_ref[...].astype(BF16)

    wb = wbuf_ref.shape[3]
    wk = wbuf_ref[0].reshape(NSA_G * HD, wb).astype(BF16)
    wv = wbuf_ref[1].reshape(NSA_G * HD, wb).astype(BF16)
    nk = wnew_ref[:, 0:128].astype(BF16)
    nv = wnew_ref[:, 128:256].astype(BF16)
    dist_w1 = qpos - (past - wb + lax.broadcasted_iota(I32, (1, wb), 1))
    mask_w1 = (dist_w1 >= 0) & (dist_w1 < WINDOW)
    dist_w2 = qpos - (past + lax.broadcasted_iota(I32, (1, R8), 1))
    mask_w2 = (dist_w2 >= 0) & (dist_w2 < WINDOW)

    ns_l = 3 * LANES
    n_sel_blk = nce + 1
    lane_s = lax.broadcasted_iota(I32, (1, ns_l), 1)
    cur = qpos // SLC_BLK
    lane16 = lax.broadcasted_iota(I32, (1, LANES), 1)
    left = lane16 < 64

    oc = [[None] * NSA_R for _ in range(NSA_G)]
    ow = [[None] * NSA_R for _ in range(NSA_G)]
    for g in range(NSA_G):
        hm = _half_mask(g)
        imp = jnp.zeros((R8, 2 * nce), F32)
        for r in range(NSA_R):
            slope = SLOPES[g * NSA_R + r]
            qh = jnp.where(hm, q_ref[:, 128 * r:128 * r + 128], 0).astype(BF16)
            s = _dot_nt(qh, kc) * (HD ** -0.5) - slope * dist_cf
            p = _masked_softmax(s, mask_c)
            imp = imp + p
            oc[g][r] = _dot(p.astype(BF16), vc)
            s1 = jnp.where(mask_w1, _dot(qh, wk) * (HD ** -0.5) - slope * dist_w1.astype(F32), NEG)
            s2 = jnp.where(mask_w2, _dot_nt(qh, nk) * (HD ** -0.5) - slope * dist_w2.astype(F32), NEG)
            mx = jnp.maximum(jnp.max(s1, axis=1, keepdims=True), jnp.max(s2, axis=1, keepdims=True))
            e1 = jnp.exp(s1 - mx)
            e2 = jnp.exp(s2 - mx)
            den = jnp.sum(e1, axis=1, keepdims=True) + jnp.sum(e2, axis=1, keepdims=True)
            p1 = jnp.where(mask_w1, e1 / den, 0.0)
            p2 = jnp.where(mask_w2, e2 / den, 0.0)
            ow[g][r] = _dot_nt(p1.astype(BF16), wv) + _dot(p2.astype(BF16), nv)
        pooled = imp[:, 0:nce] + imp[:, nce:2 * nce]
        pooled = jnp.concatenate([pooled, jnp.zeros((R8, ns_l - nce), F32)], axis=1)
        valid = lane_s <= cur
        forced = (lane_s == 0) | (lane_s == cur) | (lane_s == cur - 1)
        score = jnp.where(valid, pooled + FORCE_BONUS * forced.astype(F32), NEG)
        score = jnp.where(lane_s < n_sel_blk, score, -jnp.inf)
        idx_acc = jnp.zeros((R8, LANES), F32)
        lane_sf = lane_s.astype(F32)
        for n in range(N_SEL):
            mx = jnp.max(score, axis=1, keepdims=True)
            pick = jnp.min(jnp.where(score == mx, lane_sf, float(ns_l)), axis=1, keepdims=True)
            pick_v = jnp.where(mx > 0.5 * NEG, pick, -1.0)
            idx_acc = jnp.where(lane16 == n, pick_v, idx_acc)
            score = jnp.where(lane_sf == pick, -jnp.inf, score)
        idx_ref[g] = idx_acc.astype(I32)
    for r in range(NSA_R):
        sl = slice(128 * r, 128 * r + 128)
        oc_ref[:, sl] = jnp.where(left, oc[0][r], oc[1][r])
        ow_ref[:, sl] = jnp.where(left, ow[0][r], ow[1][r])


def _nsa_sample_sel(q8, kc_eo, vc_eo, win_t, layer, wnew8, past):
    b = q8.shape[0]
    nce2 = kc_eo.shape[1]
    wb = win_t.shape[-1]
    return pl.pallas_call(
        functools.partial(_nsa_sample_sel_body, past=past, nce=nce2 // 2), grid=(b,),
        in_specs=[pl.BlockSpec((None, 8, 512), lambda i: (i, 0, 0)),
                  pl.BlockSpec((None, nce2, 128), lambda i: (i, 0, 0)),
                  pl.BlockSpec((None, nce2, 128), lambda i: (i, 0, 0)),
                  pl.BlockSpec((None, None, 2, NSA_G, HD, wb), lambda i: (layer, i, 0, 0, 0, 0)),
                  pl.BlockSpec((None, 8, 256), lambda i: (i, 0, 0))],
        out_specs=[pl.BlockSpec((None, 8, 512), lambda i: (i, 0, 0)),
                   pl.BlockSpec((None, 8, 512), lambda i: (i, 0, 0)),
                   pl.BlockSpec((None, 2, 8, 128), lambda i: (i, 0, 0, 0))],
        out_shape=[jax.ShapeDtypeStruct((b, 8, 512), F32), jax.ShapeDtypeStruct((b, 8, 512), F32),
                   jax.ShapeDtypeStruct((b, 2, 8, 128), I32)],
        compiler_params=_cparams(("parallel",)), name="nsa_sample_sel",
    )(q8, kc_eo, vc_eo, win_t, wnew8)


def _nsa_sample_slc_body(idx_ref, pt_ref, pool_ref, q_ref, knew_ref, sm_ref, oc_ref, ow_ref, gexp_ref,
                         o_ref, kbuf, vbuf, sem, *, T, past, n_pages):
    b = pl.program_id(0)
    ns_past = past // SLC_BLK
    per_page = PAGE // SLC_BLK

    def blk_copy(g, t, n, which):
        j = idx_ref[((b * NSA_G + g) * 8 + t) * N_SEL + n]
        jc = jnp.clip(j, 0, ns_past - 1)
        page = pt_ref[b * n_pages + jc // per_page]
        slot = (g * T + t) * N_SEL + n
        dst = kbuf if which == 0 else vbuf
        return pltpu.make_async_copy(pool_ref.at[(page * 4 + 2 + which) * NSA_G + g], dst.at[slot],
                                     sem.at[which])

    for g in range(NSA_G):
        for t in range(T):
            for n in range(N_SEL):
                blk_copy(g, t, n, 0).start()
                blk_copy(g, t, n, 1).start()
    for g in range(NSA_G):
        for t in range(T):
            for n in range(N_SEL):
                blk_copy(g, t, n, 0).wait()
                blk_copy(g, t, n, 1).wait()

    R8 = 8
    NK = N_SEL * PAGE
    lane_k = lax.broadcasted_iota(I32, (1, NK), 1)
    slot_k = lane_k // PAGE
    within = lane_k - slot_k * PAGE
    half_k = within // SLC_BLK
    off_k = within - half_k * SLC_BLK
    rowi = lax.broadcasted_iota(I32, (R8, 1), 0)
    knew = knew_ref[:, 0:128].astype(BF16)
    vnew = knew_ref[:, 128:256].astype(BF16)
    kpos_new = past + lax.broadcasted_iota(I32, (1, R8), 1)
    zhalf = jnp.zeros((R8, HD), F32)
    o_acc = [jnp.zeros((R8, LANES), F32) for _ in range(NSA_R)]
    for g in range(NSA_G):
        hm = _half_mask(g)
        slope = jnp.zeros((R8, 1), F32)
        for r in range(NSA_R):
            slope = jnp.where(rowi == r, SLOPES[g * NSA_R + r], slope)
        for t in range(T):
            qpos = past + t
            qm = jnp.zeros((R8, LANES), F32)
            for r in range(NSA_R):
                qm = jnp.where(rowi == r, q_ref[t:t + 1, 128 * r:128 * r + 128], qm)
            qm = jnp.where(hm, qm, 0.0)
            qc = (qm[:, 0:HD] + qm[:, HD:2 * HD]).astype(BF16)
            base = (g * T + t) * N_SEL
            kt = jnp.concatenate([kbuf[base + n] for n in range(N_SEL)], axis=1).astype(BF16)
            vt = jnp.concatenate([vbuf[base + n] for n in range(N_SEL)], axis=1).astype(BF16)
            kpos = off_k
            ok_i = jnp.zeros((1, NK), I32)
            has_new = jnp.int32(0)
            for n in range(N_SEL):
                j = idx_ref[((b * NSA_G + g) * 8 + t) * N_SEL + n]
                in_blk = (slot_k == n) & (half_k == lax.rem(j, per_page))
                kpos = jnp.where(in_blk, j * SLC_BLK + off_k, kpos)
                ok_i = jnp.where(in_blk, ((j >= 0) & (j < ns_past)).astype(I32), ok_i)
                has_new = has_new + (j == ns_past).astype(I32)
            dist1 = qpos - kpos
            mask1 = (ok_i > 0) & (dist1 >= 0)
            dist2 = qpos - kpos_new
            mask2 = (dist2 >= 0) & ((jnp.zeros((1, R8), I32) + has_new) > 0)
            s1 = jnp.where(mask1, _dot(qc, kt) * (HD ** -0.5) - slope * dist1.astype(F32), NEG)
            s2 = jnp.where(mask2, _dot_nt(qm.astype(BF16), knew) * (HD ** -0.5) - slope * dist2.astype(F32), NEG)
            mx = jnp.maximum(jnp.max(s1, axis=1, keepdims=True), jnp.max(s2, axis=1, keepdims=True))
            e1 = jnp.exp(s1 - mx)
            e2 = jnp.exp(s2 - mx)
            den = jnp.sum(e1, axis=1, keepdims=True) + jnp.sum(e2, axis=1, keepdims=True)
            p1 = jnp.where(mask1, e1 / den, 0.0)
            p2 = jnp.where(mask2, e2 / den, 0.0)
            o1 = _dot_nt(p1.astype(BF16), vt)
            o1 = jnp.concatenate([o1, zhalf] if g == 0 else [zhalf, o1], axis=1)
            o = o1 + jnp.where(hm, _dot(p2.astype(BF16), vnew), 0.0)
            for r in range(NSA_R):
                o_acc[r] = jnp.where((rowi == t) & hm, o[r:r + 1, :], o_acc[r])
    gates = _sigmoid(sm_ref[...])
    gc, gs, gw = _gate_expand(gates, gexp_ref)
    for r in range(NSA_R):
        sl = slice(128 * r, 128 * r + 128)
        o_ref[:, sl] = gc[:, sl] * oc_ref[:, sl] + gs[:, sl] * o_acc[r] + gw[:, sl] * ow_ref[:, sl]


def _nsa_sample_slc(idx_flat, pt_flat, pool_t, q8, kvnew8, sm8, oc, ow, gexp, T, past):
    b = q8.shape[0]
    n_pages = pt_flat.shape[0] // b
    row = lambda w: pl.BlockSpec((None, 8, w), lambda i, a, c: (i, 0, 0))
    nslot = NSA_G * T * N_SEL
    grid_spec = pltpu.PrefetchScalarGridSpec(
        num_scalar_prefetch=2, grid=(b,),
        in_specs=[pl.BlockSpec(memory_space=pl.ANY), row(512), row(256), row(128), row(512), row(512),
                  pl.BlockSpec(gexp.shape, lambda i, a, c: (0, 0, 0))],
        out_specs=row(512),
        scratch_shapes=[pltpu.VMEM((nslot, HD, PAGE), F32), pltpu.VMEM((nslot, HD, PAGE), F32),
                        pltpu.SemaphoreType.DMA((2,))])
    return pl.pallas_call(
        functools.partial(_nsa_sample_slc_body, T=T, past=past, n_pages=n_pages), grid_spec=grid_spec,
        out_shape=jax.ShapeDtypeStruct((b, 8, 512), F32),
        compiler_params=_cparams(("arbitrary",)), name="nsa_sample_slc",
    )(idx_flat, pt_flat, pool_t, q8, kvnew8, sm8, oc, ow, gexp)


def _mix_out_body(ya_ref, yb_ref, yc_ref, x_ref, g_ref, w_ref, pool_ref, poolt_ref, o_ref):
    y = jnp.concatenate([ya_ref[...], yb_ref[...], yc_ref[...]], axis=-1)
    ms = _dot_split(y * y, pool_ref[...])
    rb = _dot_split(lax.rsqrt(ms + EPS), poolt_ref[...])
    yn = (y * rb * g_ref[...]).astype(BF16)
    o_ref[...] = x_ref[...] + _dot(yn, w_ref[...])


def _mix_out(ya, yb, yc, x2, g, w_bf, pool, poolt, tm):
    n = x2.shape[0]
    row = lambda w: pl.BlockSpec((tm, w), lambda i: (i, 0))
    full = lambda a: pl.BlockSpec(a.shape, lambda i: (0,) * a.ndim)
    return pl.pallas_call(
        _mix_out_body, grid=(n // tm,),
        in_specs=[row(256), row(256), row(512), row(1024), full(g), full(w_bf), full(pool), full(poolt)],
        out_specs=row(1024), out_shape=jax.ShapeDtypeStruct((n, 1024), F32),
        compiler_params=_cparams(("parallel",)), name="mix_out",
    )(ya, yb, yc, x2, g, w_bf, pool, poolt)


def _xattn_body(q_ref, kv_ref, o_ref):
    for h in range(XA_H):
        k = kv_ref[:, 128 * h:128 * h + 128].astype(BF16)
        v = kv_ref[:, XA_INNER + 128 * h:XA_INNER + 128 * h + 128].astype(BF16)
        s = _dot_nt(q_ref[:, 128 * h:128 * h + 128], k) * (XA_HD ** -0.5)
        e = jnp.exp(s - jnp.max(s, axis=-1, keepdims=True))
        a = e / jnp.sum(e, axis=-1, keepdims=True)
        o_ref[:, 128 * h:128 * h + 128] = _dot(a.astype(BF16), v).astype(BF16)


def _xattn(q, mkv, tq):
    b, t, _ = q.shape
    return pl.pallas_call(
        _xattn_body, grid=(b, t // tq),
        in_specs=[pl.BlockSpec((None, tq, 512), lambda i, j: (i, j, 0)),
                  pl.BlockSpec((None, N_MEM, 1024), lambda i, j: (i, 0, 0))],
        out_specs=pl.BlockSpec((None, tq, 512), lambda i, j: (i, j, 0)),
        out_shape=jax.ShapeDtypeStruct((b, t, 512), BF16),
        compiler_params=_cparams(("parallel", "arbitrary")), name="xattn",
    )(q, mkv)


def _moe_body(x_ref, g_ref, rwh_ref, rwl_ref, rb_ref, wg_ref, wu_ref, wd_ref, o_ref, zn_s, comb_s, acc_s):
    e = pl.program_id(1)
    lane = lax.broadcasted_iota(I32, (1, LANES), 1)

    @pl.when(e == 0)
    def _():
        z = _rms(x_ref[...], g_ref[...])
        zh, zl = _split2(z)
        zn_s[...] = zh
        logits = _dot(zh, rwh_ref[...]) + _dot(zh, rwl_ref[...]) + _dot(zl, rwh_ref[...]) + rb_ref[...]
        is_g = (lane >= N_EXP) & (lane < N_EXP + N_GROUPS)
        gl = jnp.where(is_g, logits, -jnp.inf)
        gmax = jnp.max(gl, axis=-1, keepdims=True)
        gw = 1.0 / jnp.sum(jnp.exp(gl - gmax), axis=-1, keepdims=True)
        lanef = lane.astype(F32)
        grpf = (lane // EXP_PER_GROUP).astype(F32)
        gsel = jnp.min(jnp.where(gl == gmax, lanef, 1e6), axis=-1, keepdims=True) - N_EXP
        in_grp = (lane < N_EXP) & (grpf == gsel)
        le = jnp.where(in_grp, logits, -jnp.inf)
        v1 = jnp.max(le, axis=-1, keepdims=True)
        i1 = jnp.min(jnp.where(le == v1, lanef, 1e6), axis=-1, keepdims=True)
        le2 = jnp.where(lanef == i1, -jnp.inf, le)
        v2 = jnp.max(le2, axis=-1, keepdims=True)
        i2 = jnp.min(jnp.where(le2 == v2, lanef, 1e6), axis=-1, keepdims=True)
        e2 = jnp.exp(v2 - v1)
        w1 = 1.0 / (1.0 + e2)
        w2 = e2 / (1.0 + e2)
        comb_s[...] = gw * (jnp.where(lanef == i1, w1, 0.0) + jnp.where(lanef == i2, w2, 0.0))
        acc_s[...] = jnp.zeros_like(acc_s)

    zn = zn_s[...]
    hg = _dot(zn, wg_ref[...])
    hu = _dot(zn, wu_ref[...])
    ce = jnp.sum(jnp.where(lane == e, comb_s[...], 0.0), axis=-1, keepdims=True)
    h = (hg * _sigmoid(hg)) * hu * ce
    acc_s[...] += _dot(h.astype(BF16), wd_ref[...])

    @pl.when(e == pl.num_programs(1) - 1)
    def _():
        o_ref[...] = x_ref[...] + acc_s[...]


def _moe(x2, g, rwh, rwl, rb, wg, wu, wd, layer, tm):
    n = x2.shape[0]
    full = lambda a: pl.BlockSpec(a.shape, lambda i, e: (0,) * a.ndim)
    return pl.pallas_call(
        _moe_body, grid=(n // tm, N_EXP),
        in_specs=[pl.BlockSpec((tm, 1024), lambda i, e: (i, 0)), full(g), full(rwh), full(rwl), full(rb),
                  pl.BlockSpec((None, None, 1024, EXP_FF), lambda i, e: (layer, e, 0, 0)),
                  pl.BlockSpec((None, None, 1024, EXP_FF), lambda i, e: (layer, e, 0, 0)),
                  pl.BlockSpec((None, None, EXP_FF, 1024), lambda i, e: (layer, e, 0, 0))],
        out_specs=pl.BlockSpec((tm, 1024), lambda i, e: (i, 0)),
        out_shape=jax.ShapeDtypeStruct((n, 1024), F32),
        scratch_shapes=[pltpu.VMEM((tm, 1024), BF16), pltpu.VMEM((tm, 128), F32), pltpu.VMEM((tm, 1024), F32)],
        compiler_params=_cparams(("parallel", "arbitrary")), name="moe",
    )(x2, g, rwh, rwl, rb, wg, wu, wd)


def _q_perm():
    idx = []
    for r in range(NSA_R):
        for g in range(NSA_G):
            h = g * NSA_R + r
            idx.extend(range(64 * h, 64 * h + 64))
    return np.asarray(idx)


def _prep_layer(w, l):
    p = {"layer": l}
    win = w["w_in"][l]
    qp = _q_perm()
    o_q = 1032 + 512
    cols = [win[:, 0:1024], win[:, 1032:1032 + 512], win[:, o_q:o_q + 512][:, qp],
            win[:, o_q + 512:o_q + 512 + 768], win[:, 1024:1032], win[:, o_q + 1280:o_q + 1304],
            jnp.zeros((D_MODEL, 128 - 32), F32)]
    p["w_in"] = jnp.concatenate(cols, axis=1).astype(BF16)
    p["w_st"] = win[:, 1024:1032].T.astype(BF16)
    p["norm_mix"] = w["norm_mix"][l][None]
    bias8 = jnp.concatenate([w["ml_i_bias"][l], w["ml_f_bias"][l]])
    p["ml_bcol"] = jnp.zeros((1, 128), F32).at[0, 0:8].set(bias8)
    p["ml_brow"] = bias8[:, None]
    p["conv_w"] = w["conv_w"][l]
    p["conv_b"] = w["conv_b"][l][None]
    bd = lambda m: jax.scipy.linalg.block_diag(*[m[i] for i in range(m.shape[0])])
    p["lru_wa"] = bd(w["lru_wa"][l]).astype(BF16)
    p["lru_wx"] = bd(w["lru_wx"][l]).astype(BF16)
    p["lru_ba"] = w["lru_ba"][l][None]
    p["lru_bx"] = w["lru_bx"][l][None]
    p["lru_lambda"] = w["lru_lambda"][l][None]
    cw = {}
    eye4 = jnp.eye(PAGE // CMP_BLK, dtype=F32)
    pw, ppos, pb1, pw2 = [], [], [], []
    for c, nm in ((0, "k"), (1, "v")):
        w1 = w["phi_w1"][l, c].reshape(CMP_BLK, HD, CMP_HID)
        z = jnp.zeros_like(w1)
        cw["bd" + nm] = jnp.concatenate([jnp.concatenate([w1, z], axis=2),
                                         jnp.concatenate([z, w1], axis=2)], axis=1).astype(BF16)
        cw["pos" + nm] = jnp.tile(w["cmp_pos"][l, c], (1, 2))
        cw["b1" + nm] = jnp.tile(w["phi_b1"][l, c], 2)[None]
        w2 = w["phi_w2"][l, c]
        z2 = jnp.zeros_like(w2)
        cw["w2" + nm] = jnp.concatenate([jnp.concatenate([w2, z2], axis=1),
                                         jnp.concatenate([z2, w2], axis=1)], axis=0).astype(BF16)
        w3 = w1.transpose(1, 0, 2).reshape(HD // 2, 2, CMP_BLK, CMP_HID)
        pw.append(jnp.einsum("pdrh,bc->pdbrch", w3, eye4).reshape(HD // 2, 256, 512).astype(BF16))
        pos3 = w["cmp_pos"][l, c].T.reshape(HD // 2, 2, 1, CMP_BLK)
        ppos.append(jnp.broadcast_to(pos3, (HD // 2, 2, 4, CMP_BLK)).reshape(HD // 2, 256))
        pb1.append(jnp.tile(w["phi_b1"][l, c], 4)[None])
        pw2.append(jnp.kron(eye4, w2).astype(BF16))
    cw["pw"] = jnp.stack(pw)
    cw["ppos"] = jnp.stack(ppos)
    cw["pb1"] = jnp.stack(pb1)
    cw["pw2"] = jnp.stack(pw2)
    p["cmp"] = cw
    perm = np.concatenate([np.arange(512), 512 + qp])
    p["mix_norm"] = w["mix_norm"][l][perm][None]
    p["w_out"] = w["w_out"][l][perm, :].astype(BF16)
    p["norm_xa"] = w["norm_xa"][l][None]
    p["norm_mem"] = w["norm_mem"][l][None]
    p["xa_wq"] = w["xa_wq"][l].astype(BF16)
    p["xa_wkv"] = w["xa_wkv"][l].astype(BF16)
    p["xa_wo"] = w["xa_wo"][l].astype(BF16)
    p["norm_ffn"] = w["norm_ffn"][l][None]
    rw = jnp.concatenate([w["router_ew"][l], w["router_gw"][l], jnp.zeros((D_MODEL, 128 - 20), F32)], axis=1)
    rwh = rw.astype(BF16)
    p["rwh"] = rwh
    p["rwl"] = (rw - rwh.astype(F32)).astype(BF16)
    p["rb"] = jnp.concatenate([w["router_eb"][l], w["router_gb"][l], jnp.zeros((128 - 20,), F32)])[None]
    return p


def _constants(T):
    c = {}
    head = np.arange(1024) // HD
    pool = np.zeros((1024, 128), np.float32)
    pool[np.arange(1024), head] = 1.0 / HD
    poolt = np.zeros((128, 1024), np.float32)
    poolt[head, np.arange(1024)] = 1.0
    c["pool"] = jnp.asarray(pool, BF16)
    c["poolt"] = jnp.asarray(poolt, BF16)
    gexp = np.zeros((3, 128, 512), np.float32)
    for cc in range(3):
        for g in range(NSA_G):
            for r in range(NSA_R):
                gexp[cc, 8 + cc * 8 + g * 4 + r, 128 * r + 64 * g:128 * r + 64 * g + 64] = 1.0
    c["gexp"] = jnp.asarray(gexp, BF16)
    kpos = np.arange(T)
    kaug = np.zeros((T, 128), np.float32)
    kaug[:, AUG_HI] = kpos // SLC_BLK
    kaug[:, AUG_LO] = kpos % SLC_BLK
    kaug[:, AUG_ONE] = 1.0
    kaug[kpos, AUG_SEL + kpos // SLC_BLK] = 1.0
    c["kaug"] = jnp.asarray(kaug, BF16)
    return c


def _even_odd(a):
    return jnp.concatenate([a[:, 0::2], a[:, 1::2]], axis=1)


def _pad_rows(a, rows):
    return jnp.pad(a, ((0, 0), (0, rows - a.shape[1]), (0, 0)))


def _dense_tail(x2, p, ew, B, T, mkv, tm, tq, tm_moe):
    n = x2.shape[0]
    q = _norm_mm(x2, p["norm_xa"], p["xa_wq"], tm, BF16).reshape(B, T, XA_INNER)
    if T < tq:
        o = _xattn(_pad_rows(q, tq), mkv, tq)[:, 0:T]
    else:
        o = _xattn(q, mkv, tq)
    x2 = _mm_res(o.reshape(n, XA_INNER), p["xa_wo"], x2, tm)
    return _moe(x2, p["norm_ffn"], p["rwh"], p["rwl"], p["rb"], ew[0], ew[1], ew[2], p["layer"], tm_moe)


def _layer_prompt(x2, p, c, ew, B, T, mem2):
    n = B * T
    ml, lru, q, kv, win, sm, smt, kvs_bf, win_bf = _in_proj(x2, p["norm_mix"], p["w_in"], p["w_st"], 256)
    smt_b = smt.reshape(8, B, T).transpose(1, 0, 2)
    ya, C, nn, mm = _mlstm(ml.reshape(B, T, 1024), sm.reshape(B, T, 128), smt_b, p["ml_bcol"], p["ml_brow"],
                           jnp.zeros((B, 4, 64, 64), F32), jnp.zeros((B, 4, 1, 64), F32),
                           jnp.zeros((B, 1, 128), F32), 128, 128)
    u_tm = lru.reshape(B, T, 512).transpose(1, 0, 2)
    yb_tm, tail_tm, h_last = _lru(u_tm, jnp.zeros((3, B, 256), F32), jnp.zeros((B, 256), F32), p["conv_w"],
                                  p["conv_b"], p["lru_wa"], p["lru_ba"], p["lru_wx"], p["lru_bx"],
                                  p["lru_lambda"], 256)
    yb = yb_tm.transpose(1, 0, 2).reshape(n, 256)
    n_pages = n // PAGE
    kc, vc = _compress_rows(jnp.arange(n_pages, dtype=I32), kv.reshape(n_pages, PAGE, 512), p["cmp"], 16)
    nc = T // CMP_BLK
    kc_eo = _even_odd(kc.reshape(B, nc, 128))
    vc_eo = _even_odd(vc.reshape(B, nc, 128))
    yc = _nsa_prompt(q.reshape(B, T, 512), sm.reshape(B, T, 128), kc_eo, vc_eo, kvs_bf.reshape(B, T, 256),
                     win_bf.reshape(B, T, 256), c["kaug"], c["gexp"])
    x2 = _mix_out(ya.reshape(n, 256), yb, yc.reshape(n, 512), x2, p["mix_norm"], p["w_out"], c["pool"],
                  c["poolt"], 256)
    mkv = _norm_mm(mem2, p["norm_mem"], p["xa_wkv"], 256, F32)
    x2 = _dense_tail(x2, p, ew, B, T, mkv.reshape(B, N_MEM, 1024), 256, 512, 1024)
    st = (kv.reshape(B, T, 4, NSA_G, HD), win.reshape(B, T, 2, NSA_G, HD)[:, T - WINDOW:],
          C, nn.reshape(B, 4, 64), mm[:, 0, 0:4], h_last, tail_tm.transpose(1, 0, 2))
    return x2, st, mkv.reshape(B, N_MEM, 2, XA_H, XA_HD)


def _layer_sample(x2, p, c, ew, B, T, pool_t, pt_flat, win_buf, win_t, C0, n0, m0, conv0, h0, mkv, past):
    n = B * T
    _pad_rows8 = lambda a: _pad_rows(a, 8)
    ml, lru, q, kv, win, sm, smt, _, _ = _in_proj(x2, p["norm_mix"], p["w_in"], p["w_st"], n)
    sm8 = _pad_rows8(sm.reshape(B, T, 128))
    smt_b = jnp.pad(smt.reshape(8, B, T).transpose(1, 0, 2), ((0, 0), (0, 0), (0, 8 - T)))
    m0p = jnp.pad(m0[:, None, :], ((0, 0), (0, 0), (0, 128 - ML_H)))
    ya, C, nn, mm = _mlstm(_pad_rows8(ml.reshape(B, T, 1024)), sm8, smt_b, p["ml_bcol"], p["ml_brow"],
                           C0, n0[:, :, None, :], m0p, 8, T)
    ya = ya[:, 0:T]
    u_tm = lru.reshape(B, T, 512).transpose(1, 0, 2)
    yb_tm, tail_tm, h_last = _lru(u_tm, conv0.transpose(1, 0, 2), h0, p["conv_w"], p["conv_b"], p["lru_wa"],
                                  p["lru_ba"], p["lru_wx"], p["lru_bx"], p["lru_lambda"], T)
    yb = yb_tm.transpose(1, 0, 2).reshape(n, 256)
    n_pages = past // PAGE
    kvc = _compress_pages(pt_flat, pool_t, p["cmp"], 128)
    nc = past // CMP_BLK
    kvc = kvc.reshape(2, B, n_pages, NSA_G, PAGE // CMP_BLK, HD).transpose(0, 1, 2, 4, 3, 5)
    kvc = kvc.reshape(2, B, nc, NSA_G * HD)
    kc_eo = _even_odd(kvc[0])
    vc_eo = _even_odd(kvc[1])
    wb = win_buf.shape[1]
    q8 = _pad_rows8(q.reshape(B, T, 512).astype(F32))
    win3 = win.reshape(B, T, 256)
    oc, ow, idx = _nsa_sample_sel(q8, kc_eo, vc_eo, win_t, p["layer"], _pad_rows8(win3), past)
    idx_flat = idx[:, :, :, 0:N_SEL].reshape(-1)
    kv3 = kv.reshape(B, T, 512)
    yc8 = _nsa_sample_slc(idx_flat, pt_flat, pool_t, q8, _pad_rows8(kv3[:, :, 256:512]), sm8, oc, ow,
                          c["gexp"], T, past)
    yc = yc8[:, 0:T].reshape(n, 512)
    x2 = _mix_out(ya.reshape(n, 256), yb, yc, x2, p["mix_norm"], p["w_out"], c["pool"], c["poolt"], n)
    x2 = _dense_tail(x2, p, ew, B, T, mkv.reshape(B, N_MEM, 1024), n, 16, n)
    win_new = jnp.concatenate([win_buf.reshape(B, wb, 256), win3], axis=1)[:, T:]
    st = (kv.reshape(B, T, 4, NSA_G, HD), win_new.reshape(B, wb, 2, NSA_G, HD), C, nn.reshape(B, 4, 64),
          mm[:, 0, 0:4], h_last, tail_tm.transpose(1, 0, 2))
    return x2, st


def kernel(x_prompt, x_sample, cache_nsa_kv, state_nsa_win, state_mlstm_C, state_mlstm_n, state_mlstm_m, state_rglru_h, state_rglru_conv, cache_mem_kv, page_table, mem_prompt, norm_mix, w_in, ml_i_bias, ml_f_bias, conv_w, conv_b, lru_wa, lru_ba, lru_wx, lru_bx, lru_lambda, phi_w1, phi_b1, phi_w2, cmp_pos, mix_norm, w_out, norm_xa, norm_mem, xa_wq, xa_wkv, xa_wo, norm_ffn, router_gw, router_gb, router_ew, router_eb, exp_w_gate, exp_w_up, exp_w_down, final_norm):
    w = dict(norm_mix=norm_mix, w_in=w_in, ml_i_bias=ml_i_bias, ml_f_bias=ml_f_bias, conv_w=conv_w,
             conv_b=conv_b, lru_wa=lru_wa, lru_ba=lru_ba, lru_wx=lru_wx, lru_bx=lru_bx, lru_lambda=lru_lambda,
             phi_w1=phi_w1, phi_b1=phi_b1, phi_w2=phi_w2, cmp_pos=cmp_pos, mix_norm=mix_norm, w_out=w_out,
             norm_xa=norm_xa, norm_mem=norm_mem, xa_wq=xa_wq, xa_wkv=xa_wkv, xa_wo=xa_wo, norm_ffn=norm_ffn,
             router_gw=router_gw, router_gb=router_gb, router_ew=router_ew, router_eb=router_eb)
    depth = w_in.shape[0]
    B, T, _ = x_prompt.shape
    BS, TS, _ = x_sample.shape
    n_pages = page_table.shape[1]
    past = n_pages * PAGE
    n_phys = cache_nsa_kv.shape[1]
    consts = _constants(T)
    layers = [_prep_layer(w, l) for l in range(depth)]
    ew = (exp_w_gate.astype(BF16), exp_w_up.astype(BF16), exp_w_down.astype(BF16))
    fin = final_norm[None]

    x2 = x_prompt.reshape(B * T, D_MODEL)
    mem2 = mem_prompt.reshape(B * N_MEM, D_MODEL)
    outs_p, mem_p = [], []
    for l in range(depth):
        x2, st, mkv = _layer_prompt(x2, layers[l], consts, ew, B, T, mem2)
        outs_p.append(st)
        mem_p.append(mkv)
    y_prompt = _final_norm(x2, fin, 256).reshape(B, T, D_MODEL)

    pool_t = jnp.transpose(cache_nsa_kv, (0, 1, 3, 4, 5, 2)).reshape(depth * n_phys * 4 * NSA_G, HD, PAGE)
    win_t = jnp.transpose(state_nsa_win, (0, 1, 3, 4, 5, 2))
    xs = x_sample.reshape(BS * TS, D_MODEL)
    outs_s = []
    for l in range(depth):
        pt_flat = page_table.reshape(-1) + l * n_phys
        xs, st = _layer_sample(xs, layers[l], consts, ew, BS, TS, pool_t, pt_flat, state_nsa_win[l], win_t,
                               state_mlstm_C[l], state_mlstm_n[l], state_mlstm_m[l], state_rglru_conv[l],
                               state_rglru_h[l], cache_mem_kv[l], past)
        outs_s.append(st)
    y_sample = _final_norm(xs, fin, BS * TS).reshape(BS, TS, D_MODEL)

    sp = [jnp.stack(a) for a in zip(*outs_p)]
    ss = [jnp.stack(a) for a in zip(*outs_s)]
    return (y_prompt, y_sample, sp[0], ss[0], sp[1], ss[1], sp[2], ss[2], sp[3], ss[3], sp[4], ss[4],
            sp[5], ss[5], sp[6], ss[6], jnp.stack(mem_p))
```

```python
import functools
import math

import jax
import jax.numpy as jnp
import numpy as np
from jax import lax
from jax.experimental import pallas as pl
from jax.experimental.pallas import tpu as pltpu

F32 = jnp.float32
BF16 = jnp.bfloat16
I32 = jnp.int32

D_MODEL = 1024
DEPTH = 2
PAGE = 128
HD = 64
ML_W = 256
ML_H = 4
LRU_W = 256
LRU_C = 8.0
CONV_W = 4
NSA_W = 512
NSA_H = 8
NSA_G = 2
NSA_R = 4
CMP_BLK = 32
CMP_HID = 128
SLC_BLK = 64
N_SEL = 16
WINDOW = 512
FORCE_BONUS = 100.0
XA_H = 4
XA_HD = 128
XA_INNER = 512
N_MEM = 256
N_GROUPS = 4
EXP_PER_GROUP = 4
N_EXP = 16
EXP_FF = 256
EPS = 1e-6
NEG = -1e30
SLOPES = tuple(2.0 ** (-(h + 1)) for h in range(NSA_H))

LANES = 128
SUBLANES = 8
VMEM_LIMIT = 56 * 1024 * 1024

IN_ML = (0, 1024)
IN_LRU = (1024, 1536)
IN_Q = (1536, 2048)
IN_KV = (2048, 2560)
IN_WIN = (2560, 2816)
IN_SMALL = (2816, 2944)
IN_COLS_P = 2944

AUG_HI = 0
AUG_LO = 1
AUG_ONE = 2
AUG_SEL = 8
KEY_CHUNK = 256
ML_BATCH_ROWS = 1
CMP_STEP_PAGES = 128


def _cparams(sem):
    return pltpu.CompilerParams(dimension_semantics=sem, vmem_limit_bytes=VMEM_LIMIT)


def _dot(a, b):
    return jnp.dot(a, b, preferred_element_type=F32)


def _dot_nt(a, b):
    return lax.dot_general(a, b, (((1,), (1,)), ((), ())), preferred_element_type=F32)


def _dot_tn(a, b):
    return lax.dot_general(a, b, (((0,), (0,)), ((), ())), preferred_element_type=F32)


def _split2(x):
    hi = x.astype(BF16)
    lo = (x - hi.astype(F32)).astype(BF16)
    return hi, lo


def _dot_split(x, w_bf):
    hi, lo = _split2(x)
    return _dot(hi, w_bf) + _dot(lo, w_bf)


def _sigmoid(x):
    return 1.0 / (1.0 + jnp.exp(-x))


def _gelu(x):
    return 0.5 * x * (1.0 + jnp.tanh(0.7978845608028654 * (x + 0.044715 * (x * x * x))))


def _softplus(x):
    return jnp.maximum(x, 0.0) + jnp.log1p(jnp.exp(-jnp.abs(x)))


def _log_sigmoid(x):
    return -_softplus(-x)


def _rms(x, g):
    return x * lax.rsqrt(jnp.mean(x * x, axis=-1, keepdims=True) + EPS) * g


def _masked_softmax(s, mask):
    sm = jnp.where(mask, s, NEG)
    e = jnp.exp(sm - jnp.max(sm, axis=-1, keepdims=True))
    p = e / jnp.sum(e, axis=-1, keepdims=True)
    return jnp.where(mask, p, 0.0)


def _in_proj_body(x_ref, g_ref, w_ref, wst_ref, ml_ref, lru_ref, q_ref, kv_ref, win_ref, sm_ref,
                  smt_ref, kvsb_ref, winb_ref):
    hb = _rms(x_ref[...], g_ref[...]).astype(BF16)

    def mm(rng):
        return _dot(hb, w_ref[:, rng[0]:rng[1]])

    ml_ref[...] = mm(IN_ML)
    lru_ref[...] = mm(IN_LRU)
    q_ref[...] = mm(IN_Q).astype(BF16)
    kv = mm(IN_KV)
    kv_ref[...] = kv
    kvsb_ref[...] = kv[:, 256:512].astype(BF16)
    win = mm(IN_WIN)
    win_ref[...] = win
    winb_ref[...] = win.astype(BF16)
    sm_ref[...] = mm(IN_SMALL)
    smt_ref[...] = _dot_nt(wst_ref[...], hb)


def _in_proj(x2, g, w_p, w_st, tm):
    n = x2.shape[0]
    row = lambda w: pl.BlockSpec((tm, w), lambda i: (i, 0))
    full = lambda a: pl.BlockSpec(a.shape, lambda i: (0,) * a.ndim)
    shapes = [(1024, F32), (512, F32), (512, BF16), (512, F32), (256, F32), (128, F32)]
    out_shape = [jax.ShapeDtypeStruct((n, w), dt) for w, dt in shapes]
    out_specs = [row(w) for w, _ in shapes]
    out_shape += [jax.ShapeDtypeStruct((8, n), F32), jax.ShapeDtypeStruct((n, 256), BF16),
                  jax.ShapeDtypeStruct((n, 256), BF16)]
    out_specs += [pl.BlockSpec((8, tm), lambda i: (0, i)), row(256), row(256)]
    return pl.pallas_call(
        _in_proj_body, grid=(n // tm,),
        in_specs=[row(D_MODEL), full(g), full(w_p), full(w_st)],
        out_specs=out_specs, out_shape=out_shape,
        compiler_params=_cparams(("parallel",)), name="in_proj",
    )(x2, g, w_p, w_st)


def _norm_mm_body(x_ref, g_ref, w_ref, o_ref):
    o_ref[...] = _dot(_rms(x_ref[...], g_ref[...]).astype(BF16), w_ref[...]).astype(o_ref.dtype)


def _norm_mm(x2, g, w_bf, tm, out_dtype):
    n, k = x2.shape
    m = w_bf.shape[1]
    return pl.pallas_call(
        _norm_mm_body, grid=(n // tm,),
        in_specs=[pl.BlockSpec((tm, k), lambda i: (i, 0)), pl.BlockSpec((1, k), lambda i: (0, 0)),
                  pl.BlockSpec((k, m), lambda i: (0, 0))],
        out_specs=pl.BlockSpec((tm, m), lambda i: (i, 0)),
        out_shape=jax.ShapeDtypeStruct((n, m), out_dtype),
        compiler_params=_cparams(("parallel",)), name="norm_mm",
    )(x2, g, w_bf)


def _mm_res_body(a_ref, w_ref, x_ref, o_ref):
    o_ref[...] = x_ref[...] + _dot(a_ref[...], w_ref[...])


def _mm_res(a_bf, w_bf, x2, tm):
    n, k = a_bf.shape
    m = w_bf.shape[1]
    return pl.pallas_call(
        _mm_res_body, grid=(n // tm,),
        in_specs=[pl.BlockSpec((tm, k), lambda i: (i, 0)), pl.BlockSpec((k, m), lambda i: (0, 0)),
                  pl.BlockSpec((tm, m), lambda i: (i, 0))],
        out_specs=pl.BlockSpec((tm, m), lambda i: (i, 0)),
        out_shape=jax.ShapeDtypeStruct((n, m), F32),
        compiler_params=_cparams(("parallel",)), name="mm_res",
    )(a_bf, w_bf, x2)


def _final_norm_body(x_ref, g_ref, o_ref):
    o_ref[...] = _rms(x_ref[...], g_ref[...])


def _final_norm(x2, g, tm):
    n, k = x2.shape
    return pl.pallas_call(
        _final_norm_body, grid=(n // tm,),
        in_specs=[pl.BlockSpec((tm, k), lambda i: (i, 0)), pl.BlockSpec((1, k), lambda i: (0, 0))],
        out_specs=pl.BlockSpec((tm, k), lambda i: (i, 0)),
        out_shape=jax.ShapeDtypeStruct((n, k), F32),
        compiler_params=_cparams(("parallel",)), name="final_norm",
    )(x2, g)


def _mlstm_body(u_ref, sm_ref, smt_ref, bcol_ref, brow_ref, c0_ref, n0_ref, m0_ref,
                y_ref, c_ref, n_ref, m_ref, c_s, n_s, m_s, *, L, t_real, nbb):
    ci = pl.program_id(1)

    @pl.when(ci == 0)
    def _():
        c_s[...] = c0_ref[...]
        n_s[...] = n0_ref[...]
        m_s[...] = m0_ref[...]

    row = lax.broadcasted_iota(I32, (L, L), 0)
    col = lax.broadcasted_iota(I32, (L, L), 1)
    causal = col <= row
    lane = lax.broadcasted_iota(I32, (1, LANES), 1)
    real_col = lax.broadcasted_iota(I32, (L, 1), 0) < t_real
    real_row = lax.broadcasted_iota(I32, (1, L), 1) < t_real
    for bb in range(nbb):
        sm = sm_ref[bb] + bcol_ref[...]
        smt = smt_ref[bb] + brow_ref[...]
        m_all = m_s[bb]
        m_next = m_all
        for h in range(ML_H):
            q = u_ref[bb, :, 64 * h:64 * h + 64]
            k = u_ref[bb, :, 256 + 64 * h:256 + 64 * h + 64] * (HD ** -0.5)
            v = u_ref[bb, :, 512 + 64 * h:512 + 64 * h + 64]
            o = u_ref[bb, :, 768 + 64 * h:768 + 64 * h + 64]
            ig_col = jnp.where(real_col, sm[:, h:h + 1], NEG)
            lf_col = jnp.where(real_col, _log_sigmoid(sm[:, 4 + h:5 + h]), 0.0)
            ig_row = jnp.where(real_row, smt[h:h + 1, :], NEG)
            lf_row = jnp.where(real_row, _log_sigmoid(smt[4 + h:5 + h, :]), 0.0)
            b_col = jnp.sum(jnp.where(causal, lf_row, 0.0), axis=1, keepdims=True)
            b_row = jnp.sum(jnp.where(row <= col, lf_col, 0.0), axis=0, keepdims=True)
            m_prev = m_all[:, h:h + 1]
            log_d = jnp.where(causal, b_col - b_row + ig_row, NEG)
            inter = b_col + m_prev
            m_t = jnp.maximum(inter, jnp.max(log_d, axis=1, keepdims=True))
            w_carry = jnp.exp(inter - m_t)
            qb = q.astype(BF16)
            vb = v.astype(BF16)
            s = _dot_nt(qb, k.astype(BF16)) * jnp.exp(log_d - m_t)
            c_old = c_s[bb, h]
            n_old = n_s[bb, h]
            num = _dot(s.astype(BF16), vb) + w_carry * _dot_nt(qb, c_old.astype(BF16))
            den = jnp.sum(s, axis=1, keepdims=True) + w_carry * jnp.sum(q * n_old, axis=1, keepdims=True)
            hh = num / jnp.maximum(jnp.abs(den), jnp.exp(-m_t))
            y_ref[bb, :, 64 * h:64 * h + 64] = _sigmoid(o) * hh
            b_end = b_col[L - 1:L, :]
            log_w_row = b_end - b_row + ig_row
            m_new = jnp.maximum(b_end + m_prev, jnp.max(log_w_row, axis=1, keepdims=True))
            w_col = jnp.exp(b_end - b_col + ig_col - m_new)
            decay = jnp.exp(b_end + m_prev - m_new)
            c_s[bb, h] = decay * c_old + _dot_tn((v * w_col).astype(BF16), k.astype(BF16))
            n_s[bb, h] = decay * n_old + jnp.sum(k * w_col, axis=0, keepdims=True)
            m_next = jnp.where(lane == h, m_new, m_next)
        m_s[bb] = m_next

    @pl.when(ci == pl.num_programs(1) - 1)
    def _():
        c_ref[...] = c_s[...]
        n_ref[...] = n_s[...]
        m_ref[...] = m_s[...]


def _mlstm(u_ml, u_sm, u_smt, bcol, brow, c0, n0, m0, L, t_real):
    b, t, _ = u_ml.shape
    nc = t // L
    assert t_real == L or nc == 1
    nbb = ML_BATCH_ROWS
    assert b % nbb == 0
    return pl.pallas_call(
        functools.partial(_mlstm_body, L=L, t_real=t_real, nbb=nbb), grid=(b // nbb, nc),
        in_specs=[pl.BlockSpec((nbb, L, 1024), lambda i, c: (i, c, 0)),
                  pl.BlockSpec((nbb, L, 128), lambda i, c: (i, c, 0)),
                  pl.BlockSpec((nbb, 8, L), lambda i, c: (i, 0, c)),
                  pl.BlockSpec((1, 128), lambda i, c: (0, 0)),
                  pl.BlockSpec((8, 1), lambda i, c: (0, 0)),
                  pl.BlockSpec((nbb, 4, 64, 64), lambda i, c: (i, 0, 0, 0)),
                  pl.BlockSpec((nbb, 4, 1, 64), lambda i, c: (i, 0, 0, 0)),
                  pl.BlockSpec((nbb, 1, 128), lambda i, c: (i, 0, 0))],
        out_specs=[pl.BlockSpec((nbb, L, 256), lambda i, c: (i, c, 0)),
                   pl.BlockSpec((nbb, 4, 64, 64), lambda i, c: (i, 0, 0, 0)),
                   pl.BlockSpec((nbb, 4, 1, 64), lambda i, c: (i, 0, 0, 0)),
                   pl.BlockSpec((nbb, 1, 128), lambda i, c: (i, 0, 0))],
        out_shape=[jax.ShapeDtypeStruct((b, t, 256), F32), jax.ShapeDtypeStruct((b, 4, 64, 64), F32),
                   jax.ShapeDtypeStruct((b, 4, 1, 64), F32), jax.ShapeDtypeStruct((b, 1, 128), F32)],
        scratch_shapes=[pltpu.VMEM((nbb, 4, 64, 64), F32), pltpu.VMEM((nbb, 4, 1, 64), F32),
                        pltpu.VMEM((nbb, 1, 128), F32)],
        compiler_params=_cparams(("parallel", "arbitrary")), name="mlstm",
    )(u_ml, u_sm, u_smt, bcol, brow, c0, n0, m0)


def _lru_body(u_ref, cb_ref, h0_ref, cw_ref, cbias_ref, wa_ref, ba_ref, wx_ref, bx_ref, lam_ref,
              y_ref, tail_ref, hl_ref, tail_s, h_s, a_s, hs_s, *, tt, nb):
    i = pl.program_id(0)

    @pl.when(i == 0)
    def _():
        tail_s[...] = cb_ref[...]
        h_s[...] = h0_ref[...]

    x = u_ref[:, :, 0:LRU_W]
    g = u_ref[:, :, LRU_W:2 * LRU_W]
    xe = jnp.concatenate([tail_s[...], x], axis=0)
    xc = cbias_ref[...] + xe[0:tt] * cw_ref[0:1, :]
    for j in range(1, CONV_W):
        xc = xc + xe[j:j + tt] * cw_ref[j:j + 1, :]
    xc2 = xc.reshape(tt * nb, LRU_W)
    xcb = xc2.astype(BF16)
    r = _sigmoid(_dot(xcb, wa_ref[...]) + ba_ref[...])
    ig = _sigmoid(_dot(xcb, wx_ref[...]) + bx_ref[...])
    log_a = -LRU_C * r * _softplus(-lam_ref[...])
    a = jnp.exp(log_a)
    mult = jnp.sqrt(jnp.tanh(-log_a) * (a * a + 1.0))
    a_s[...] = a.reshape(tt, nb, LRU_W)
    hs_s[...] = (mult * (ig * xc2)).reshape(tt, nb, LRU_W)

    def step(t, h):
        h = a_s[t] * h + hs_s[t]
        hs_s[t] = h
        return h

    h_last = lax.fori_loop(0, tt, step, h_s[...], unroll=min(8, tt))
    h_s[...] = h_last
    y_ref[...] = hs_s[...] * _gelu(g)
    tail_s[...] = xe[tt:tt + CONV_W - 1]

    @pl.when(i == pl.num_programs(0) - 1)
    def _():
        tail_ref[...] = tail_s[...]
        hl_ref[...] = h_s[...]


def _lru(u_tm, cb_tm, h0, cw, cbias, wa_bd, ba, wx_bd, bx, lam, tt):
    t, nb, _ = u_tm.shape
    full = lambda a: pl.BlockSpec(a.shape, lambda i: (0,) * a.ndim)
    return pl.pallas_call(
        functools.partial(_lru_body, tt=tt, nb=nb), grid=(t // tt,),
        in_specs=[pl.BlockSpec((tt, nb, 512), lambda i: (i, 0, 0)), full(cb_tm), full(h0), full(cw),
                  full(cbias), full(wa_bd), full(ba), full(wx_bd), full(bx), full(lam)],
        out_specs=[pl.BlockSpec((tt, nb, 256), lambda i: (i, 0, 0)),
                   pl.BlockSpec((3, nb, 256), lambda i: (0, 0, 0)),
                   pl.BlockSpec((nb, 256), lambda i: (0, 0))],
        out_shape=[jax.ShapeDtypeStruct((t, nb, 256), F32), jax.ShapeDtypeStruct((3, nb, 256), F32),
                   jax.ShapeDtypeStruct((nb, 256), F32)],
        scratch_shapes=[pltpu.VMEM((3, nb, 256), F32), pltpu.VMEM((nb, 256), F32),
                        pltpu.VMEM((tt, nb, 256), F32), pltpu.VMEM((tt, nb, 256), F32)],
        compiler_params=_cparams(("arbitrary",)), name="rglru",
    )(u_tm, cb_tm, h0, cw, cbias, wa_bd, ba, wx_bd, bx, lam)


def _compress_rows_body(pt_ref, pool_ref, bdk_ref, bdv_ref, posk_ref, posv_ref, b1k_ref, b1v_ref,
                        w2k_ref, w2v_ref, kc_ref, vc_ref, buf, sem, *, P):
    s = pl.program_id(0)
    ns = pl.num_programs(0)
    nblk = P * (PAGE // CMP_BLK)

    def page_copy(step, slot, p, c):
        return pltpu.make_async_copy(
            pool_ref.at[pt_ref[step * P + p], :, pl.ds(LANES * c, LANES)],
            buf.at[slot, c, pl.ds(p * PAGE, PAGE), :], sem.at[slot])

    def start_all(step, slot):
        def body(p, carry):
            page_copy(step, slot, p, 0).start()
            page_copy(step, slot, p, 1).start()
            return carry
        lax.fori_loop(0, P, body, 0)

    def wait_all(step, slot):
        def body(p, carry):
            page_copy(step, slot, p, 0).wait()
            page_copy(step, slot, p, 1).wait()
            return carry
        lax.fori_loop(0, P, body, 0)

    @pl.when(s == 0)
    def _():
        start_all(0, 0)

    slot = lax.rem(s, 2)

    @pl.when(s + 1 < ns)
    def _():
        start_all(s + 1, 1 - slot)

    wait_all(s, slot)

    acc_k = jnp.zeros((nblk, 256), F32)
    acc_v = jnp.zeros((nblk, 256), F32)
    for r in range(CMP_BLK):
        rows = pl.ds(r, nblk, stride=CMP_BLK)
        xk = (buf[slot, 0, rows, :] + posk_ref[r:r + 1, :]).astype(BF16)
        xv = (buf[slot, 1, rows, :] + posv_ref[r:r + 1, :]).astype(BF16)
        acc_k = acc_k + _dot(xk, bdk_ref[r])
        acc_v = acc_v + _dot(xv, bdv_ref[r])
    kc_ref[...] = _dot(_gelu(acc_k + b1k_ref[...]).astype(BF16), w2k_ref[...])
    vc_ref[...] = _dot(_gelu(acc_v + b1v_ref[...]).astype(BF16), w2v_ref[...])


def _compress_rows(pt_flat, pool3, cw, P):
    n_pages = pt_flat.shape[0]
    nblk = P * (PAGE // CMP_BLK)
    full = lambda a: pl.BlockSpec(a.shape, lambda i, pt: (0,) * a.ndim)
    ws = (cw["bdk"], cw["bdv"], cw["posk"], cw["posv"], cw["b1k"], cw["b1v"], cw["w2k"], cw["w2v"])
    grid_spec = pltpu.PrefetchScalarGridSpec(
        num_scalar_prefetch=1, grid=(n_pages // P,),
        in_specs=[pl.BlockSpec(memory_space=pl.ANY)] + [full(a) for a in ws],
        out_specs=[pl.BlockSpec((nblk, 128), lambda i, pt: (i, 0)),
                   pl.BlockSpec((nblk, 128), lambda i, pt: (i, 0))],
        scratch_shapes=[pltpu.VMEM((2, 2, P * PAGE, LANES), F32), pltpu.SemaphoreType.DMA((2,))])
    return pl.pallas_call(
        functools.partial(_compress_rows_body, P=P), grid_spec=grid_spec,
        out_shape=[jax.ShapeDtypeStruct((n_pages * 4, 128), F32)] * 2,
        compiler_params=_cparams(("arbitrary",)), name="compress_rows",
    )(pt_flat, pool3, *ws)


def _compress_pages_body(pt_ref, pool_ref, w_ref, pos_ref, b1_ref, w2_ref, o_ref, buf, sem, *, P, nstep):
    c = pl.program_id(0)
    s = pl.program_id(1)
    lin = c * nstep + s
    PG = NSA_G * P

    def tile_copy(cc, step, slot, p):
        return pltpu.make_async_copy(
            pool_ref.at[pl.ds((pt_ref[step * P + p] * 4 + cc) * NSA_G, NSA_G)],
            buf.at[slot, :, :, p, :], sem.at[slot])

    def start_all(cc, step, slot):
        def body(p, carry):
            tile_copy(cc, step, slot, p).start()
            return carry
        lax.fori_loop(0, P, body, 0, unroll=4)

    def wait_all(slot):
        pltpu.make_async_copy(buf.at[slot], buf.at[slot], sem.at[slot]).wait()

    @pl.when(lin == 0)
    def _():
        start_all(0, 0, 0)

    slot = lax.rem(lin, 2)
    nxt = lin + 1

    @pl.when(nxt < 2 * nstep)
    def _():
        start_all(lax.div(nxt, nstep), lax.rem(nxt, nstep), 1 - slot)

    wait_all(slot)

    acc = jnp.zeros((PG, 4 * CMP_HID), F32)
    for dp in range(HD // 2):
        chan = [jnp.concatenate([buf[slot, 0, d], buf[slot, 1, d]], axis=0) for d in (2 * dp, 2 * dp + 1)]
        lhs = jnp.concatenate(chan, axis=1) + pos_ref[dp:dp + 1, :]
        acc = acc + _dot(lhs.astype(BF16), w_ref[dp])
    h = _gelu(acc + b1_ref[...])
    out = _dot(h.astype(BF16), w2_ref[...])
    for blk in range(PAGE // CMP_BLK):
        cols = slice(HD * blk, HD * blk + HD)
        o_ref[blk] = jnp.concatenate([out[0:P, cols], out[P:2 * P, cols]], axis=1)


def _compress_pages(pt_flat, pool_t, cw, P):
    n_pages = pt_flat.shape[0]
    nstep = n_pages // P
    PG = NSA_G * P
    grid_spec = pltpu.PrefetchScalarGridSpec(
        num_scalar_prefetch=1, grid=(2, nstep),
        in_specs=[pl.BlockSpec(memory_space=pl.ANY),
                  pl.BlockSpec((None, HD // 2, 256, 512), lambda c, s, pt: (c, 0, 0, 0)),
                  pl.BlockSpec((None, HD // 2, 256), lambda c, s, pt: (c, 0, 0)),
                  pl.BlockSpec((None, 1, 512), lambda c, s, pt: (c, 0, 0)),
                  pl.BlockSpec((None, 512, 256), lambda c, s, pt: (c, 0, 0))],
        out_specs=pl.BlockSpec((None, PAGE // CMP_BLK, P, LANES), lambda c, s, pt: (c, 0, s, 0)),
        scratch_shapes=[pltpu.VMEM((2, NSA_G, HD, P, LANES), F32), pltpu.SemaphoreType.DMA((2,))])
    return pl.pallas_call(
        functools.partial(_compress_pages_body, P=P, nstep=nstep), grid_spec=grid_spec,
        out_shape=jax.ShapeDtypeStruct((2, PAGE // CMP_BLK, n_pages, LANES), F32),
        compiler_params=_cparams(("arbitrary", "arbitrary")), name="compress_pages",
    )(pt_flat, pool_t, cw["pw"], cw["ppos"], cw["pb1"], cw["pw2"])


def _half_mask(g):
    lane = lax.broadcasted_iota(I32, (1, LANES), 1)
    return (lane >= 64 * g) & (lane < 64 * g + 64)


def _gate_expand(gates, gexp_ref):
    hi, lo = _split2(gates)
    return [_dot(hi, gexp_ref[c]) + _dot(lo, gexp_ref[c]) for c in range(3)]


def _nsa_prompt_body(q_ref, sm_ref, kc_ref, vc_ref, kvs_ref, kaug_ref, win_ref, gexp_ref, o_ref,
                     s_scr, mx_scr, acc_scr, *, T):
    QT = 128
    M = NSA_H * QT
    CK = KEY_CHUNK
    q0 = pl.program_id(1) * QT
    lane = lax.broadcasted_iota(I32, (1, LANES), 1)
    left = lane < 64
    rowm = lax.broadcasted_iota(I32, (M, 1), 0)
    qposm = q0 + (rowm & (QT - 1))
    slopem = jnp.zeros((M, 1), F32)
    for h in range(NSA_H):
        slopem = jnp.where((rowm >= QT * h) & (rowm < QT * (h + 1)), SLOPES[h], slopem)

    nce = T // CMP_BLK // 2
    blk = jnp.where(lane < nce, 2 * lane, 2 * (lane - nce) + 1)
    c_end = jnp.where(lane < 2 * nce, blk * CMP_BLK + (CMP_BLK - 1), 1 << 30)
    dist_c = qposm - c_end
    mask_c = dist_c >= 0
    zpad = jnp.zeros((LANES - 2 * nce, LANES), F32)
    kc = jnp.concatenate([kc_ref[...], zpad], axis=0).astype(BF16)
    vc = jnp.concatenate([vc_ref[...], zpad], axis=0).astype(BF16)

    qs = [jnp.where(_half_mask(g), q_ref[:, 128 * r:128 * r + 128], 0)
          for g in range(NSA_G) for r in range(NSA_R)]
    qm = jnp.concatenate(qs, axis=0)

    s = _dot_nt(qm, kc) * (HD ** -0.5) - slopem * dist_c.astype(F32)
    p = _masked_softmax(s, mask_c)
    o_cmp = _dot(p.astype(BF16), vc)

    ns = T // SLC_BLK
    jrow = lax.broadcasted_iota(I32, (ns, 1), 0)
    cur = (q0 + lane) // SLC_BLK
    in_sel = (lane >= AUG_SEL) & (lane < AUG_SEL + ns)
    valid = jrow <= cur
    forced = (jrow == 0) | (jrow == cur) | (jrow == cur - 1)
    sel_bias = []
    for g in range(NSA_G):
        b0 = g * NSA_R * QT
        imp = p[b0:b0 + QT] + p[b0 + QT:b0 + 2 * QT] + p[b0 + 2 * QT:b0 + 3 * QT] + p[b0 + 3 * QT:b0 + 4 * QT]
        imp_t = imp.T
        pooled = imp_t[0:nce] + imp_t[nce:2 * nce]
        score = jnp.where(valid, pooled + FORCE_BONUS * forced.astype(F32), NEG)
        cnt = jnp.zeros((ns, LANES), F32)
        for i2 in range(ns):
            si = score[i2:i2 + 1, :]
            beats = (si > score) | ((si == score) & (i2 < jrow))
            cnt = cnt + beats.astype(F32)
        sel_t = ((cnt < N_SEL) & (score > 0.5 * NEG)).astype(F32)
        selp = jnp.concatenate([jnp.zeros((AUG_SEL, LANES), F32), sel_t,
                                jnp.zeros((LANES - AUG_SEL - ns, LANES), F32)], axis=0).T
        sel_bias.append(jnp.where(in_sel, (selp - 1.0) * (-NEG), 0.0))

    shift = -(q0 + QT - 1).astype(F32)

    def aug(h, base):
        sl = SLOPES[h]
        a = jnp.where(lane == AUG_HI, sl * SLC_BLK, jnp.where(lane == AUG_LO, sl,
                      jnp.where(lane == AUG_ONE, sl * shift, base)))
        return a.astype(BF16)

    qsc = [(qs[h].astype(F32) * (HD ** -0.5)).astype(BF16) for h in range(NSA_H)]
    zero_t = jnp.zeros((QT, LANES), F32)
    qsel = jnp.concatenate([jnp.concatenate([qsc[h], aug(h, sel_bias[h // NSA_R])], axis=1)
                            for h in range(NSA_H)], axis=0)
    qwin = jnp.concatenate([jnp.concatenate([qsc[h], aug(h, zero_t)], axis=1) for h in range(NSA_H)], axis=0)
    ones_k = jnp.ones((CK, LANES), BF16)

    def key_chunk(c):
        r0 = pl.multiple_of(c * CK, CK)
        return jnp.concatenate([kvs_ref[pl.ds(r0, CK), 0:128], kaug_ref[pl.ds(r0, CK), :]], axis=1)

    def fold(sc):
        return jnp.maximum(sc[:, 0:128], sc[:, 128:256])

    nk = q0 // CK + 1
    mx_scr[...] = jnp.full((M, LANES), NEG, F32)

    def body1(c, carry):
        sc = _dot_nt(qsel, key_chunk(c))
        s_scr[c] = sc
        mx_scr[...] = jnp.maximum(mx_scr[...], fold(sc))
        return carry

    lax.fori_loop(0, nk - 1, body1, 0)
    cl = nk - 1
    kpos_l = cl * CK + lax.broadcasted_iota(I32, (1, CK), 1)
    sc = jnp.where(kpos_l <= qposm, _dot_nt(qsel, key_chunk(cl)), NEG)
    s_scr[cl] = sc
    m = jnp.max(jnp.maximum(mx_scr[...], fold(sc)), axis=1, keepdims=True)

    acc_scr[...] = jnp.zeros((M, 2 * LANES), F32)

    def body2(c, carry):
        e = jnp.exp(s_scr[c] - m).astype(BF16)
        r0 = pl.multiple_of(c * CK, CK)
        vo = jnp.concatenate([kvs_ref[pl.ds(r0, CK), 128:256], ones_k], axis=1)
        acc_scr[...] += _dot(e, vo)
        return carry

    lax.fori_loop(0, nk, body2, 0)
    acc = acc_scr[...]
    o_slc = acc[:, 0:128] / acc[:, 128:129]

    w0 = pl.multiple_of(jnp.maximum(q0 - WINDOW, 0), QT)
    WK = WINDOW + QT
    dist_w = qposm - (w0 + lax.broadcasted_iota(I32, (1, WK), 1))
    mask_w = (dist_w >= 0) & (dist_w < WINDOW)
    kwin = jnp.concatenate([win_ref[pl.ds(w0, WK), 0:128], kaug_ref[pl.ds(w0, WK), :]], axis=1)
    sw = jnp.where(mask_w, _dot_nt(qwin, kwin), NEG)
    e = jnp.exp(sw - jnp.max(sw, axis=1, keepdims=True)).astype(BF16)
    vo = jnp.concatenate([win_ref[pl.ds(w0, WK), 128:256], jnp.ones((WK, LANES), BF16)], axis=1)
    accw = _dot(e, vo)
    o_win = accw[:, 0:128] / accw[:, 128:129]

    gates = _sigmoid(sm_ref[...])
    gc, gs, gw = _gate_expand(gates, gexp_ref)
    for r in range(NSA_R):
        sl = slice(128 * r, 128 * r + 128)
        r0 = slice(QT * r, QT * (r + 1))
        r1 = slice(QT * (NSA_R + r), QT * (NSA_R + r + 1))
        oc = jnp.where(left, o_cmp[r0], o_cmp[r1])
        os_ = jnp.where(left, o_slc[r0], o_slc[r1])
        ow = jnp.where(left, o_win[r0], o_win[r1])
        o_ref[:, sl] = gc[:, sl] * oc + gs[:, sl] * os_ + gw[:, sl] * ow


def _nsa_prompt(q, sm, kc_eo, vc_eo, kvs_bf, win_bf, kaug, gexp):
    b, t, _ = q.shape
    return pl.pallas_call(
        functools.partial(_nsa_prompt_body, T=t), grid=(b, t // 128),
        in_specs=[pl.BlockSpec((None, 128, 512), lambda i, j: (i, j, 0)),
                  pl.BlockSpec((None, 128, 128), lambda i, j: (i, j, 0)),
                  pl.BlockSpec((None, t // CMP_BLK, 128), lambda i, j: (i, 0, 0)),
                  pl.BlockSpec((None, t // CMP_BLK, 128), lambda i, j: (i, 0, 0)),
                  pl.BlockSpec((None, t, 256), lambda i, j: (i, 0, 0)),
                  pl.BlockSpec(kaug.shape, lambda i, j: (0, 0)),
                  pl.BlockSpec((None, t, 256), lambda i, j: (i, 0, 0)),
                  pl.BlockSpec(gexp.shape, lambda i, j: (0, 0, 0))],
        out_specs=pl.BlockSpec((None, 128, 512), lambda i, j: (i, j, 0)),
        out_shape=jax.ShapeDtypeStruct((b, t, 512), F32),
        scratch_shapes=[pltpu.VMEM((t // KEY_CHUNK, NSA_H * 128, KEY_CHUNK), F32),
                        pltpu.VMEM((NSA_H * 128, LANES), F32), pltpu.VMEM((NSA_H * 128, 2 * LANES), F32)],
        compiler_params=_cparams(("parallel", "arbitrary")), name="nsa_prompt",
    )(q, sm, kc_eo, vc_eo, kvs_bf, kaug, win_bf, gexp)


def _sel_block_of_lane(lane, n_pages):
    return jnp.where(lane < n_pages, 2 * lane,
                     jnp.where(lane < 2 * n_pages, 2 * (lane - n_pages) + 1,
                               jnp.where(lane == 2 * n_pages, 2 * n_pages, 1 << 20)))


def _nsa_sample_sel_body(q_ref, kc_ref, vc_ref, wbuf_ref, wnew_ref, oc_ref, ow_ref, sc_ref, *, past, n_pages):
    R8 = 8
    per_page = PAGE // CMP_BLK
    ncmp = per_page * n_pages
    qpos = past + lax.broadcasted_iota(I32, (R8, 1), 0)
    lane_c = lax.broadcasted_iota(I32, (1, ncmp), 1)
    blk = (lane_c % n_pages) * per_page + lane_c // n_pages
    dist_c = qpos - (blk * CMP_BLK + (CMP_BLK - 1))
    mask_c = dist_c >= 0
    dist_cf = dist_c.astype(F32)
    kc = kc_ref[...].reshape(ncmp, LANES).astype(BF16)
    vc = vc_ref[...].reshape(ncmp, LANES).astype(BF16)

    wb = wbuf_ref.shape[3]
    wk = wbuf_ref[0].reshape(NSA_G * HD, wb).astype(BF16)
    wv = wbuf_ref[1].reshape(NSA_G * HD, wb).astype(BF16)
    nk = wnew_ref[:, 0:128].astype(BF16)
    nv = wnew_ref[:, 128:256].astype(BF16)
    dist_w1 = qpos - (past - wb + lax.broadcasted_iota(I32, (1, wb), 1))
    mask_w1 = (dist_w1 >= 0) & (dist_w1 < WINDOW)
    dist_w2 = qpos - (past + lax.broadcasted_iota(I32, (1, R8), 1))
    mask_w2 = (dist_w2 >= 0) & (dist_w2 < WINDOW)

    ns_l = 3 * LANES
    assert 2 * n_pages + 1 <= ns_l
    lane_s = lax.broadcasted_iota(I32, (1, ns_l), 1)
    jmap = _sel_block_of_lane(lane_s, n_pages)
    cur = qpos // SLC_BLK
    lane16 = lax.broadcasted_iota(I32, (1, LANES), 1)
    left = lane16 < 64

    oc = [[None] * NSA_R for _ in range(NSA_G)]
    ow = [[None] * NSA_R for _ in range(NSA_G)]
    for g in range(NSA_G):
        hm = _half_mask(g)
        imp = jnp.zeros((R8, ncmp), F32)
        for r in range(NSA_R):
            slope = SLOPES[g * NSA_R + r]
            qh = jnp.where(hm, q_ref[:, 128 * r:128 * r + 128], 0).astype(BF16)
            s = _dot_nt(qh, kc) * (HD ** -0.5) - slope * dist_cf
            p = _masked_softmax(s, mask_c)
            imp = imp + p
            oc[g][r] = _dot(p.astype(BF16), vc)
            s1 = jnp.where(mask_w1, _dot(qh, wk) * (HD ** -0.5) - slope * dist_w1.astype(F32), NEG)
            s2 = jnp.where(mask_w2, _dot_nt(qh, nk) * (HD ** -0.5) - slope * dist_w2.astype(F32), NEG)
            mx = jnp.maximum(jnp.max(s1, axis=1, keepdims=True), jnp.max(s2, axis=1, keepdims=True))
            e1 = jnp.exp(s1 - mx)
            e2 = jnp.exp(s2 - mx)
            den = jnp.sum(e1, axis=1, keepdims=True) + jnp.sum(e2, axis=1, keepdims=True)
            p1 = jnp.where(mask_w1, e1 / den, 0.0)
            p2 = jnp.where(mask_w2, e2 / den, 0.0)
            ow[g][r] = _dot_nt(p1.astype(BF16), wv) + _dot(p2.astype(BF16), nv)
        np_ = n_pages
        pooled = jnp.concatenate([imp[:, 0:np_] + imp[:, np_:2 * np_], imp[:, 2 * np_:3 * np_] + imp[:, 3 * np_:],
                                  jnp.zeros((R8, ns_l - 2 * np_), F32)], axis=1)
        valid = jmap <= cur
        forced = (jmap == 0) | (jmap == cur) | (jmap == cur - 1)
        score = jnp.where(valid, pooled + FORCE_BONUS * forced.astype(F32), NEG)
        sc_ref[g] = jnp.where(lane_s <= 2 * np_, score, -jnp.inf)
    for r in range(NSA_R):
        sl = slice(128 * r, 128 * r + 128)
        oc_ref[:, sl] = jnp.where(left, oc[0][r], oc[1][r])
        ow_ref[:, sl] = jnp.where(left, ow[0][r], ow[1][r])


def _nsa_sample_sel(q8, kvc, win_t, layer, wnew8, past):
    b = q8.shape[0]
    n_pages = past // PAGE
    per_page = PAGE // CMP_BLK
    wb = win_t.shape[-1]
    return pl.pallas_call(
        functools.partial(_nsa_sample_sel_body, past=past, n_pages=n_pages), grid=(b,),
        in_specs=[pl.BlockSpec((None, 8, 512), lambda i: (i, 0, 0)),
                  pl.BlockSpec((None, per_page, n_pages, 128), lambda i: (0, 0, i, 0)),
                  pl.BlockSpec((None, per_page, n_pages, 128), lambda i: (1, 0, i, 0)),
                  pl.BlockSpec((None, None, 2, NSA_G, HD, wb), lambda i: (layer, i, 0, 0, 0, 0)),
                  pl.BlockSpec((None, 8, 256), lambda i: (i, 0, 0))],
        out_specs=[pl.BlockSpec((None, 8, 512), lambda i: (i, 0, 0)),
                   pl.BlockSpec((None, 8, 512), lambda i: (i, 0, 0)),
                   pl.BlockSpec((None, 2, 8, 3 * LANES), lambda i: (i, 0, 0, 0))],
        out_shape=[jax.ShapeDtypeStruct((b, 8, 512), F32), jax.ShapeDtypeStruct((b, 8, 512), F32),
                   jax.ShapeDtypeStruct((b, 2, 8, 3 * LANES), F32)],
        compiler_params=_cparams(("parallel",)), name="nsa_sample_sel",
    )(q8, kvc, kvc, win_t, wnew8)


def _topk_body(sc_ref, idx_ref, *, n_pages):
    score = sc_ref[...]
    rows, width = score.shape
    blk_f = _sel_block_of_lane(lax.broadcasted_iota(I32, (1, width), 1), n_pages).astype(F32)
    lane_o = lax.broadcasted_iota(I32, (1, LANES), 1)
    idx_acc = jnp.zeros((rows, LANES), F32)
    for n in range(N_SEL):
        mx = jnp.max(score, axis=1, keepdims=True)
        pick = jnp.min(jnp.where(score == mx, blk_f, 2e6), axis=1, keepdims=True)
        idx_acc = jnp.where(lane_o == n, jnp.where(mx > 0.5 * NEG, pick, -1.0), idx_acc)
        score = jnp.where(blk_f == pick, -jnp.inf, score)
    idx_ref[...] = idx_acc.astype(I32)


def _topk(score2, n_pages):
    rows, width = score2.shape
    return pl.pallas_call(
        functools.partial(_topk_body, n_pages=n_pages), grid=(1,),
        in_specs=[pl.BlockSpec((rows, width), lambda i: (0, 0))],
        out_specs=pl.BlockSpec((rows, LANES), lambda i: (0, 0)),
        out_shape=jax.ShapeDtypeStruct((rows, LANES), I32),
        compiler_params=_cparams(("arbitrary",)), name="topk",
    )(score2)


def _nsa_sample_slc_body(idx_ref, pt_ref, pool_ref, q_ref, knew_ref, sm_ref, oc_ref, ow_ref, gexp_ref,
                         o_ref, kbuf, vbuf, sem, *, T, past, n_pages):
    b = pl.program_id(0)
    ns_past = past // SLC_BLK
    per_page = PAGE // SLC_BLK

    def blk_copy(g, t, n, which):
        j = idx_ref[((b * NSA_G + g) * 8 + t) * N_SEL + n]
        jc = jnp.clip(j, 0, ns_past - 1)
        page = pt_ref[b * n_pages + jc // per_page]
        slot = (g * T + t) * N_SEL + n
        dst = kbuf if which == 0 else vbuf
        return pltpu.make_async_copy(pool_ref.at[(page * 4 + 2 + which) * NSA_G + g], dst.at[slot],
                                     sem.at[which])

    for g in range(NSA_G):
        for t in range(T):
            for n in range(N_SEL):
                blk_copy(g, t, n, 0).start()
                blk_copy(g, t, n, 1).start()
    pltpu.make_async_copy(kbuf, kbuf, sem.at[0]).wait()
    pltpu.make_async_copy(vbuf, vbuf, sem.at[1]).wait()

    R8 = 8
    NK = N_SEL * PAGE
    lane_k = lax.broadcasted_iota(I32, (1, NK), 1)
    slot_k = lane_k // PAGE
    within = lane_k - slot_k * PAGE
    half_k = within // SLC_BLK
    off_k = within - half_k * SLC_BLK
    rowi = lax.broadcasted_iota(I32, (R8, 1), 0)
    knew = knew_ref[:, 0:128].astype(BF16)
    vnew = knew_ref[:, 128:256].astype(BF16)
    kpos_new = past + lax.broadcasted_iota(I32, (1, R8), 1)
    zhalf = jnp.zeros((R8, HD), F32)
    o_acc = [jnp.zeros((R8, LANES), F32) for _ in range(NSA_R)]
    for g in range(NSA_G):
        hm = _half_mask(g)
        slope = jnp.zeros((R8, 1), F32)
        for r in range(NSA_R):
            slope = jnp.where(rowi == r, SLOPES[g * NSA_R + r], slope)
        for t in range(T):
            qpos = past + t
            qm = jnp.zeros((R8, LANES), F32)
            for r in range(NSA_R):
                qm = jnp.where(rowi == r, q_ref[t:t + 1, 128 * r:128 * r + 128], qm)
            qm = jnp.where(hm, qm, 0.0)
            qc = (qm[:, 0:HD] + qm[:, HD:2 * HD]).astype(BF16)
            base = (g * T + t) * N_SEL
            kt = jnp.concatenate([kbuf[base + n] for n in range(N_SEL)], axis=1).astype(BF16)
            vt = jnp.concatenate([vbuf[base + n] for n in range(N_SEL)], axis=1).astype(BF16)
            kpos = off_k
            ok_i = jnp.zeros((1, NK), I32)
            has_new = jnp.int32(0)
            for n in range(N_SEL):
                j = idx_ref[((b * NSA_G + g) * 8 + t) * N_SEL + n]
                in_blk = (slot_k == n) & (half_k == lax.rem(j, per_page))
                kpos = jnp.where(in_blk, j * SLC_BLK + off_k, kpos)
                ok_i = jnp.where(in_blk, ((j >= 0) & (j < ns_past)).astype(I32), ok_i)
                has_new = has_new + (j == ns_past).astype(I32)
            dist1 = qpos - kpos
            mask1 = (ok_i > 0) & (dist1 >= 0)
            dist2 = qpos - kpos_new
            mask2 = (dist2 >= 0) & ((jnp.zeros((1, R8), I32) + has_new) > 0)
            s1 = jnp.where(mask1, _dot(qc, kt) * (HD ** -0.5) - slope * dist1.astype(F32), NEG)
            s2 = jnp.where(mask2, _dot_nt(qm.astype(BF16), knew) * (HD ** -0.5) - slope * dist2.astype(F32), NEG)
            mx = jnp.maximum(jnp.max(s1, axis=1, keepdims=True), jnp.max(s2, axis=1, keepdims=True))
            e1 = jnp.exp(s1 - mx)
            e2 = jnp.exp(s2 - mx)
            den = jnp.sum(e1, axis=1, keepdims=True) + jnp.sum(e2, axis=1, keepdims=True)
            p1 = jnp.where(mask1, e1 / den, 0.0)
            p2 = jnp.where(mask2, e2 / den, 0.0)
            o1 = _dot_nt(p1.astype(BF16), vt)
            o1 = jnp.concatenate([o1, zhalf] if g == 0 else [zhalf, o1], axis=1)
            o = o1 + jnp.where(hm, _dot(p2.astype(BF16), vnew), 0.0)
            for r in range(NSA_R):
                o_acc[r] = jnp.where((rowi == t) & hm, o[r:r + 1, :], o_acc[r])
    gates = _sigmoid(sm_ref[...])
    gc, gs, gw = _gate_expand(gates, gexp_ref)
    for r in range(NSA_R):
        sl = slice(128 * r, 128 * r + 128)
        o_ref[:, sl] = gc[:, sl] * oc_ref[:, sl] + gs[:, sl] * o_acc[r] + gw[:, sl] * ow_ref[:, sl]


def _nsa_sample_slc(idx_flat, pt_flat, pool_t, q8, kvnew8, sm8, oc, ow, gexp, T, past):
    b = q8.shape[0]
    n_pages = pt_flat.shape[0] // b
    row = lambda w: pl.BlockSpec((None, 8, w), lambda i, a, c: (i, 0, 0))
    nslot = NSA_G * T * N_SEL
    grid_spec = pltpu.PrefetchScalarGridSpec(
        num_scalar_prefetch=2, grid=(b,),
        in_specs=[pl.BlockSpec(memory_space=pl.ANY), row(512), row(256), row(128), row(512), row(512),
                  pl.BlockSpec(gexp.shape, lambda i, a, c: (0, 0, 0))],
        out_specs=row(512),
        scratch_shapes=[pltpu.VMEM((nslot, HD, PAGE), F32), pltpu.VMEM((nslot, HD, PAGE), F32),
                        pltpu.SemaphoreType.DMA((2,))])
    return pl.pallas_call(
        functools.partial(_nsa_sample_slc_body, T=T, past=past, n_pages=n_pages), grid_spec=grid_spec,
        out_shape=jax.ShapeDtypeStruct((b, 8, 512), F32),
        compiler_params=_cparams(("arbitrary",)), name="nsa_sample_slc",
    )(idx_flat, pt_flat, pool_t, q8, kvnew8, sm8, oc, ow, gexp)


def _mix_out_body(ya_ref, yb_ref, yc_ref, x_ref, g_ref, w_ref, pool_ref, poolt_ref, o_ref):
    y = jnp.concatenate([ya_ref[...], yb_ref[...], yc_ref[...]], axis=-1)
    ms = _dot_split(y * y, pool_ref[...])
    rb = _dot_split(lax.rsqrt(ms + EPS), poolt_ref[...])
    yn = (y * rb * g_ref[...]).astype(BF16)
    o_ref[...] = x_ref[...] + _dot(yn, w_ref[...])


def _mix_out(ya, yb, yc, x2, g, w_bf, pool, poolt, tm):
    n = x2.shape[0]
    row = lambda w: pl.BlockSpec((tm, w), lambda i: (i, 0))
    full = lambda a: pl.BlockSpec(a.shape, lambda i: (0,) * a.ndim)
    return pl.pallas_call(
        _mix_out_body, grid=(n // tm,),
        in_specs=[row(256), row(256), row(512), row(1024), full(g), full(w_bf), full(pool), full(poolt)],
        out_specs=row(1024), out_shape=jax.ShapeDtypeStruct((n, 1024), F32),
        compiler_params=_cparams(("parallel",)), name="mix_out",
    )(ya, yb, yc, x2, g, w_bf, pool, poolt)


def _xattn_body(q_ref, kv_ref, o_ref):
    for h in range(XA_H):
        k = kv_ref[:, 128 * h:128 * h + 128].astype(BF16)
        v = kv_ref[:, XA_INNER + 128 * h:XA_INNER + 128 * h + 128].astype(BF16)
        s = _dot_nt(q_ref[:, 128 * h:128 * h + 128], k) * (XA_HD ** -0.5)
        e = jnp.exp(s - jnp.max(s, axis=-1, keepdims=True))
        a = e / jnp.sum(e, axis=-1, keepdims=True)
        o_ref[:, 128 * h:128 * h + 128] = _dot(a.astype(BF16), v).astype(BF16)


def _xattn(q, mkv, tq):
    b, t, _ = q.shape
    return pl.pallas_call(
        _xattn_body, grid=(b, t // tq),
        in_specs=[pl.BlockSpec((None, tq, 512), lambda i, j: (i, j, 0)),
                  pl.BlockSpec((None, N_MEM, 1024), lambda i, j: (i, 0, 0))],
        out_specs=pl.BlockSpec((None, tq, 512), lambda i, j: (i, j, 0)),
        out_shape=jax.ShapeDtypeStruct((b, t, 512), BF16),
        compiler_params=_cparams(("parallel", "arbitrary")), name="xattn",
    )(q, mkv)


def _moe_body(x_ref, g_ref, rwh_ref, rwl_ref, rb_ref, wg_ref, wu_ref, wd_ref, o_ref, zn_s, comb_s, acc_s):
    e = pl.program_id(1)
    lane = lax.broadcasted_iota(I32, (1, LANES), 1)

    @pl.when(e == 0)
    def _():
        z = _rms(x_ref[...], g_ref[...])
        zh, zl = _split2(z)
        zn_s[...] = zh
        logits = _dot(zh, rwh_ref[...]) + _dot(zh, rwl_ref[...]) + _dot(zl, rwh_ref[...]) + rb_ref[...]
        is_g = (lane >= N_EXP) & (lane < N_EXP + N_GROUPS)
        gl = jnp.where(is_g, logits, -jnp.inf)
        gmax = jnp.max(gl, axis=-1, keepdims=True)
        gw = 1.0 / jnp.sum(jnp.exp(gl - gmax), axis=-1, keepdims=True)
        lanef = lane.astype(F32)
        grpf = (lane // EXP_PER_GROUP).astype(F32)
        gsel = jnp.min(jnp.where(gl == gmax, lanef, 1e6), axis=-1, keepdims=True) - N_EXP
        in_grp = (lane < N_EXP) & (grpf == gsel)
        le = jnp.where(in_grp, logits, -jnp.inf)
        v1 = jnp.max(le, axis=-1, keepdims=True)
        i1 = jnp.min(jnp.where(le == v1, lanef, 1e6), axis=-1, keepdims=True)
        le2 = jnp.where(lanef == i1, -jnp.inf, le)
        v2 = jnp.max(le2, axis=-1, keepdims=True)
        i2 = jnp.min(jnp.where(le2 == v2, lanef, 1e6), axis=-1, keepdims=True)
        e2 = jnp.exp(v2 - v1)
        w1 = 1.0 / (1.0 + e2)
        w2 = e2 / (1.0 + e2)
        comb_s[...] = gw * (jnp.where(lanef == i1, w1, 0.0) + jnp.where(lanef == i2, w2, 0.0))
        acc_s[...] = jnp.zeros_like(acc_s)

    zn = zn_s[...]
    hg = _dot(zn, wg_ref[...])
    hu = _dot(zn, wu_ref[...])
    ce = jnp.sum(jnp.where(lane == e, comb_s[...], 0.0), axis=-1, keepdims=True)
    h = (hg * _sigmoid(hg)) * hu * ce
    acc_s[...] += _dot(h.astype(BF16), wd_ref[...])

    @pl.when(e == pl.num_programs(1) - 1)
    def _():
        o_ref[...] = x_ref[...] + acc_s[...]


def _moe(x2, g, rwh, rwl, rb, wg, wu, wd, layer, tm):
    n = x2.shape[0]
    full = lambda a: pl.BlockSpec(a.shape, lambda i, e: (0,) * a.ndim)
    return pl.pallas_call(
        _moe_body, grid=(n // tm, N_EXP),
        in_specs=[pl.BlockSpec((tm, 1024), lambda i, e: (i, 0)), full(g), full(rwh), full(rwl), full(rb),
                  pl.BlockSpec((None, None, 1024, EXP_FF), lambda i, e: (layer, e, 0, 0)),
                  pl.BlockSpec((None, None, 1024, EXP_FF), lambda i, e: (layer, e, 0, 0)),
                  pl.BlockSpec((None, None, EXP_FF, 1024), lambda i, e: (layer, e, 0, 0))],
        out_specs=pl.BlockSpec((tm, 1024), lambda i, e: (i, 0)),
        out_shape=jax.ShapeDtypeStruct((n, 1024), F32),
        scratch_shapes=[pltpu.VMEM((tm, 1024), BF16), pltpu.VMEM((tm, 128), F32), pltpu.VMEM((tm, 1024), F32)],
        compiler_params=_cparams(("parallel", "arbitrary")), name="moe",
    )(x2, g, rwh, rwl, rb, wg, wu, wd)


def _q_perm():
    idx = []
    for r in range(NSA_R):
        for g in range(NSA_G):
            h = g * NSA_R + r
            idx.extend(range(64 * h, 64 * h + 64))
    return np.asarray(idx)


def _prep_layer(w, l):
    p = {"layer": l}
    win = w["w_in"][l]
    qp = _q_perm()
    o_q = 1032 + 512
    cols = [win[:, 0:1024], win[:, 1032:1032 + 512], win[:, o_q:o_q + 512][:, qp],
            win[:, o_q + 512:o_q + 512 + 768], win[:, 1024:1032], win[:, o_q + 1280:o_q + 1304],
            jnp.zeros((D_MODEL, 128 - 32), F32)]
    p["w_in"] = jnp.concatenate(cols, axis=1).astype(BF16)
    p["w_st"] = win[:, 1024:1032].T.astype(BF16)
    p["norm_mix"] = w["norm_mix"][l][None]
    bias8 = jnp.concatenate([w["ml_i_bias"][l], w["ml_f_bias"][l]])
    p["ml_bcol"] = jnp.zeros((1, 128), F32).at[0, 0:8].set(bias8)
    p["ml_brow"] = bias8[:, None]
    p["conv_w"] = w["conv_w"][l]
    p["conv_b"] = w["conv_b"][l][None]
    bd = lambda m: jax.scipy.linalg.block_diag(*[m[i] for i in range(m.shape[0])])
    p["lru_wa"] = bd(w["lru_wa"][l]).astype(BF16)
    p["lru_wx"] = bd(w["lru_wx"][l]).astype(BF16)
    p["lru_ba"] = w["lru_ba"][l][None]
    p["lru_bx"] = w["lru_bx"][l][None]
    p["lru_lambda"] = w["lru_lambda"][l][None]
    cw = {}
    eye4 = jnp.eye(PAGE // CMP_BLK, dtype=F32)
    pw, ppos, pb1, pw2 = [], [], [], []
    for c, nm in ((0, "k"), (1, "v")):
        w1 = w["phi_w1"][l, c].reshape(CMP_BLK, HD, CMP_HID)
        z = jnp.zeros_like(w1)
        cw["bd" + nm] = jnp.concatenate([jnp.concatenate([w1, z], axis=2),
                                         jnp.concatenate([z, w1], axis=2)], axis=1).astype(BF16)
        cw["pos" + nm] = jnp.tile(w["cmp_pos"][l, c], (1, 2))
        cw["b1" + nm] = jnp.tile(w["phi_b1"][l, c], 2)[None]
        w2 = w["phi_w2"][l, c]
        z2 = jnp.zeros_like(w2)
        cw["w2" + nm] = jnp.concatenate([jnp.concatenate([w2, z2], axis=1),
                                         jnp.concatenate([z2, w2], axis=1)], axis=0).astype(BF16)
        w3 = w1.transpose(1, 0, 2).reshape(HD // 2, 2, CMP_BLK, CMP_HID)
        pw.append(jnp.einsum("pdrh,bc->pdbrch", w3, eye4).reshape(HD // 2, 256, 512).astype(BF16))
        pos3 = w["cmp_pos"][l, c].T.reshape(HD // 2, 2, 1, CMP_BLK)
        ppos.append(jnp.broadcast_to(pos3, (HD // 2, 2, 4, CMP_BLK)).reshape(HD // 2, 256))
        pb1.append(jnp.tile(w["phi_b1"][l, c], 4)[None])
        pw2.append(jnp.kron(eye4, w2).astype(BF16))
    cw["pw"] = jnp.stack(pw)
    cw["ppos"] = jnp.stack(ppos)
    cw["pb1"] = jnp.stack(pb1)
    cw["pw2"] = jnp.stack(pw2)
    p["cmp"] = cw
    perm = np.concatenate([np.arange(512), 512 + qp])
    p["mix_norm"] = w["mix_norm"][l][perm][None]
    p["w_out"] = w["w_out"][l][perm, :].astype(BF16)
    p["norm_xa"] = w["norm_xa"][l][None]
    p["norm_mem"] = w["norm_mem"][l][None]
    p["xa_wq"] = w["xa_wq"][l].astype(BF16)
    p["xa_wkv"] = w["xa_wkv"][l].astype(BF16)
    p["xa_wo"] = w["xa_wo"][l].astype(BF16)
    p["norm_ffn"] = w["norm_ffn"][l][None]
    rw = jnp.concatenate([w["router_ew"][l], w["router_gw"][l], jnp.zeros((D_MODEL, 128 - 20), F32)], axis=1)
    rwh = rw.astype(BF16)
    p["rwh"] = rwh
    p["rwl"] = (rw - rwh.astype(F32)).astype(BF16)
    p["rb"] = jnp.concatenate([w["router_eb"][l], w["router_gb"][l], jnp.zeros((128 - 20,), F32)])[None]
    return p


def _constants(T):
    c = {}
    head = np.arange(1024) // HD
    pool = np.zeros((1024, 128), np.float32)
    pool[np.arange(1024), head] = 1.0 / HD
    poolt = np.zeros((128, 1024), np.float32)
    poolt[head, np.arange(1024)] = 1.0
    c["pool"] = jnp.asarray(pool, BF16)
    c["poolt"] = jnp.asarray(poolt, BF16)
    gexp = np.zeros((3, 128, 512), np.float32)
    for cc in range(3):
        for g in range(NSA_G):
            for r in range(NSA_R):
                gexp[cc, 8 + cc * 8 + g * 4 + r, 128 * r + 64 * g:128 * r + 64 * g + 64] = 1.0
    c["gexp"] = jnp.asarray(gexp, BF16)
    kpos = np.arange(T)
    kaug = np.zeros((T, 128), np.float32)
    kaug[:, AUG_HI] = kpos // SLC_BLK
    kaug[:, AUG_LO] = kpos % SLC_BLK
    kaug[:, AUG_ONE] = 1.0
    kaug[kpos, AUG_SEL + kpos // SLC_BLK] = 1.0
    c["kaug"] = jnp.asarray(kaug, BF16)
    return c


def _even_odd(a):
    return jnp.concatenate([a[:, 0::2], a[:, 1::2]], axis=1)


def _pad_rows(a, rows):
    return jnp.pad(a, ((0, 0), (0, rows - a.shape[1]), (0, 0)))


def _dense_tail(x2, p, ew, B, T, mkv, tm, tq, tm_moe):
    n = x2.shape[0]
    q = _norm_mm(x2, p["norm_xa"], p["xa_wq"], tm, BF16).reshape(B, T, XA_INNER)
    if T < tq:
        o = _xattn(_pad_rows(q, tq), mkv, tq)[:, 0:T]
    else:
        o = _xattn(q, mkv, tq)
    x2 = _mm_res(o.reshape(n, XA_INNER), p["xa_wo"], x2, tm)
    return _moe(x2, p["norm_ffn"], p["rwh"], p["rwl"], p["rb"], ew[0], ew[1], ew[2], p["layer"], tm_moe)


def _layer_prompt(x2, p, c, ew, B, T, mem2):
    n = B * T
    ml, lru, q, kv, win, sm, smt, kvs_bf, win_bf = _in_proj(x2, p["norm_mix"], p["w_in"], p["w_st"], 256)
    smt_b = smt.reshape(8, B, T).transpose(1, 0, 2)
    ya, C, nn, mm = _mlstm(ml.reshape(B, T, 1024), sm.reshape(B, T, 128), smt_b, p["ml_bcol"], p["ml_brow"],
                           jnp.zeros((B, 4, 64, 64), F32), jnp.zeros((B, 4, 1, 64), F32),
                           jnp.zeros((B, 1, 128), F32), 128, 128)
    u_tm = lru.reshape(B, T, 512).transpose(1, 0, 2)
    yb_tm, tail_tm, h_last = _lru(u_tm, jnp.zeros((3, B, 256), F32), jnp.zeros((B, 256), F32), p["conv_w"],
                                  p["conv_b"], p["lru_wa"], p["lru_ba"], p["lru_wx"], p["lru_bx"],
                                  p["lru_lambda"], 256)
    yb = yb_tm.transpose(1, 0, 2).reshape(n, 256)
    n_pages = n // PAGE
    kc, vc = _compress_rows(jnp.arange(n_pages, dtype=I32), kv.reshape(n_pages, PAGE, 512), p["cmp"], 16)
    nc = T // CMP_BLK
    kc_eo = _even_odd(kc.reshape(B, nc, 128))
    vc_eo = _even_odd(vc.reshape(B, nc, 128))
    yc = _nsa_prompt(q.reshape(B, T, 512), sm.reshape(B, T, 128), kc_eo, vc_eo, kvs_bf.reshape(B, T, 256),
                     win_bf.reshape(B, T, 256), c["kaug"], c["gexp"])
    x2 = _mix_out(ya.reshape(n, 256), yb, yc.reshape(n, 512), x2, p["mix_norm"], p["w_out"], c["pool"],
                  c["poolt"], 256)
    mkv = _norm_mm(mem2, p["norm_mem"], p["xa_wkv"], 256, F32)
    x2 = _dense_tail(x2, p, ew, B, T, mkv.reshape(B, N_MEM, 1024), 256, 512, 1024)
    st = (kv.reshape(B, T, 4, NSA_G, HD), win.reshape(B, T, 2, NSA_G, HD)[:, T - WINDOW:],
          C, nn.reshape(B, 4, 64), mm[:, 0, 0:4], h_last, tail_tm.transpose(1, 0, 2))
    return x2, st, mkv.reshape(B, N_MEM, 2, XA_H, XA_HD)


def _layer_sample(x2, p, c, ew, B, T, pool_t, pt_flat, win_buf, win_t, C0, n0, m0, conv0, h0, mkv, past):
    n = B * T
    _pad_rows8 = lambda a: _pad_rows(a, 8)
    ml, lru, q, kv, win, sm, smt, _, _ = _in_proj(x2, p["norm_mix"], p["w_in"], p["w_st"], n)
    sm8 = _pad_rows8(sm.reshape(B, T, 128))
    smt_b = jnp.pad(smt.reshape(8, B, T).transpose(1, 0, 2), ((0, 0), (0, 0), (0, 8 - T)))
    m0p = jnp.pad(m0[:, None, :], ((0, 0), (0, 0), (0, 128 - ML_H)))
    ya, C, nn, mm = _mlstm(_pad_rows8(ml.reshape(B, T, 1024)), sm8, smt_b, p["ml_bcol"], p["ml_brow"],
                           C0, n0[:, :, None, :], m0p, 8, T)
    ya = ya[:, 0:T]
    u_tm = lru.reshape(B, T, 512).transpose(1, 0, 2)
    yb_tm, tail_tm, h_last = _lru(u_tm, conv0.transpose(1, 0, 2), h0, p["conv_w"], p["conv_b"], p["lru_wa"],
                                  p["lru_ba"], p["lru_wx"], p["lru_bx"], p["lru_lambda"], T)
    yb = yb_tm.transpose(1, 0, 2).reshape(n, 256)
    n_pages = past // PAGE
    kvc = _compress_pages(pt_flat, pool_t, p["cmp"], CMP_STEP_PAGES)
    wb = win_buf.shape[1]
    q8 = _pad_rows8(q.reshape(B, T, 512).astype(F32))
    win3 = win.reshape(B, T, 256)
    oc, ow, sc = _nsa_sample_sel(q8, kvc, win_t, p["layer"], _pad_rows8(win3), past)
    idx_flat = _topk(sc.reshape(B * NSA_G * 8, 3 * LANES), n_pages)[:, 0:N_SEL].reshape(-1)
    kv3 = kv.reshape(B, T, 512)
    yc8 = _nsa_sample_slc(idx_flat, pt_flat, pool_t, q8, _pad_rows8(kv3[:, :, 256:512]), sm8, oc, ow,
                          c["gexp"], T, past)
    yc = yc8[:, 0:T].reshape(n, 512)
    x2 = _mix_out(ya.reshape(n, 256), yb, yc, x2, p["mix_norm"], p["w_out"], c["pool"], c["poolt"], n)
    x2 = _dense_tail(x2, p, ew, B, T, mkv.reshape(B, N_MEM, 1024), n, 16, n)
    win_new = jnp.concatenate([win_buf.reshape(B, wb, 256), win3], axis=1)[:, T:]
    st = (kv.reshape(B, T, 4, NSA_G, HD), win_new.reshape(B, wb, 2, NSA_G, HD), C, nn.reshape(B, 4, 64),
          mm[:, 0, 0:4], h_last, tail_tm.transpose(1, 0, 2))
    return x2, st


def kernel(x_prompt, x_sample, cache_nsa_kv, state_nsa_win, state_mlstm_C, state_mlstm_n, state_mlstm_m, state_rglru_h, state_rglru_conv, cache_mem_kv, page_table, mem_prompt, norm_mix, w_in, ml_i_bias, ml_f_bias, conv_w, conv_b, lru_wa, lru_ba, lru_wx, lru_bx, lru_lambda, phi_w1, phi_b1, phi_w2, cmp_pos, mix_norm, w_out, norm_xa, norm_mem, xa_wq, xa_wkv, xa_wo, norm_ffn, router_gw, router_gb, router_ew, router_eb, exp_w_gate, exp_w_up, exp_w_down, final_norm):
    w = dict(norm_mix=norm_mix, w_in=w_in, ml_i_bias=ml_i_bias, ml_f_bias=ml_f_bias, conv_w=conv_w,
             conv_b=conv_b, lru_wa=lru_wa, lru_ba=lru_ba, lru_wx=lru_wx, lru_bx=lru_bx, lru_lambda=lru_lambda,
             phi_w1=phi_w1, phi_b1=phi_b1, phi_w2=phi_w2, cmp_pos=cmp_pos, mix_norm=mix_norm, w_out=w_out,
             norm_xa=norm_xa, norm_mem=norm_mem, xa_wq=xa_wq, xa_wkv=xa_wkv, xa_wo=xa_wo, norm_ffn=norm_ffn,
             router_gw=router_gw, router_gb=router_gb, router_ew=router_ew, router_eb=router_eb)
    depth = w_in.shape[0]
    B, T, _ = x_prompt.shape
    BS, TS, _ = x_sample.shape
    n_pages = page_table.shape[1]
    past = n_pages * PAGE
    n_phys = cache_nsa_kv.shape[1]
    consts = _constants(T)
    layers = [_prep_layer(w, l) for l in range(depth)]
    ew = (exp_w_gate.astype(BF16), exp_w_up.astype(BF16), exp_w_down.astype(BF16))
    fin = final_norm[None]

    x2 = x_prompt.reshape(B * T, D_MODEL)
    mem2 = mem_prompt.reshape(B * N_MEM, D_MODEL)
    outs_p, mem_p = [], []
    for l in range(depth):
        x2, st, mkv = _layer_prompt(x2, layers[l], consts, ew, B, T, mem2)
        outs_p.append(st)
        mem_p.append(mkv)
    y_prompt = _final_norm(x2, fin, 256).reshape(B, T, D_MODEL)

    pool_t = jnp.transpose(cache_nsa_kv, (0, 1, 3, 4, 5, 2)).reshape(depth * n_phys * 4 * NSA_G, HD, PAGE)
    win_t = jnp.transpose(state_nsa_win, (0, 1, 3, 4, 5, 2))
    xs = x_sample.reshape(BS * TS, D_MODEL)
    outs_s = []
    for l in range(depth):
        pt_flat = page_table.reshape(-1) + l * n_phys
        xs, st = _layer_sample(xs, layers[l], consts, ew, BS, TS, pool_t, pt_flat, state_nsa_win[l], win_t,
                               state_mlstm_C[l], state_mlstm_n[l], state_mlstm_m[l], state_rglru_conv[l],
                               state_rglru_h[l], cache_mem_kv[l], past)
        outs_s.append(st)
    y_sample = _final_norm(xs, fin, BS * TS).reshape(BS, TS, D_MODEL)

    sp = [jnp.stack(a) for a in zip(*outs_p)]
    ss = [jnp.stack(a) for a in zip(*outs_s)]
    return (y_prompt, y_sample, sp[0], ss[0], sp[1], ss[1], sp[2], ss[2], sp[3], ss[3], sp[4], ss[4],
            sp[5], ss[5], sp[6], ss[6], jnp.stack(mem_p))
```

```python
import functools
import math

import jax
import jax.numpy as jnp
import numpy as np
from jax import lax
from jax.experimental import pallas as pl
from jax.experimental.pallas import tpu as pltpu

F32 = jnp.float32
BF16 = jnp.bfloat16
I32 = jnp.int32

D_MODEL = 1024
DEPTH = 2
PAGE = 128
HD = 64
ML_W = 256
ML_H = 4
LRU_W = 256
LRU_C = 8.0
CONV_W = 4
NSA_W = 512
NSA_H = 8
NSA_G = 2
NSA_R = 4
CMP_BLK = 32
CMP_HID = 128
SLC_BLK = 64
N_SEL = 16
WINDOW = 512
FORCE_BONUS = 100.0
XA_H = 4
XA_HD = 128
XA_INNER = 512
N_MEM = 256
N_GROUPS = 4
EXP_PER_GROUP = 4
N_EXP = 16
EXP_FF = 256
EPS = 1e-6
NEG = -1e30
SLOPES = tuple(2.0 ** (-(h + 1)) for h in range(NSA_H))

LANES = 128
SUBLANES = 8
VMEM_LIMIT = 56 * 1024 * 1024

IN_ML = (0, 1024)
IN_LRU = (1024, 1536)
IN_Q = (1536, 2048)
IN_KV = (2048, 2560)
IN_WIN = (2560, 2816)
IN_SMALL = (2816, 2944)
IN_COLS_P = 2944

AUG_HI = 0
AUG_LO = 1
AUG_ONE = 2
AUG_SEL = 8
KEY_CHUNK = 256
ML_BATCH_ROWS = 1
ML_CHUNK = 256
CMP_STEP_PAGES = 128


def _cparams(sem):
    return pltpu.CompilerParams(dimension_semantics=sem, vmem_limit_bytes=VMEM_LIMIT)


def _dot(a, b):
    return jnp.dot(a, b, preferred_element_type=F32)


def _dot_nt(a, b):
    return lax.dot_general(a, b, (((1,), (1,)), ((), ())), preferred_element_type=F32)


def _dot_tn(a, b):
    return lax.dot_general(a, b, (((0,), (0,)), ((), ())), preferred_element_type=F32)


def _split2(x):
    hi = x.astype(BF16)
    lo = (x - hi.astype(F32)).astype(BF16)
    return hi, lo


def _dot_split(x, w_bf):
    hi, lo = _split2(x)
    return _dot(hi, w_bf) + _dot(lo, w_bf)


def _sigmoid(x):
    return 1.0 / (1.0 + jnp.exp(-x))


def _gelu(x):
    return 0.5 * x * (1.0 + jnp.tanh(0.7978845608028654 * (x + 0.044715 * (x * x * x))))


def _softplus(x):
    return jnp.maximum(x, 0.0) + jnp.log1p(jnp.exp(-jnp.abs(x)))


def _log_sigmoid(x):
    return -_softplus(-x)


def _rms(x, g):
    return x * lax.rsqrt(jnp.mean(x * x, axis=-1, keepdims=True) + EPS) * g


def _masked_softmax(s, mask):
    sm = jnp.where(mask, s, NEG)
    e = jnp.exp(sm - jnp.max(sm, axis=-1, keepdims=True))
    p = e / jnp.sum(e, axis=-1, keepdims=True)
    return jnp.where(mask, p, 0.0)


def _in_proj_body(x_ref, g_ref, w_ref, wst_ref, ml_ref, lru_ref, q_ref, kv_ref, win_ref, sm_ref,
                  smt_ref, kvsb_ref, winb_ref):
    hb = _rms(x_ref[...], g_ref[...]).astype(BF16)

    def mm(rng):
        return _dot(hb, w_ref[:, rng[0]:rng[1]])

    ml_ref[...] = mm(IN_ML)
    lru_ref[...] = mm(IN_LRU)
    q_ref[...] = mm(IN_Q).astype(BF16)
    kv = mm(IN_KV)
    kv_ref[...] = kv
    kvsb_ref[...] = kv[:, 256:512].astype(BF16)
    win = mm(IN_WIN)
    win_ref[...] = win
    winb_ref[...] = win.astype(BF16)
    sm_ref[...] = mm(IN_SMALL)
    smt_ref[...] = _dot_nt(wst_ref[...], hb)


def _in_proj(x2, g, w_p, w_st, tm):
    n = x2.shape[0]
    row = lambda w: pl.BlockSpec((tm, w), lambda i: (i, 0))
    full = lambda a: pl.BlockSpec(a.shape, lambda i: (0,) * a.ndim)
    shapes = [(1024, F32), (512, F32), (512, BF16), (512, F32), (256, F32), (128, F32)]
    out_shape = [jax.ShapeDtypeStruct((n, w), dt) for w, dt in shapes]
    out_specs = [row(w) for w, _ in shapes]
    out_shape += [jax.ShapeDtypeStruct((8, n), F32), jax.ShapeDtypeStruct((n, 256), BF16),
                  jax.ShapeDtypeStruct((n, 256), BF16)]
    out_specs += [pl.BlockSpec((8, tm), lambda i: (0, i)), row(256), row(256)]
    return pl.pallas_call(
        _in_proj_body, grid=(n // tm,),
        in_specs=[row(D_MODEL), full(g), full(w_p), full(w_st)],
        out_specs=out_specs, out_shape=out_shape,
        compiler_params=_cparams(("parallel",)), name="in_proj",
    )(x2, g, w_p, w_st)


def _norm_mm_body(x_ref, g_ref, w_ref, o_ref):
    o_ref[...] = _dot(_rms(x_ref[...], g_ref[...]).astype(BF16), w_ref[...]).astype(o_ref.dtype)


def _norm_mm(x2, g, w_bf, tm, out_dtype):
    n, k = x2.shape
    m = w_bf.shape[1]
    return pl.pallas_call(
        _norm_mm_body, grid=(n // tm,),
        in_specs=[pl.BlockSpec((tm, k), lambda i: (i, 0)), pl.BlockSpec((1, k), lambda i: (0, 0)),
                  pl.BlockSpec((k, m), lambda i: (0, 0))],
        out_specs=pl.BlockSpec((tm, m), lambda i: (i, 0)),
        out_shape=jax.ShapeDtypeStruct((n, m), out_dtype),
        compiler_params=_cparams(("parallel",)), name="norm_mm",
    )(x2, g, w_bf)


def _mm_res_body(a_ref, w_ref, x_ref, o_ref):
    o_ref[...] = x_ref[...] + _dot(a_ref[...], w_ref[...])


def _mm_res(a_bf, w_bf, x2, tm):
    n, k = a_bf.shape
    m = w_bf.shape[1]
    return pl.pallas_call(
        _mm_res_body, grid=(n // tm,),
        in_specs=[pl.BlockSpec((tm, k), lambda i: (i, 0)), pl.BlockSpec((k, m), lambda i: (0, 0)),
                  pl.BlockSpec((tm, m), lambda i: (i, 0))],
        out_specs=pl.BlockSpec((tm, m), lambda i: (i, 0)),
        out_shape=jax.ShapeDtypeStruct((n, m), F32),
        compiler_params=_cparams(("parallel",)), name="mm_res",
    )(a_bf, w_bf, x2)


def _final_norm_body(x_ref, g_ref, o_ref):
    o_ref[...] = _rms(x_ref[...], g_ref[...])


def _final_norm(x2, g, tm):
    n, k = x2.shape
    return pl.pallas_call(
        _final_norm_body, grid=(n // tm,),
        in_specs=[pl.BlockSpec((tm, k), lambda i: (i, 0)), pl.BlockSpec((1, k), lambda i: (0, 0))],
        out_specs=pl.BlockSpec((tm, k), lambda i: (i, 0)),
        out_shape=jax.ShapeDtypeStruct((n, k), F32),
        compiler_params=_cparams(("parallel",)), name="final_norm",
    )(x2, g)


def _mlstm_body(u_ref, sm_ref, smt_ref, bcol_ref, brow_ref, c0_ref, n0_ref, m0_ref,
                y_ref, c_ref, n_ref, m_ref, c_s, n_s, m_s, *, L, t_real, nbb):
    ci = pl.program_id(1)

    @pl.when(ci == 0)
    def _():
        c_s[...] = c0_ref[...]
        n_s[...] = n0_ref[...]
        m_s[...] = m0_ref[...]

    row = lax.broadcasted_iota(I32, (L, L), 0)
    col = lax.broadcasted_iota(I32, (L, L), 1)
    causal = col <= row
    lane = lax.broadcasted_iota(I32, (1, LANES), 1)
    real_col = lax.broadcasted_iota(I32, (L, 1), 0) < t_real
    real_row = lax.broadcasted_iota(I32, (1, L), 1) < t_real
    for bb in range(nbb):
        sm = sm_ref[bb] + bcol_ref[...]
        smt = smt_ref[bb] + brow_ref[...]
        m_all = m_s[bb]
        m_next = m_all
        for h in range(ML_H):
            q = u_ref[bb, :, 64 * h:64 * h + 64]
            k = u_ref[bb, :, 256 + 64 * h:256 + 64 * h + 64] * (HD ** -0.5)
            v = u_ref[bb, :, 512 + 64 * h:512 + 64 * h + 64]
            o = u_ref[bb, :, 768 + 64 * h:768 + 64 * h + 64]
            ig_col = jnp.where(real_col, sm[:, h:h + 1], NEG)
            lf_col = jnp.where(real_col, _log_sigmoid(sm[:, 4 + h:5 + h]), 0.0)
            ig_row = jnp.where(real_row, smt[h:h + 1, :], NEG)
            lf_row = jnp.where(real_row, _log_sigmoid(smt[4 + h:5 + h, :]), 0.0)
            b_col = jnp.sum(jnp.where(causal, lf_row, 0.0), axis=1, keepdims=True)
            b_row = jnp.sum(jnp.where(row <= col, lf_col, 0.0), axis=0, keepdims=True)
            m_prev = m_all[:, h:h + 1]
            log_d = jnp.where(causal, b_col - b_row + ig_row, NEG)
            inter = b_col + m_prev
            m_t = jnp.maximum(inter, jnp.max(log_d, axis=1, keepdims=True))
            w_carry = jnp.exp(inter - m_t)
            qb = q.astype(BF16)
            vb = v.astype(BF16)
            s = _dot_nt(qb, k.astype(BF16)) * jnp.exp(log_d - m_t)
            c_old = c_s[bb, h]
            n_old = n_s[bb, h]
            num = _dot(s.astype(BF16), vb) + w_carry * _dot_nt(qb, c_old.astype(BF16))
            den = jnp.sum(s, axis=1, keepdims=True) + w_carry * jnp.sum(q * n_old, axis=1, keepdims=True)
            hh = num / jnp.maximum(jnp.abs(den), jnp.exp(-m_t))
            y_ref[bb, :, 64 * h:64 * h + 64] = _sigmoid(o) * hh
            b_end = b_col[L - 1:L, :]
            log_w_row = b_end - b_row + ig_row
            m_new = jnp.maximum(b_end + m_prev, jnp.max(log_w_row, axis=1, keepdims=True))
            w_col = jnp.exp(b_end - b_col + ig_col - m_new)
            decay = jnp.exp(b_end + m_prev - m_new)
            c_s[bb, h] = decay * c_old + _dot_tn((v * w_col).astype(BF16), k.astype(BF16))
            n_s[bb, h] = decay * n_old + jnp.sum(k * w_col, axis=0, keepdims=True)
            m_next = jnp.where(lane == h, m_new, m_next)
        m_s[bb] = m_next

    @pl.when(ci == pl.num_programs(1) - 1)
    def _():
        c_ref[...] = c_s[...]
        n_ref[...] = n_s[...]
        m_ref[...] = m_s[...]


def _mlstm(u_ml, u_sm, u_smt, bcol, brow, c0, n0, m0, L, t_real):
    b, t, _ = u_ml.shape
    nc = t // L
    assert t_real == L or nc == 1
    nbb = ML_BATCH_ROWS
    assert b % nbb == 0
    return pl.pallas_call(
        functools.partial(_mlstm_body, L=L, t_real=t_real, nbb=nbb), grid=(b // nbb, nc),
        in_specs=[pl.BlockSpec((nbb, L, 1024), lambda i, c: (i, c, 0)),
                  pl.BlockSpec((nbb, L, 128), lambda i, c: (i, c, 0)),
                  pl.BlockSpec((nbb, 8, L), lambda i, c: (i, 0, c)),
                  pl.BlockSpec((1, 128), lambda i, c: (0, 0)),
                  pl.BlockSpec((8, 1), lambda i, c: (0, 0)),
                  pl.BlockSpec((nbb, 4, 64, 64), lambda i, c: (i, 0, 0, 0)),
                  pl.BlockSpec((nbb, 4, 1, 64), lambda i, c: (i, 0, 0, 0)),
                  pl.BlockSpec((nbb, 1, 128), lambda i, c: (i, 0, 0))],
        out_specs=[pl.BlockSpec((nbb, L, 256), lambda i, c: (i, c, 0)),
                   pl.BlockSpec((nbb, 4, 64, 64), lambda i, c: (i, 0, 0, 0)),
                   pl.BlockSpec((nbb, 4, 1, 64), lambda i, c: (i, 0, 0, 0)),
                   pl.BlockSpec((nbb, 1, 128), lambda i, c: (i, 0, 0))],
        out_shape=[jax.ShapeDtypeStruct((b, t, 256), F32), jax.ShapeDtypeStruct((b, 4, 64, 64), F32),
                   jax.ShapeDtypeStruct((b, 4, 1, 64), F32), jax.ShapeDtypeStruct((b, 1, 128), F32)],
        scratch_shapes=[pltpu.VMEM((nbb, 4, 64, 64), F32), pltpu.VMEM((nbb, 4, 1, 64), F32),
                        pltpu.VMEM((nbb, 1, 128), F32)],
        compiler_params=_cparams(("parallel", "arbitrary")), name="mlstm",
    )(u_ml, u_sm, u_smt, bcol, brow, c0, n0, m0)


def _lru_body(u_ref, cb_ref, h0_ref, cw_ref, cbias_ref, wa_ref, ba_ref, wx_ref, bx_ref, lam_ref,
              y_ref, tail_ref, hl_ref, tail_s, h_s, a_s, hs_s, *, tt, nb):
    i = pl.program_id(0)

    @pl.when(i == 0)
    def _():
        tail_s[...] = cb_ref[...]
        h_s[...] = h0_ref[...]

    x = u_ref[:, :, 0:LRU_W]
    g = u_ref[:, :, LRU_W:2 * LRU_W]
    xe = jnp.concatenate([tail_s[...], x], axis=0)
    xc = cbias_ref[...] + xe[0:tt] * cw_ref[0:1, :]
    for j in range(1, CONV_W):
        xc = xc + xe[j:j + tt] * cw_ref[j:j + 1, :]
    xc2 = xc.reshape(tt * nb, LRU_W)
    xcb = xc2.astype(BF16)
    r = _sigmoid(_dot(xcb, wa_ref[...]) + ba_ref[...])
    ig = _sigmoid(_dot(xcb, wx_ref[...]) + bx_ref[...])
    log_a = -LRU_C * r * _softplus(-lam_ref[...])
    a = jnp.exp(log_a)
    mult = jnp.sqrt(jnp.tanh(-log_a) * (a * a + 1.0))
    a_s[...] = a.reshape(tt, nb, LRU_W)
    hs_s[...] = (mult * (ig * xc2)).reshape(tt, nb, LRU_W)

    def step(t, h):
        h = a_s[t] * h + hs_s[t]
        hs_s[t] = h
        return h

    h_last = lax.fori_loop(0, tt, step, h_s[...], unroll=min(8, tt))
    h_s[...] = h_last
    y_ref[...] = hs_s[...] * _gelu(g)
    tail_s[...] = xe[tt:tt + CONV_W - 1]

    @pl.when(i == pl.num_programs(0) - 1)
    def _():
        tail_ref[...] = tail_s[...]
        hl_ref[...] = h_s[...]


def _lru(u_tm, cb_tm, h0, cw, cbias, wa_bd, ba, wx_bd, bx, lam, tt):
    t, nb, _ = u_tm.shape
    full = lambda a: pl.BlockSpec(a.shape, lambda i: (0,) * a.ndim)
    return pl.pallas_call(
        functools.partial(_lru_body, tt=tt, nb=nb), grid=(t // tt,),
        in_specs=[pl.BlockSpec((tt, nb, 512), lambda i: (i, 0, 0)), full(cb_tm), full(h0), full(cw),
                  full(cbias), full(wa_bd), full(ba), full(wx_bd), full(bx), full(lam)],
        out_specs=[pl.BlockSpec((tt, nb, 256), lambda i: (i, 0, 0)),
                   pl.BlockSpec((3, nb, 256), lambda i: (0, 0, 0)),
                   pl.BlockSpec((nb, 256), lambda i: (0, 0))],
        out_shape=[jax.ShapeDtypeStruct((t, nb, 256), F32), jax.ShapeDtypeStruct((3, nb, 256), F32),
                   jax.ShapeDtypeStruct((nb, 256), F32)],
        scratch_shapes=[pltpu.VMEM((3, nb, 256), F32), pltpu.VMEM((nb, 256), F32),
                        pltpu.VMEM((tt, nb, 256), F32), pltpu.VMEM((tt, nb, 256), F32)],
        compiler_params=_cparams(("arbitrary",)), name="rglru",
    )(u_tm, cb_tm, h0, cw, cbias, wa_bd, ba, wx_bd, bx, lam)


def _compress_rows_body(pt_ref, pool_ref, bdk_ref, bdv_ref, posk_ref, posv_ref, b1k_ref, b1v_ref,
                        w2k_ref, w2v_ref, kc_ref, vc_ref, buf, sem, *, P):
    s = pl.program_id(0)
    ns = pl.num_programs(0)
    nblk = P * (PAGE // CMP_BLK)

    def page_copy(step, slot, p, c):
        return pltpu.make_async_copy(
            pool_ref.at[pt_ref[step * P + p], :, pl.ds(LANES * c, LANES)],
            buf.at[slot, c, pl.ds(p * PAGE, PAGE), :], sem.at[slot])

    def start_all(step, slot):
        def body(p, carry):
            page_copy(step, slot, p, 0).start()
            page_copy(step, slot, p, 1).start()
            return carry
        lax.fori_loop(0, P, body, 0)

    def wait_all(step, slot):
        def body(p, carry):
            page_copy(step, slot, p, 0).wait()
            page_copy(step, slot, p, 1).wait()
            return carry
        lax.fori_loop(0, P, body, 0)

    @pl.when(s == 0)
    def _():
        start_all(0, 0)

    slot = lax.rem(s, 2)

    @pl.when(s + 1 < ns)
    def _():
        start_all(s + 1, 1 - slot)

    wait_all(s, slot)

    acc_k = jnp.zeros((nblk, 256), F32)
    acc_v = jnp.zeros((nblk, 256), F32)
    for r in range(CMP_BLK):
        rows = pl.ds(r, nblk, stride=CMP_BLK)
        xk = (buf[slot, 0, rows, :] + posk_ref[r:r + 1, :]).astype(BF16)
        xv = (buf[slot, 1, rows, :] + posv_ref[r:r + 1, :]).astype(BF16)
        acc_k = acc_k + _dot(xk, bdk_ref[r])
        acc_v = acc_v + _dot(xv, bdv_ref[r])
    kc_ref[...] = _dot(_gelu(acc_k + b1k_ref[...]).astype(BF16), w2k_ref[...])
    vc_ref[...] = _dot(_gelu(acc_v + b1v_ref[...]).astype(BF16), w2v_ref[...])


def _compress_rows(pt_flat, pool3, cw, P):
    n_pages = pt_flat.shape[0]
    nblk = P * (PAGE // CMP_BLK)
    full = lambda a: pl.BlockSpec(a.shape, lambda i, pt: (0,) * a.ndim)
    ws = (cw["bdk"], cw["bdv"], cw["posk"], cw["posv"], cw["b1k"], cw["b1v"], cw["w2k"], cw["w2v"])
    grid_spec = pltpu.PrefetchScalarGridSpec(
        num_scalar_prefetch=1, grid=(n_pages // P,),
        in_specs=[pl.BlockSpec(memory_space=pl.ANY)] + [full(a) for a in ws],
        out_specs=[pl.BlockSpec((nblk, 128), lambda i, pt: (i, 0)),
                   pl.BlockSpec((nblk, 128), lambda i, pt: (i, 0))],
        scratch_shapes=[pltpu.VMEM((2, 2, P * PAGE, LANES), F32), pltpu.SemaphoreType.DMA((2,))])
    return pl.pallas_call(
        functools.partial(_compress_rows_body, P=P), grid_spec=grid_spec,
        out_shape=[jax.ShapeDtypeStruct((n_pages * 4, 128), F32)] * 2,
        compiler_params=_cparams(("arbitrary",)), name="compress_rows",
    )(pt_flat, pool3, *ws)


def _compress_pages_body(pt_ref, pool_ref, w_ref, pos_ref, b1_ref, w2_ref, o_ref, buf, sem, *, P, nstep):
    c = pl.program_id(0)
    s = pl.program_id(1)
    lin = c * nstep + s
    PG = NSA_G * P

    def tile_copy(cc, step, slot, p):
        return pltpu.make_async_copy(
            pool_ref.at[pl.ds((pt_ref[step * P + p] * 4 + cc) * NSA_G, NSA_G)],
            buf.at[slot, :, :, p, :], sem.at[slot])

    def start_all(cc, step, slot):
        def body(p, carry):
            tile_copy(cc, step, slot, p).start()
            return carry
        lax.fori_loop(0, P, body, 0, unroll=4)

    def wait_all(slot):
        pltpu.make_async_copy(buf.at[slot], buf.at[slot], sem.at[slot]).wait()

    @pl.when(lin == 0)
    def _():
        start_all(0, 0, 0)

    slot = lax.rem(lin, 2)
    nxt = lin + 1

    @pl.when(nxt < 2 * nstep)
    def _():
        start_all(lax.div(nxt, nstep), lax.rem(nxt, nstep), 1 - slot)

    wait_all(slot)

    acc = jnp.zeros((PG, 4 * CMP_HID), F32)
    for dp in range(HD // 2):
        chan = [jnp.concatenate([buf[slot, 0, d], buf[slot, 1, d]], axis=0) for d in (2 * dp, 2 * dp + 1)]
        lhs = jnp.concatenate(chan, axis=1) + pos_ref[dp:dp + 1, :]
        acc = acc + _dot(lhs.astype(BF16), w_ref[dp])
    h = _gelu(acc + b1_ref[...])
    out = _dot(h.astype(BF16), w2_ref[...])
    for blk in range(PAGE // CMP_BLK):
        cols = slice(HD * blk, HD * blk + HD)
        o_ref[blk] = jnp.concatenate([out[0:P, cols], out[P:2 * P, cols]], axis=1)


def _compress_pages(pt_flat, pool_t, cw, P):
    n_pages = pt_flat.shape[0]
    nstep = n_pages // P
    PG = NSA_G * P
    grid_spec = pltpu.PrefetchScalarGridSpec(
        num_scalar_prefetch=1, grid=(2, nstep),
        in_specs=[pl.BlockSpec(memory_space=pl.ANY),
                  pl.BlockSpec((None, HD // 2, 256, 512), lambda c, s, pt: (c, 0, 0, 0)),
                  pl.BlockSpec((None, HD // 2, 256), lambda c, s, pt: (c, 0, 0)),
                  pl.BlockSpec((None, 1, 512), lambda c, s, pt: (c, 0, 0)),
                  pl.BlockSpec((None, 512, 256), lambda c, s, pt: (c, 0, 0))],
        out_specs=pl.BlockSpec((None, PAGE // CMP_BLK, P, LANES), lambda c, s, pt: (c, 0, s, 0)),
        scratch_shapes=[pltpu.VMEM((2, NSA_G, HD, P, LANES), F32), pltpu.SemaphoreType.DMA((2,))])
    return pl.pallas_call(
        functools.partial(_compress_pages_body, P=P, nstep=nstep), grid_spec=grid_spec,
        out_shape=jax.ShapeDtypeStruct((2, PAGE // CMP_BLK, n_pages, LANES), F32),
        compiler_params=_cparams(("arbitrary", "arbitrary")), name="compress_pages",
    )(pt_flat, pool_t, cw["pw"], cw["ppos"], cw["pb1"], cw["pw2"])


def _half_mask(g):
    lane = lax.broadcasted_iota(I32, (1, LANES), 1)
    return (lane >= 64 * g) & (lane < 64 * g + 64)


def _gate_expand(gates, gexp_ref):
    hi, lo = _split2(gates)
    return [_dot(hi, gexp_ref[c]) + _dot(lo, gexp_ref[c]) for c in range(3)]


def _nsa_prompt_body(q_ref, sm_ref, kc_ref, vc_ref, kvs_ref, kaug_ref, win_ref, gexp_ref, o_ref,
                     s_scr, mx_scr, acc_scr, *, T):
    QT = 128
    M = NSA_H * QT
    CK = KEY_CHUNK
    q0 = pl.program_id(1) * QT
    lane = lax.broadcasted_iota(I32, (1, LANES), 1)
    left = lane < 64
    rowm = lax.broadcasted_iota(I32, (M, 1), 0)
    qposm = q0 + (rowm & (QT - 1))
    slopem = jnp.zeros((M, 1), F32)
    for h in range(NSA_H):
        slopem = jnp.where((rowm >= QT * h) & (rowm < QT * (h + 1)), SLOPES[h], slopem)

    nce = T // CMP_BLK // 2
    blk = jnp.where(lane < nce, 2 * lane, 2 * (lane - nce) + 1)
    c_end = jnp.where(lane < 2 * nce, blk * CMP_BLK + (CMP_BLK - 1), 1 << 30)
    dist_c = qposm - c_end
    mask_c = dist_c >= 0
    zpad = jnp.zeros((LANES - 2 * nce, LANES), F32)
    kc = jnp.concatenate([kc_ref[...], zpad], axis=0).astype(BF16)
    vc = jnp.concatenate([vc_ref[...], zpad], axis=0).astype(BF16)

    qs = [jnp.where(_half_mask(g), q_ref[:, 128 * r:128 * r + 128], 0)
          for g in range(NSA_G) for r in range(NSA_R)]
    qm = jnp.concatenate(qs, axis=0)

    s = _dot_nt(qm, kc) * (HD ** -0.5) - slopem * dist_c.astype(F32)
    p = _masked_softmax(s, mask_c)
    o_cmp = _dot(p.astype(BF16), vc)

    ns = T // SLC_BLK
    jrow = lax.broadcasted_iota(I32, (ns, 1), 0)
    cur = (q0 + lane) // SLC_BLK
    in_sel = (lane >= AUG_SEL) & (lane < AUG_SEL + ns)
    valid = jrow <= cur
    forced = (jrow == 0) | (jrow == cur) | (jrow == cur - 1)
    sel_bias = []
    for g in range(NSA_G):
        b0 = g * NSA_R * QT
        imp = p[b0:b0 + QT] + p[b0 + QT:b0 + 2 * QT] + p[b0 + 2 * QT:b0 + 3 * QT] + p[b0 + 3 * QT:b0 + 4 * QT]
        imp_t = imp.T
        pooled = imp_t[0:nce] + imp_t[nce:2 * nce]
        score = jnp.where(valid, pooled + FORCE_BONUS * forced.astype(F32), NEG)
        cnt = jnp.zeros((ns, LANES), F32)
        for i2 in range(ns):
            si = score[i2:i2 + 1, :]
            beats = (si > score) | ((si == score) & (i2 < jrow))
            cnt = cnt + beats.astype(F32)
        sel_t = ((cnt < N_SEL) & (score > 0.5 * NEG)).astype(F32)
        selp = jnp.concatenate([jnp.zeros((AUG_SEL, LANES), F32), sel_t,
                                jnp.zeros((LANES - AUG_SEL - ns, LANES), F32)], axis=0).T
        sel_bias.append(jnp.where(in_sel, (selp - 1.0) * (-NEG), 0.0))

    shift = -(q0 + QT - 1).astype(F32)

    def aug(h, base):
        sl = SLOPES[h]
        a = jnp.where(lane == AUG_HI, sl * SLC_BLK, jnp.where(lane == AUG_LO, sl,
                      jnp.where(lane == AUG_ONE, sl * shift, base)))
        return a.astype(BF16)

    qsc = [(qs[h].astype(F32) * (HD ** -0.5)).astype(BF16) for h in range(NSA_H)]
    zero_t = jnp.zeros((QT, LANES), F32)
    qsel = jnp.concatenate([jnp.concatenate([qsc[h], aug(h, sel_bias[h // NSA_R])], axis=1)
                            for h in range(NSA_H)], axis=0)
    qwin = jnp.concatenate([jnp.concatenate([qsc[h], aug(h, zero_t)], axis=1) for h in range(NSA_H)], axis=0)
    ones_k = jnp.ones((CK, LANES), BF16)

    def key_chunk(c):
        r0 = pl.multiple_of(c * CK, CK)
        return jnp.concatenate([kvs_ref[pl.ds(r0, CK), 0:128], kaug_ref[pl.ds(r0, CK), :]], axis=1)

    def fold(sc):
        return jnp.maximum(sc[:, 0:128], sc[:, 128:256])

    nk = q0 // CK + 1
    mx_scr[...] = jnp.full((M, LANES), NEG, F32)

    def body1(c, carry):
        sc = _dot_nt(qsel, key_chunk(c))
        s_scr[c] = sc
        mx_scr[...] = jnp.maximum(mx_scr[...], fold(sc))
        return carry

    lax.fori_loop(0, nk - 1, body1, 0)
    cl = nk - 1
    kpos_l = cl * CK + lax.broadcasted_iota(I32, (1, CK), 1)
    sc = jnp.where(kpos_l <= qposm, _dot_nt(qsel, key_chunk(cl)), NEG)
    s_scr[cl] = sc
    m = jnp.max(jnp.maximum(mx_scr[...], fold(sc)), axis=1, keepdims=True)

    acc_scr[...] = jnp.zeros((M, 2 * LANES), F32)

    def body2(c, carry):
        e = jnp.exp(s_scr[c] - m).astype(BF16)
        r0 = pl.multiple_of(c * CK, CK)
        vo = jnp.concatenate([kvs_ref[pl.ds(r0, CK), 128:256], ones_k], axis=1)
        acc_scr[...] += _dot(e, vo)
        return carry

    lax.fori_loop(0, nk, body2, 0)
    acc = acc_scr[...]
    o_slc = acc[:, 0:128] / acc[:, 128:129]

    w0 = pl.multiple_of(jnp.maximum(q0 - WINDOW, 0), QT)
    WK = WINDOW + QT
    dist_w = qposm - (w0 + lax.broadcasted_iota(I32, (1, WK), 1))
    mask_w = (dist_w >= 0) & (dist_w < WINDOW)
    kwin = jnp.concatenate([win_ref[pl.ds(w0, WK), 0:128], kaug_ref[pl.ds(w0, WK), :]], axis=1)
    sw = jnp.where(mask_w, _dot_nt(qwin, kwin), NEG)
    e = jnp.exp(sw - jnp.max(sw, axis=1, keepdims=True)).astype(BF16)
    vo = jnp.concatenate([win_ref[pl.ds(w0, WK), 128:256], jnp.ones((WK, LANES), BF16)], axis=1)
    accw = _dot(e, vo)
    o_win = accw[:, 0:128] / accw[:, 128:129]

    gates = _sigmoid(sm_ref[...])
    gc, gs, gw = _gate_expand(gates, gexp_ref)
    for r in range(NSA_R):
        sl = slice(128 * r, 128 * r + 128)
        r0 = slice(QT * r, QT * (r + 1))
        r1 = slice(QT * (NSA_R + r), QT * (NSA_R + r + 1))
        oc = jnp.where(left, o_cmp[r0], o_cmp[r1])
        os_ = jnp.where(left, o_slc[r0], o_slc[r1])
        ow = jnp.where(left, o_win[r0], o_win[r1])
        o_ref[:, sl] = gc[:, sl] * oc + gs[:, sl] * os_ + gw[:, sl] * ow


def _nsa_prompt(q, sm, kc_eo, vc_eo, kvs_bf, win_bf, kaug, gexp):
    b, t, _ = q.shape
    return pl.pallas_call(
        functools.partial(_nsa_prompt_body, T=t), grid=(b, t // 128),
        in_specs=[pl.BlockSpec((None, 128, 512), lambda i, j: (i, j, 0)),
                  pl.BlockSpec((None, 128, 128), lambda i, j: (i, j, 0)),
                  pl.BlockSpec((None, t // CMP_BLK, 128), lambda i, j: (i, 0, 0)),
                  pl.BlockSpec((None, t // CMP_BLK, 128), lambda i, j: (i, 0, 0)),
                  pl.BlockSpec((None, t, 256), lambda i, j: (i, 0, 0)),
                  pl.BlockSpec(kaug.shape, lambda i, j: (0, 0)),
                  pl.BlockSpec((None, t, 256), lambda i, j: (i, 0, 0)),
                  pl.BlockSpec(gexp.shape, lambda i, j: (0, 0, 0))],
        out_specs=pl.BlockSpec((None, 128, 512), lambda i, j: (i, j, 0)),
        out_shape=jax.ShapeDtypeStruct((b, t, 512), F32),
        scratch_shapes=[pltpu.VMEM((t // KEY_CHUNK, NSA_H * 128, KEY_CHUNK), F32),
                        pltpu.VMEM((NSA_H * 128, LANES), F32), pltpu.VMEM((NSA_H * 128, 2 * LANES), F32)],
        compiler_params=_cparams(("parallel", "arbitrary")), name="nsa_prompt",
    )(q, sm, kc_eo, vc_eo, kvs_bf, kaug, win_bf, gexp)


def _sel_block_of_lane(lane, n_pages):
    return jnp.where(lane < n_pages, 2 * lane,
                     jnp.where(lane < 2 * n_pages, 2 * (lane - n_pages) + 1,
                               jnp.where(lane == 2 * n_pages, 2 * n_pages, 1 << 20)))


def _nsa_sample_sel_body(q_ref, kc_ref, vc_ref, wbuf_ref, wnew_ref, oc_ref, ow_ref, sc_ref, *, past, n_pages):
    R8 = 8
    per_page = PAGE // CMP_BLK
    ncmp = per_page * n_pages
    qpos = past + lax.broadcasted_iota(I32, (R8, 1), 0)
    lane_c = lax.broadcasted_iota(I32, (1, ncmp), 1)
    blk = (lane_c % n_pages) * per_page + lane_c // n_pages
    dist_c = qpos - (blk * CMP_BLK + (CMP_BLK - 1))
    mask_c = dist_c >= 0
    dist_cf = dist_c.astype(F32)
    kc = kc_ref[...].reshape(ncmp, LANES).astype(BF16)
    vc = vc_ref[...].reshape(ncmp, LANES).astype(BF16)

    wb = wbuf_ref.shape[3]
    wk = wbuf_ref[0].reshape(NSA_G * HD, wb).astype(BF16)
    wv = wbuf_ref[1].reshape(NSA_G * HD, wb).astype(BF16)
    nk = wnew_ref[:, 0:128].astype(BF16)
    nv = wnew_ref[:, 128:256].astype(BF16)
    dist_w1 = qpos - (past - wb + lax.broadcasted_iota(I32, (1, wb), 1))
    mask_w1 = (dist_w1 >= 0) & (dist_w1 < WINDOW)
    dist_w2 = qpos - (past + lax.broadcasted_iota(I32, (1, R8), 1))
    mask_w2 = (dist_w2 >= 0) & (dist_w2 < WINDOW)

    ns_l = 3 * LANES
    assert 2 * n_pages + 1 <= ns_l
    lane_s = lax.broadcasted_iota(I32, (1, ns_l), 1)
    jmap = _sel_block_of_lane(lane_s, n_pages)
    cur = qpos // SLC_BLK
    lane16 = lax.broadcasted_iota(I32, (1, LANES), 1)
    left = lane16 < 64

    oc = [[None] * NSA_R for _ in range(NSA_G)]
    ow = [[None] * NSA_R for _ in range(NSA_G)]
    for g in range(NSA_G):
        hm = _half_mask(g)
        imp = jnp.zeros((R8, ncmp), F32)
        for r in range(NSA_R):
            slope = SLOPES[g * NSA_R + r]
            qh = jnp.where(hm, q_ref[:, 128 * r:128 * r + 128], 0).astype(BF16)
            s = _dot_nt(qh, kc) * (HD ** -0.5) - slope * dist_cf
            p = _masked_softmax(s, mask_c)
            imp = imp + p
            oc[g][r] = _dot(p.astype(BF16), vc)
            s1 = jnp.where(mask_w1, _dot(qh, wk) * (HD ** -0.5) - slope * dist_w1.astype(F32), NEG)
            s2 = jnp.where(mask_w2, _dot_nt(qh, nk) * (HD ** -0.5) - slope * dist_w2.astype(F32), NEG)
            mx = jnp.maximum(jnp.max(s1, axis=1, keepdims=True), jnp.max(s2, axis=1, keepdims=True))
            e1 = jnp.exp(s1 - mx)
            e2 = jnp.exp(s2 - mx)
            den = jnp.sum(e1, axis=1, keepdims=True) + jnp.sum(e2, axis=1, keepdims=True)
            p1 = jnp.where(mask_w1, e1 / den, 0.0)
            p2 = jnp.where(mask_w2, e2 / den, 0.0)
            ow[g][r] = _dot_nt(p1.astype(BF16), wv) + _dot(p2.astype(BF16), nv)
        np_ = n_pages
        pooled = jnp.concatenate([imp[:, 0:np_] + imp[:, np_:2 * np_], imp[:, 2 * np_:3 * np_] + imp[:, 3 * np_:],
                                  jnp.zeros((R8, ns_l - 2 * np_), F32)], axis=1)
        valid = jmap <= cur
        forced = (jmap == 0) | (jmap == cur) | (jmap == cur - 1)
        score = jnp.where(valid, pooled + FORCE_BONUS * forced.astype(F32), NEG)
        sc_ref[g] = jnp.where(lane_s <= 2 * np_, score, -jnp.inf)
    for r in range(NSA_R):
        sl = slice(128 * r, 128 * r + 128)
        oc_ref[:, sl] = jnp.where(left, oc[0][r], oc[1][r])
        ow_ref[:, sl] = jnp.where(left, ow[0][r], ow[1][r])


def _nsa_sample_sel(q8, kvc, win_t, layer, wnew8, past):
    b = q8.shape[0]
    n_pages = past // PAGE
    per_page = PAGE // CMP_BLK
    wb = win_t.shape[-1]
    return pl.pallas_call(
        functools.partial(_nsa_sample_sel_body, past=past, n_pages=n_pages), grid=(b,),
        in_specs=[pl.BlockSpec((None, 8, 512), lambda i: (i, 0, 0)),
                  pl.BlockSpec((None, per_page, n_pages, 128), lambda i: (0, 0, i, 0)),
                  pl.BlockSpec((None, per_page, n_pages, 128), lambda i: (1, 0, i, 0)),
                  pl.BlockSpec((None, None, 2, NSA_G, HD, wb), lambda i: (layer, i, 0, 0, 0, 0)),
                  pl.BlockSpec((None, 8, 256), lambda i: (i, 0, 0))],
        out_specs=[pl.BlockSpec((None, 8, 512), lambda i: (i, 0, 0)),
                   pl.BlockSpec((None, 8, 512), lambda i: (i, 0, 0)),
                   pl.BlockSpec((None, 2, 8, 3 * LANES), lambda i: (i, 0, 0, 0))],
        out_shape=[jax.ShapeDtypeStruct((b, 8, 512), F32), jax.ShapeDtypeStruct((b, 8, 512), F32),
                   jax.ShapeDtypeStruct((b, 2, 8, 3 * LANES), F32)],
        compiler_params=_cparams(("parallel",)), name="nsa_sample_sel",
    )(q8, kvc, kvc, win_t, wnew8)


def _topk_body(sc_ref, idx_ref, *, n_pages):
    score = sc_ref[...]
    rows, width = score.shape
    blk_f = _sel_block_of_lane(lax.broadcasted_iota(I32, (1, width), 1), n_pages).astype(F32)
    lane_o = lax.broadcasted_iota(I32, (1, LANES), 1)
    idx_acc = jnp.zeros((rows, LANES), F32)
    for n in range(N_SEL):
        mx = jnp.max(score, axis=1, keepdims=True)
        pick = jnp.min(jnp.where(score == mx, blk_f, 2e6), axis=1, keepdims=True)
        idx_acc = jnp.where(lane_o == n, jnp.where(mx > 0.5 * NEG, pick, -1.0), idx_acc)
        score = jnp.where(blk_f == pick, -jnp.inf, score)
    idx_ref[...] = idx_acc.astype(I32)


def _topk(score2, n_pages):
    rows, width = score2.shape
    return pl.pallas_call(
        functools.partial(_topk_body, n_pages=n_pages), grid=(1,),
        in_specs=[pl.BlockSpec((rows, width), lambda i: (0, 0))],
        out_specs=pl.BlockSpec((rows, LANES), lambda i: (0, 0)),
        out_shape=jax.ShapeDtypeStruct((rows, LANES), I32),
        compiler_params=_cparams(("arbitrary",)), name="topk",
    )(score2)


def _nsa_sample_slc_body(idx_ref, tile_ref, pool_ref, q_ref, knew_ref, sm_ref, oc_ref, ow_ref, gexp_ref,
                         o_ref, kbuf, vbuf, sem, *, T, past):
    b = pl.program_id(0)
    nb = pl.num_programs(0)
    ns_past = past // SLC_BLK
    per_page = PAGE // SLC_BLK
    nslot = NSA_G * T * N_SEL

    def start_all(bb, buf_slot):
        def body(i, carry):
            kt = tile_ref[bb * nslot + i]
            pltpu.make_async_copy(pool_ref.at[kt], kbuf.at[buf_slot, i], sem.at[0, buf_slot]).start()
            pltpu.make_async_copy(pool_ref.at[kt + NSA_G], vbuf.at[buf_slot, i], sem.at[1, buf_slot]).start()
            return carry
        lax.fori_loop(0, nslot, body, 0, unroll=4)

    @pl.when(b == 0)
    def _():
        start_all(0, 0)

    cur_slot = lax.rem(b, 2)

    @pl.when(b + 1 < nb)
    def _():
        start_all(b + 1, 1 - cur_slot)

    pltpu.make_async_copy(kbuf.at[cur_slot], kbuf.at[cur_slot], sem.at[0, cur_slot]).wait()
    pltpu.make_async_copy(vbuf.at[cur_slot], vbuf.at[cur_slot], sem.at[1, cur_slot]).wait()

    R8 = 8
    NK = N_SEL * PAGE
    lane_k = lax.broadcasted_iota(I32, (1, NK), 1)
    slot_k = lane_k // PAGE
    within = lane_k - slot_k * PAGE
    half_k = within // SLC_BLK
    off_k = within - half_k * SLC_BLK
    rowi = lax.broadcasted_iota(I32, (R8, 1), 0)
    knew = knew_ref[:, 0:128].astype(BF16)
    vnew = knew_ref[:, 128:256].astype(BF16)
    kpos_new = past + lax.broadcasted_iota(I32, (1, R8), 1)
    zhalf = jnp.zeros((R8, HD), F32)
    o_acc = [jnp.zeros((R8, LANES), F32) for _ in range(NSA_R)]
    for g in range(NSA_G):
        hm = _half_mask(g)
        slope = jnp.zeros((R8, 1), F32)
        for r in range(NSA_R):
            slope = jnp.where(rowi == r, SLOPES[g * NSA_R + r], slope)
        for t in range(T):
            qpos = past + t
            qm = jnp.zeros((R8, LANES), F32)
            for r in range(NSA_R):
                qm = jnp.where(rowi == r, q_ref[t:t + 1, 128 * r:128 * r + 128], qm)
            qm = jnp.where(hm, qm, 0.0)
            qc = (qm[:, 0:HD] + qm[:, HD:2 * HD]).astype(BF16)
            base = (g * T + t) * N_SEL
            kt = jnp.concatenate([kbuf[cur_slot, base + n] for n in range(N_SEL)], axis=1).astype(BF16)
            vt = jnp.concatenate([vbuf[cur_slot, base + n] for n in range(N_SEL)], axis=1).astype(BF16)
            kpos = off_k
            ok_i = jnp.zeros((1, NK), I32)
            has_new = jnp.int32(0)
            for n in range(N_SEL):
                j = idx_ref[((b * NSA_G + g) * 8 + t) * N_SEL + n]
                in_blk = (slot_k == n) & (half_k == lax.rem(j, per_page))
                kpos = jnp.where(in_blk, j * SLC_BLK + off_k, kpos)
                ok_i = jnp.where(in_blk, ((j >= 0) & (j < ns_past)).astype(I32), ok_i)
                has_new = has_new + (j == ns_past).astype(I32)
            dist1 = qpos - kpos
            mask1 = (ok_i > 0) & (dist1 >= 0)
            dist2 = qpos - kpos_new
            mask2 = (dist2 >= 0) & ((jnp.zeros((1, R8), I32) + has_new) > 0)
            s1 = jnp.where(mask1, _dot(qc, kt) * (HD ** -0.5) - slope * dist1.astype(F32), NEG)
            s2 = jnp.where(mask2, _dot_nt(qm.astype(BF16), knew) * (HD ** -0.5) - slope * dist2.astype(F32), NEG)
            mx = jnp.maximum(jnp.max(s1, axis=1, keepdims=True), jnp.max(s2, axis=1, keepdims=True))
            e1 = jnp.exp(s1 - mx)
            e2 = jnp.exp(s2 - mx)
            den = jnp.sum(e1, axis=1, keepdims=True) + jnp.sum(e2, axis=1, keepdims=True)
            p1 = jnp.where(mask1, e1 / den, 0.0)
            p2 = jnp.where(mask2, e2 / den, 0.0)
            o1 = _dot_nt(p1.astype(BF16), vt)
            o1 = jnp.concatenate([o1, zhalf] if g == 0 else [zhalf, o1], axis=1)
            o = o1 + jnp.where(hm, _dot(p2.astype(BF16), vnew), 0.0)
            for r in range(NSA_R):
                o_acc[r] = jnp.where((rowi == t) & hm, o[r:r + 1, :], o_acc[r])
    gates = _sigmoid(sm_ref[...])
    gc, gs, gw = _gate_expand(gates, gexp_ref)
    for r in range(NSA_R):
        sl = slice(128 * r, 128 * r + 128)
        o_ref[:, sl] = gc[:, sl] * oc_ref[:, sl] + gs[:, sl] * o_acc[r] + gw[:, sl] * ow_ref[:, sl]


def _nsa_sample_slc(idx, pt_flat, pool_t, q8, kvnew8, sm8, oc, ow, gexp, T, past):
    b = q8.shape[0]
    n_pages = pt_flat.shape[0] // b
    row = lambda w: pl.BlockSpec((None, 8, w), lambda i, a, c: (i, 0, 0))
    nslot = NSA_G * T * N_SEL
    page = jnp.clip(idx[:, :, 0:T], 0, past // SLC_BLK - 1) // (PAGE // SLC_BLK)
    phys = jnp.take_along_axis(pt_flat.reshape(b, 1, 1, n_pages), page.reshape(b, 1, 1, -1), axis=3)
    grp = jnp.arange(NSA_G, dtype=I32).reshape(1, NSA_G, 1, 1)
    tiles = ((phys.reshape(page.shape) * 4 + 2) * NSA_G + grp).reshape(-1)
    idx_flat = idx.reshape(-1)
    grid_spec = pltpu.PrefetchScalarGridSpec(
        num_scalar_prefetch=2, grid=(b,),
        in_specs=[pl.BlockSpec(memory_space=pl.ANY), row(512), row(256), row(128), row(512), row(512),
                  pl.BlockSpec(gexp.shape, lambda i, a, c: (0, 0, 0))],
        out_specs=row(512),
        scratch_shapes=[pltpu.VMEM((2, nslot, HD, PAGE), F32), pltpu.VMEM((2, nslot, HD, PAGE), F32),
                        pltpu.SemaphoreType.DMA((2, 2))])
    return pl.pallas_call(
        functools.partial(_nsa_sample_slc_body, T=T, past=past), grid_spec=grid_spec,
        out_shape=jax.ShapeDtypeStruct((b, 8, 512), F32),
        compiler_params=_cparams(("arbitrary",)), name="nsa_sample_slc",
    )(idx_flat, tiles, pool_t, q8, kvnew8, sm8, oc, ow, gexp)


def _mix_out_body(ya_ref, yb_ref, yc_ref, x_ref, g_ref, w_ref, pool_ref, poolt_ref, o_ref):
    y = jnp.concatenate([ya_ref[...], yb_ref[...], yc_ref[...]], axis=-1)
    ms = _dot_split(y * y, pool_ref[...])
    rb = _dot_split(lax.rsqrt(ms + EPS), poolt_ref[...])
    yn = (y * rb * g_ref[...]).astype(BF16)
    o_ref[...] = x_ref[...] + _dot(yn, w_ref[...])


def _mix_out(ya, yb, yc, x2, g, w_bf, pool, poolt, tm):
    n = x2.shape[0]
    row = lambda w: pl.BlockSpec((tm, w), lambda i: (i, 0))
    full = lambda a: pl.BlockSpec(a.shape, lambda i: (0,) * a.ndim)
    return pl.pallas_call(
        _mix_out_body, grid=(n // tm,),
        in_specs=[row(256), row(256), row(512), row(1024), full(g), full(w_bf), full(pool), full(poolt)],
        out_specs=row(1024), out_shape=jax.ShapeDtypeStruct((n, 1024), F32),
        compiler_params=_cparams(("parallel",)), name="mix_out",
    )(ya, yb, yc, x2, g, w_bf, pool, poolt)


def _xattn_body(q_ref, kv_ref, o_ref):
    for h in range(XA_H):
        if len(kv_ref.shape) == 2:
            k = kv_ref[:, 128 * h:128 * h + 128].astype(BF16)
            v = kv_ref[:, XA_INNER + 128 * h:XA_INNER + 128 * h + 128].astype(BF16)
        else:
            k = kv_ref[:, 0, h, :].astype(BF16)
            v = kv_ref[:, 1, h, :].astype(BF16)
        s = _dot_nt(q_ref[:, 128 * h:128 * h + 128], k) * (XA_HD ** -0.5)
        e = jnp.exp(s - jnp.max(s, axis=-1, keepdims=True))
        a = e / jnp.sum(e, axis=-1, keepdims=True)
        o_ref[:, 128 * h:128 * h + 128] = _dot(a.astype(BF16), v).astype(BF16)


def _xattn(q, mkv, tq, layer=None):
    b, t, _ = q.shape
    if layer is None:
        kv_spec = pl.BlockSpec((None, N_MEM, 1024), lambda i, j: (i, 0, 0))
    else:
        kv_spec = pl.BlockSpec((None, None, N_MEM, 2, XA_H, XA_HD), lambda i, j: (layer, i, 0, 0, 0, 0))
    return pl.pallas_call(
        _xattn_body, grid=(b, t // tq),
        in_specs=[pl.BlockSpec((None, tq, 512), lambda i, j: (i, j, 0)), kv_spec],
        out_specs=pl.BlockSpec((None, tq, 512), lambda i, j: (i, j, 0)),
        out_shape=jax.ShapeDtypeStruct((b, t, 512), BF16),
        compiler_params=_cparams(("parallel", "arbitrary")), name="xattn",
    )(q, mkv)


def _moe_body(x_ref, g_ref, rwh_ref, rwl_ref, rb_ref, wg_ref, wu_ref, wd_ref, o_ref, zn_s, comb_s, acc_s):
    e = pl.program_id(1)
    lane = lax.broadcasted_iota(I32, (1, LANES), 1)

    @pl.when(e == 0)
    def _():
        z = _rms(x_ref[...], g_ref[...])
        zh, zl = _split2(z)
        zn_s[...] = zh
        logits = _dot(zh, rwh_ref[...]) + _dot(zh, rwl_ref[...]) + _dot(zl, rwh_ref[...]) + rb_ref[...]
        is_g = (lane >= N_EXP) & (lane < N_EXP + N_GROUPS)
        gl = jnp.where(is_g, logits, -jnp.inf)
        gmax = jnp.max(gl, axis=-1, keepdims=True)
        gw = 1.0 / jnp.sum(jnp.exp(gl - gmax), axis=-1, keepdims=True)
        lanef = lane.astype(F32)
        grpf = (lane // EXP_PER_GROUP).astype(F32)
        gsel = jnp.min(jnp.where(gl == gmax, lanef, 1e6), axis=-1, keepdims=True) - N_EXP
        in_grp = (lane < N_EXP) & (grpf == gsel)
        le = jnp.where(in_grp, logits, -jnp.inf)
        v1 = jnp.max(le, axis=-1, keepdims=True)
        i1 = jnp.min(jnp.where(le == v1, lanef, 1e6), axis=-1, keepdims=True)
        le2 = jnp.where(lanef == i1, -jnp.inf, le)
        v2 = jnp.max(le2, axis=-1, keepdims=True)
        i2 = jnp.min(jnp.where(le2 == v2, lanef, 1e6), axis=-1, keepdims=True)
        e2 = jnp.exp(v2 - v1)
        w1 = 1.0 / (1.0 + e2)
        w2 = e2 / (1.0 + e2)
        comb_s[...] = gw * (jnp.where(lanef == i1, w1, 0.0) + jnp.where(lanef == i2, w2, 0.0))
        acc_s[...] = jnp.zeros_like(acc_s)

    zn = zn_s[...]
    hg = _dot(zn, wg_ref[...])
    hu = _dot(zn, wu_ref[...])
    ce = jnp.sum(jnp.where(lane == e, comb_s[...], 0.0), axis=-1, keepdims=True)
    h = (hg * _sigmoid(hg)) * hu * ce
    acc_s[...] += _dot(h.astype(BF16), wd_ref[...])

    @pl.when(e == pl.num_programs(1) - 1)
    def _():
        o_ref[...] = x_ref[...] + acc_s[...]


def _moe(x2, g, rwh, rwl, rb, wg, wu, wd, layer, tm):
    n = x2.shape[0]
    full = lambda a: pl.BlockSpec(a.shape, lambda i, e: (0,) * a.ndim)
    return pl.pallas_call(
        _moe_body, grid=(n // tm, N_EXP),
        in_specs=[pl.BlockSpec((tm, 1024), lambda i, e: (i, 0)), full(g), full(rwh), full(rwl), full(rb),
                  pl.BlockSpec((None, None, 1024, EXP_FF), lambda i, e: (layer, e, 0, 0)),
                  pl.BlockSpec((None, None, 1024, EXP_FF), lambda i, e: (layer, e, 0, 0)),
                  pl.BlockSpec((None, None, EXP_FF, 1024), lambda i, e: (layer, e, 0, 0))],
        out_specs=pl.BlockSpec((tm, 1024), lambda i, e: (i, 0)),
        out_shape=jax.ShapeDtypeStruct((n, 1024), F32),
        scratch_shapes=[pltpu.VMEM((tm, 1024), BF16), pltpu.VMEM((tm, 128), F32), pltpu.VMEM((tm, 1024), F32)],
        compiler_params=_cparams(("parallel", "arbitrary")), name="moe",
    )(x2, g, rwh, rwl, rb, wg, wu, wd)


def _q_perm():
    idx = []
    for r in range(NSA_R):
        for g in range(NSA_G):
            h = g * NSA_R + r
            idx.extend(range(64 * h, 64 * h + 64))
    return np.asarray(idx)


def _prep_layer(w, l):
    p = {"layer": l}
    win = w["w_in"][l]
    qp = _q_perm()
    o_q = 1032 + 512
    cols = [win[:, 0:1024], win[:, 1032:1032 + 512], win[:, o_q:o_q + 512][:, qp],
            win[:, o_q + 512:o_q + 512 + 768], win[:, 1024:1032], win[:, o_q + 1280:o_q + 1304],
            jnp.zeros((D_MODEL, 128 - 32), F32)]
    p["w_in"] = jnp.concatenate(cols, axis=1).astype(BF16)
    p["w_st"] = win[:, 1024:1032].T.astype(BF16)
    p["norm_mix"] = w["norm_mix"][l][None]
    bias8 = jnp.concatenate([w["ml_i_bias"][l], w["ml_f_bias"][l]])
    p["ml_bcol"] = jnp.zeros((1, 128), F32).at[0, 0:8].set(bias8)
    p["ml_brow"] = bias8[:, None]
    p["conv_w"] = w["conv_w"][l]
    p["conv_b"] = w["conv_b"][l][None]
    bd = lambda m: jax.scipy.linalg.block_diag(*[m[i] for i in range(m.shape[0])])
    p["lru_wa"] = bd(w["lru_wa"][l]).astype(BF16)
    p["lru_wx"] = bd(w["lru_wx"][l]).astype(BF16)
    p["lru_ba"] = w["lru_ba"][l][None]
    p["lru_bx"] = w["lru_bx"][l][None]
    p["lru_lambda"] = w["lru_lambda"][l][None]
    cw = {}
    eye4 = jnp.eye(PAGE // CMP_BLK, dtype=F32)
    pw, ppos, pb1, pw2 = [], [], [], []
    for c, nm in ((0, "k"), (1, "v")):
        w1 = w["phi_w1"][l, c].reshape(CMP_BLK, HD, CMP_HID)
        z = jnp.zeros_like(w1)
        cw["bd" + nm] = jnp.concatenate([jnp.concatenate([w1, z], axis=2),
                                         jnp.concatenate([z, w1], axis=2)], axis=1).astype(BF16)
        cw["pos" + nm] = jnp.tile(w["cmp_pos"][l, c], (1, 2))
        cw["b1" + nm] = jnp.tile(w["phi_b1"][l, c], 2)[None]
        w2 = w["phi_w2"][l, c]
        z2 = jnp.zeros_like(w2)
        cw["w2" + nm] = jnp.concatenate([jnp.concatenate([w2, z2], axis=1),
                                         jnp.concatenate([z2, w2], axis=1)], axis=0).astype(BF16)
        w3 = w1.transpose(1, 0, 2).reshape(HD // 2, 2, CMP_BLK, CMP_HID)
        pw.append(jnp.einsum("pdrh,bc->pdbrch", w3, eye4).reshape(HD // 2, 256, 512).astype(BF16))
        pos3 = w["cmp_pos"][l, c].T.reshape(HD // 2, 2, 1, CMP_BLK)
        ppos.append(jnp.broadcast_to(pos3, (HD // 2, 2, 4, CMP_BLK)).reshape(HD // 2, 256))
        pb1.append(jnp.tile(w["phi_b1"][l, c], 4)[None])
        pw2.append(jnp.kron(eye4, w2).astype(BF16))
    cw["pw"] = jnp.stack(pw)
    cw["ppos"] = jnp.stack(ppos)
    cw["pb1"] = jnp.stack(pb1)
    cw["pw2"] = jnp.stack(pw2)
    p["cmp"] = cw
    perm = np.concatenate([np.arange(512), 512 + qp])
    p["mix_norm"] = w["mix_norm"][l][perm][None]
    p["w_out"] = w["w_out"][l][perm, :].astype(BF16)
    p["norm_xa"] = w["norm_xa"][l][None]
    p["norm_mem"] = w["norm_mem"][l][None]
    p["xa_wq"] = w["xa_wq"][l].astype(BF16)
    p["xa_wkv"] = w["xa_wkv"][l].astype(BF16)
    p["xa_wo"] = w["xa_wo"][l].astype(BF16)
    p["norm_ffn"] = w["norm_ffn"][l][None]
    rw = jnp.concatenate([w["router_ew"][l], w["router_gw"][l], jnp.zeros((D_MODEL, 128 - 20), F32)], axis=1)
    rwh = rw.astype(BF16)
    p["rwh"] = rwh
    p["rwl"] = (rw - rwh.astype(F32)).astype(BF16)
    p["rb"] = jnp.concatenate([w["router_eb"][l], w["router_gb"][l], jnp.zeros((128 - 20,), F32)])[None]
    return p


def _constants(T):
    c = {}
    head = np.arange(1024) // HD
    pool = np.zeros((1024, 128), np.float32)
    pool[np.arange(1024), head] = 1.0 / HD
    poolt = np.zeros((128, 1024), np.float32)
    poolt[head, np.arange(1024)] = 1.0
    c["pool"] = jnp.asarray(pool, BF16)
    c["poolt"] = jnp.asarray(poolt, BF16)
    gexp = np.zeros((3, 128, 512), np.float32)
    for cc in range(3):
        for g in range(NSA_G):
            for r in range(NSA_R):
                gexp[cc, 8 + cc * 8 + g * 4 + r, 128 * r + 64 * g:128 * r + 64 * g + 64] = 1.0
    c["gexp"] = jnp.asarray(gexp, BF16)
    kpos = np.arange(T)
    kaug = np.zeros((T, 128), np.float32)
    kaug[:, AUG_HI] = kpos // SLC_BLK
    kaug[:, AUG_LO] = kpos % SLC_BLK
    kaug[:, AUG_ONE] = 1.0
    kaug[kpos, AUG_SEL + kpos // SLC_BLK] = 1.0
    c["kaug"] = jnp.asarray(kaug, BF16)
    return c


def _even_odd(a):
    return jnp.concatenate([a[:, 0::2], a[:, 1::2]], axis=1)


def _pad_rows(a, rows):
    return jnp.pad(a, ((0, 0), (0, rows - a.shape[1]), (0, 0)))


def _dense_tail(x2, p, ew, B, T, mkv, tm, tq, tm_moe, cache_layer=None):
    n = x2.shape[0]
    q = _norm_mm(x2, p["norm_xa"], p["xa_wq"], tm, BF16).reshape(B, T, XA_INNER)
    if T < tq:
        o = _xattn(_pad_rows(q, tq), mkv, tq, cache_layer)[:, 0:T]
    else:
        o = _xattn(q, mkv, tq, cache_layer)
    x2 = _mm_res(o.reshape(n, XA_INNER), p["xa_wo"], x2, tm)
    return _moe(x2, p["norm_ffn"], p["rwh"], p["rwl"], p["rb"], ew[0], ew[1], ew[2], p["layer"], tm_moe)


def _layer_prompt(x2, p, c, ew, B, T, mem2):
    n = B * T
    ml, lru, q, kv, win, sm, smt, kvs_bf, win_bf = _in_proj(x2, p["norm_mix"], p["w_in"], p["w_st"], 256)
    smt_b = smt.reshape(8, B, T).transpose(1, 0, 2)
    ya, C, nn, mm = _mlstm(ml.reshape(B, T, 1024), sm.reshape(B, T, 128), smt_b, p["ml_bcol"], p["ml_brow"],
                           jnp.zeros((B, 4, 64, 64), F32), jnp.zeros((B, 4, 1, 64), F32),
                           jnp.zeros((B, 1, 128), F32), ML_CHUNK, ML_CHUNK)
    u_tm = lru.reshape(B, T, 512).transpose(1, 0, 2)
    yb_tm, tail_tm, h_last = _lru(u_tm, jnp.zeros((3, B, 256), F32), jnp.zeros((B, 256), F32), p["conv_w"],
                                  p["conv_b"], p["lru_wa"], p["lru_ba"], p["lru_wx"], p["lru_bx"],
                                  p["lru_lambda"], 256)
    yb = yb_tm.transpose(1, 0, 2).reshape(n, 256)
    n_pages = n // PAGE
    kc, vc = _compress_rows(jnp.arange(n_pages, dtype=I32), kv.reshape(n_pages, PAGE, 512), p["cmp"], 16)
    nc = T // CMP_BLK
    kc_eo = _even_odd(kc.reshape(B, nc, 128))
    vc_eo = _even_odd(vc.reshape(B, nc, 128))
    yc = _nsa_prompt(q.reshape(B, T, 512), sm.reshape(B, T, 128), kc_eo, vc_eo, kvs_bf.reshape(B, T, 256),
                     win_bf.reshape(B, T, 256), c["kaug"], c["gexp"])
    x2 = _mix_out(ya.reshape(n, 256), yb, yc.reshape(n, 512), x2, p["mix_norm"], p["w_out"], c["pool"],
                  c["poolt"], 256)
    mkv = _norm_mm(mem2, p["norm_mem"], p["xa_wkv"], 256, F32)
    x2 = _dense_tail(x2, p, ew, B, T, mkv.reshape(B, N_MEM, 1024), 256, 512, 1024)
    st = (kv.reshape(B, T, 4, NSA_G, HD), win.reshape(B, T, 2, NSA_G, HD)[:, T - WINDOW:],
          C, nn.reshape(B, 4, 64), mm[:, 0, 0:4], h_last, tail_tm.transpose(1, 0, 2))
    return x2, st, mkv.reshape(B, N_MEM, 2, XA_H, XA_HD)


def _layer_sample(x2, p, c, ew, B, T, pool_t, pt_flat, win_buf, win_t, C0, n0, m0, conv0, h0, mkv, past):
    n = B * T
    _pad_rows8 = lambda a: _pad_rows(a, 8)
    ml, lru, q, kv, win, sm, smt, _, _ = _in_proj(x2, p["norm_mix"], p["w_in"], p["w_st"], n)
    sm8 = _pad_rows8(sm.reshape(B, T, 128))
    smt_b = jnp.pad(smt.reshape(8, B, T).transpose(1, 0, 2), ((0, 0), (0, 0), (0, 8 - T)))
    m0p = jnp.pad(m0[:, None, :], ((0, 0), (0, 0), (0, 128 - ML_H)))
    ya, C, nn, mm = _mlstm(_pad_rows8(ml.reshape(B, T, 1024)), sm8, smt_b, p["ml_bcol"], p["ml_brow"],
                           C0, n0[:, :, None, :], m0p, 8, T)
    ya = ya[:, 0:T]
    u_tm = lru.reshape(B, T, 512).transpose(1, 0, 2)
    yb_tm, tail_tm, h_last = _lru(u_tm, conv0.transpose(1, 0, 2), h0, p["conv_w"], p["conv_b"], p["lru_wa"],
                                  p["lru_ba"], p["lru_wx"], p["lru_bx"], p["lru_lambda"], T)
    yb = yb_tm.transpose(1, 0, 2).reshape(n, 256)
    n_pages = past // PAGE
    kvc = _compress_pages(pt_flat, pool_t, p["cmp"], CMP_STEP_PAGES)
    wb = win_buf.shape[1]
    q8 = _pad_rows8(q.reshape(B, T, 512).astype(F32))
    win3 = win.reshape(B, T, 256)
    oc, ow, sc = _nsa_sample_sel(q8, kvc, win_t, p["layer"], _pad_rows8(win3), past)
    idx = _topk(sc.reshape(B * NSA_G * 8, 3 * LANES), n_pages)[:, 0:N_SEL].reshape(B, NSA_G, 8, N_SEL)
    kv3 = kv.reshape(B, T, 512)
    yc8 = _nsa_sample_slc(idx, pt_flat, pool_t, q8, _pad_rows8(kv3[:, :, 256:512]), sm8, oc, ow,
                          c["gexp"], T, past)
    yc = yc8[:, 0:T].reshape(n, 512)
    x2 = _mix_out(ya.reshape(n, 256), yb, yc, x2, p["mix_norm"], p["w_out"], c["pool"], c["poolt"], n)
    x2 = _dense_tail(x2, p, ew, B, T, mkv, n, 16, n, cache_layer=p["layer"])
    win_new = jnp.concatenate([win_buf.reshape(B, wb, 256), win3], axis=1)[:, T:]
    st = (kv.reshape(B, T, 4, NSA_G, HD), win_new.reshape(B, wb, 2, NSA_G, HD), C, nn.reshape(B, 4, 64),
          mm[:, 0, 0:4], h_last, tail_tm.transpose(1, 0, 2))
    return x2, st


def kernel(x_prompt, x_sample, cache_nsa_kv, state_nsa_win, state_mlstm_C, state_mlstm_n, state_mlstm_m, state_rglru_h, state_rglru_conv, cache_mem_kv, page_table, mem_prompt, norm_mix, w_in, ml_i_bias, ml_f_bias, conv_w, conv_b, lru_wa, lru_ba, lru_wx, lru_bx, lru_lambda, phi_w1, phi_b1, phi_w2, cmp_pos, mix_norm, w_out, norm_xa, norm_mem, xa_wq, xa_wkv, xa_wo, norm_ffn, router_gw, router_gb, router_ew, router_eb, exp_w_gate, exp_w_up, exp_w_down, final_norm):
    w = dict(norm_mix=norm_mix, w_in=w_in, ml_i_bias=ml_i_bias, ml_f_bias=ml_f_bias, conv_w=conv_w,
             conv_b=conv_b, lru_wa=lru_wa, lru_ba=lru_ba, lru_wx=lru_wx, lru_bx=lru_bx, lru_lambda=lru_lambda,
             phi_w1=phi_w1, phi_b1=phi_b1, phi_w2=phi_w2, cmp_pos=cmp_pos, mix_norm=mix_norm, w_out=w_out,
             norm_xa=norm_xa, norm_mem=norm_mem, xa_wq=xa_wq, xa_wkv=xa_wkv, xa_wo=xa_wo, norm_ffn=norm_ffn,
             router_gw=router_gw, router_gb=router_gb, router_ew=router_ew, router_eb=router_eb)
    depth = w_in.shape[0]
    B, T, _ = x_prompt.shape
    BS, TS, _ = x_sample.shape
    n_pages = page_table.shape[1]
    past = n_pages * PAGE
    n_phys = cache_nsa_kv.shape[1]
    consts = _constants(T)
    layers = [_prep_layer(w, l) for l in range(depth)]
    ew = (exp_w_gate.astype(BF16), exp_w_up.astype(BF16), exp_w_down.astype(BF16))
    fin = final_norm[None]

    x2 = x_prompt.reshape(B * T, D_MODEL)
    mem2 = mem_prompt.reshape(B * N_MEM, D_MODEL)
    outs_p, mem_p = [], []
    for l in range(depth):
        x2, st, mkv = _layer_prompt(x2, layers[l], consts, ew, B, T, mem2)
        outs_p.append(st)
        mem_p.append(mkv)
    y_prompt = _final_norm(x2, fin, 256).reshape(B, T, D_MODEL)

    pool_t = jnp.transpose(cache_nsa_kv, (0, 1, 3, 4, 5, 2)).reshape(depth * n_phys * 4 * NSA_G, HD, PAGE)
    win_t = jnp.transpose(state_nsa_win, (0, 1, 3, 4, 5, 2))
    xs = x_sample.reshape(BS * TS, D_MODEL)
    outs_s = []
    for l in range(depth):
        pt_flat = page_table.reshape(-1) + l * n_phys
        xs, st = _layer_sample(xs, layers[l], consts, ew, BS, TS, pool_t, pt_flat, state_nsa_win[l], win_t,
                               state_mlstm_C[l], state_mlstm_n[l], state_mlstm_m[l], state_rglru_conv[l],
                               state_rglru_h[l], cache_mem_kv, past)
        outs_s.append(st)
    y_sample = _final_norm(xs, fin, BS * TS).reshape(BS, TS, D_MODEL)

    sp = [jnp.stack(a) for a in zip(*outs_p)]
    ss = [jnp.stack(a) for a in zip(*outs_s)]
    return (y_prompt, y_sample, sp[0], ss[0], sp[1], ss[1], sp[2], ss[2], sp[3], ss[3], sp[4], ss[4],
            sp[5], ss[5], sp[6], ss[6], jnp.stack(mem_p))
```

```python
import functools
import math

import jax
import jax.numpy as jnp
import numpy as np
from jax import lax
from jax.experimental import pallas as pl
from jax.experimental.pallas import tpu as pltpu

F32 = jnp.float32
BF16 = jnp.bfloat16
I32 = jnp.int32

D_MODEL = 1024
DEPTH = 2
PAGE = 128
HD = 64
ML_W = 256
ML_H = 4
LRU_W = 256
LRU_C = 8.0
CONV_W = 4
NSA_W = 512
NSA_H = 8
NSA_G = 2
NSA_R = 4
CMP_BLK = 32
CMP_HID = 128
SLC_BLK = 64
N_SEL = 16
WINDOW = 512
FORCE_BONUS = 100.0
XA_H = 4
XA_HD = 128
XA_INNER = 512
N_MEM = 256
N_GROUPS = 4
EXP_PER_GROUP = 4
N_EXP = 16
EXP_FF = 256
EPS = 1e-6
NEG = -1e30
SLOPES = tuple(2.0 ** (-(h + 1)) for h in range(NSA_H))

LANES = 128
SUBLANES = 8
VMEM_LIMIT = 56 * 1024 * 1024

IN_ML = (0, 1024)
IN_LRU = (1024, 1536)
IN_Q = (1536, 2048)
IN_KV = (2048, 2560)
IN_WIN = (2560, 2816)
IN_SMALL = (2816, 2944)
IN_COLS_P = 2944

AUG_HI = 0
AUG_LO = 1
AUG_ONE = 2
AUG_SEL = 8
KEY_CHUNK = 512
ML_BATCH_ROWS = 1
ML_CHUNK = 256
CMP_STEP_PAGES = 128


def _cparams(sem):
    return pltpu.CompilerParams(dimension_semantics=sem, vmem_limit_bytes=VMEM_LIMIT)


def _dot(a, b):
    return jnp.dot(a, b, preferred_element_type=F32)


def _dot_nt(a, b):
    return lax.dot_general(a, b, (((1,), (1,)), ((), ())), preferred_element_type=F32)


def _dot_tn(a, b):
    return lax.dot_general(a, b, (((0,), (0,)), ((), ())), preferred_element_type=F32)


def _split2(x):
    hi = x.astype(BF16)
    lo = (x - hi.astype(F32)).astype(BF16)
    return hi, lo


def _dot_split(x, w_bf):
    hi, lo = _split2(x)
    return _dot(hi, w_bf) + _dot(lo, w_bf)


def _split3(x):
    hi = x.astype(BF16)
    r1 = x - hi.astype(F32)
    mid = r1.astype(BF16)
    lo = (r1 - mid.astype(F32)).astype(BF16)
    return hi, mid, lo


def _dot3_rhs(w_bf, x):
    hi, mid, lo = _split3(x)
    return _dot(w_bf, hi) + _dot(w_bf, mid) + _dot(w_bf, lo)


def _dot3_lhs(x, w_bf):
    hi, mid, lo = _split3(x)
    return _dot(hi, w_bf) + _dot(mid, w_bf) + _dot(lo, w_bf)


def _sigmoid(x):
    return 1.0 / (1.0 + jnp.exp(-x))


def _gelu(x):
    return 0.5 * x * (1.0 + jnp.tanh(0.7978845608028654 * (x + 0.044715 * (x * x * x))))


def _softplus(x):
    return jnp.maximum(x, 0.0) + jnp.log1p(jnp.exp(-jnp.abs(x)))


def _log_sigmoid(x):
    return -_softplus(-x)


def _rms(x, g):
    return x * lax.rsqrt(jnp.mean(x * x, axis=-1, keepdims=True) + EPS) * g


def _masked_softmax(s, mask):
    sm = jnp.where(mask, s, NEG)
    e = jnp.exp(sm - jnp.max(sm, axis=-1, keepdims=True))
    p = e / jnp.sum(e, axis=-1, keepdims=True)
    return jnp.where(mask, p, 0.0)


def _in_proj_body(x_ref, g_ref, w_ref, wst_ref, ml_ref, lru_ref, q_ref, kv_ref, win_ref, sm_ref,
                  smt_ref, kvsb_ref, winb_ref):
    hb = _rms(x_ref[...], g_ref[...]).astype(BF16)

    def mm(rng):
        return _dot(hb, w_ref[:, rng[0]:rng[1]])

    ml_ref[...] = mm(IN_ML)
    lru_ref[...] = mm(IN_LRU)
    q_ref[...] = mm(IN_Q).astype(BF16)
    kv = mm(IN_KV)
    kv_ref[...] = kv
    kvsb_ref[...] = kv[:, 256:512].astype(BF16)
    win = mm(IN_WIN)
    win_ref[...] = win
    winb_ref[...] = win.astype(BF16)
    sm_ref[...] = mm(IN_SMALL)
    smt_ref[...] = _dot_nt(wst_ref[...], hb)


def _in_proj(x2, g, w_p, w_st, tm):
    n = x2.shape[0]
    row = lambda w: pl.BlockSpec((tm, w), lambda i: (i, 0))
    full = lambda a: pl.BlockSpec(a.shape, lambda i: (0,) * a.ndim)
    shapes = [(1024, F32), (512, F32), (512, BF16), (512, F32), (256, F32), (128, F32)]
    out_shape = [jax.ShapeDtypeStruct((n, w), dt) for w, dt in shapes]
    out_specs = [row(w) for w, _ in shapes]
    out_shape += [jax.ShapeDtypeStruct((8, n), F32), jax.ShapeDtypeStruct((n, 256), BF16),
                  jax.ShapeDtypeStruct((n, 256), BF16)]
    out_specs += [pl.BlockSpec((8, tm), lambda i: (0, i)), row(256), row(256)]
    return pl.pallas_call(
        _in_proj_body, grid=(n // tm,),
        in_specs=[row(D_MODEL), full(g), full(w_p), full(w_st)],
        out_specs=out_specs, out_shape=out_shape,
        compiler_params=_cparams(("parallel",)), name="in_proj",
    )(x2, g, w_p, w_st)


def _norm_mm_body(x_ref, g_ref, w_ref, o_ref):
    o_ref[...] = _dot(_rms(x_ref[...], g_ref[...]).astype(BF16), w_ref[...]).astype(o_ref.dtype)


def _norm_mm(x2, g, w_bf, tm, out_dtype):
    n, k = x2.shape
    m = w_bf.shape[1]
    return pl.pallas_call(
        _norm_mm_body, grid=(n // tm,),
        in_specs=[pl.BlockSpec((tm, k), lambda i: (i, 0)), pl.BlockSpec((1, k), lambda i: (0, 0)),
                  pl.BlockSpec((k, m), lambda i: (0, 0))],
        out_specs=pl.BlockSpec((tm, m), lambda i: (i, 0)),
        out_shape=jax.ShapeDtypeStruct((n, m), out_dtype),
        compiler_params=_cparams(("parallel",)), name="norm_mm",
    )(x2, g, w_bf)


def _mm_res_body(a_ref, w_ref, x_ref, o_ref):
    o_ref[...] = x_ref[...] + _dot(a_ref[...], w_ref[...])


def _mm_res(a_bf, w_bf, x2, tm):
    n, k = a_bf.shape
    m = w_bf.shape[1]
    return pl.pallas_call(
        _mm_res_body, grid=(n // tm,),
        in_specs=[pl.BlockSpec((tm, k), lambda i: (i, 0)), pl.BlockSpec((k, m), lambda i: (0, 0)),
                  pl.BlockSpec((tm, m), lambda i: (i, 0))],
        out_specs=pl.BlockSpec((tm, m), lambda i: (i, 0)),
        out_shape=jax.ShapeDtypeStruct((n, m), F32),
        compiler_params=_cparams(("parallel",)), name="mm_res",
    )(a_bf, w_bf, x2)


def _final_norm_body(x_ref, g_ref, o_ref):
    o_ref[...] = _rms(x_ref[...], g_ref[...])


def _final_norm(x2, g, tm):
    n, k = x2.shape
    return pl.pallas_call(
        _final_norm_body, grid=(n // tm,),
        in_specs=[pl.BlockSpec((tm, k), lambda i: (i, 0)), pl.BlockSpec((1, k), lambda i: (0, 0))],
        out_specs=pl.BlockSpec((tm, k), lambda i: (i, 0)),
        out_shape=jax.ShapeDtypeStruct((n, k), F32),
        compiler_params=_cparams(("parallel",)), name="final_norm",
    )(x2, g)


def _mlstm_body(u_ref, sm_ref, smt_ref, bcol_ref, brow_ref, c0_ref, n0_ref, m0_ref,
                y_ref, c_ref, n_ref, m_ref, c_s, n_s, m_s, *, L, t_real, nbb):
    ci = pl.program_id(1)

    @pl.when(ci == 0)
    def _():
        c_s[...] = c0_ref[...]
        n_s[...] = n0_ref[...]
        m_s[...] = m0_ref[...]

    row = lax.broadcasted_iota(I32, (L, L), 0)
    col = lax.broadcasted_iota(I32, (L, L), 1)
    causal = col <= row
    lane = lax.broadcasted_iota(I32, (1, LANES), 1)
    real_col = lax.broadcasted_iota(I32, (L, 1), 0) < t_real
    real_row = lax.broadcasted_iota(I32, (1, L), 1) < t_real
    tri_lo = jnp.where(causal, 1.0, 0.0).astype(BF16)
    tri_up = jnp.where(row <= col, 1.0, 0.0).astype(BF16)
    ones_v = jnp.ones((L, HD), BF16)
    zpad_cn = jnp.zeros((LANES - HD - 1, HD), F32)
    for bb in range(nbb):
        sm = sm_ref[bb] + bcol_ref[...]
        smt = smt_ref[bb] + brow_ref[...]
        m_all = m_s[bb]
        m_next = m_all
        b_cols = _dot3_rhs(tri_lo, jnp.where(real_col, _log_sigmoid(sm), 0.0))
        b_rows = _dot3_lhs(jnp.where(real_row, _log_sigmoid(smt), 0.0), tri_up)
        for h in range(ML_H):
            q = u_ref[bb, :, 64 * h:64 * h + 64]
            k = u_ref[bb, :, 256 + 64 * h:256 + 64 * h + 64] * (HD ** -0.5)
            v = u_ref[bb, :, 512 + 64 * h:512 + 64 * h + 64]
            o = u_ref[bb, :, 768 + 64 * h:768 + 64 * h + 64]
            ig_col = jnp.where(real_col, sm[:, h:h + 1], NEG)
            ig_row = jnp.where(real_row, smt[h:h + 1, :], NEG)
            b_col = b_cols[:, 4 + h:5 + h]
            b_row = b_rows[4 + h:5 + h, :]
            m_prev = m_all[:, h:h + 1]
            log_d = jnp.where(causal, b_col - b_row + ig_row, NEG)
            inter = b_col + m_prev
            m_t = jnp.maximum(inter, jnp.max(log_d, axis=1, keepdims=True))
            w_carry = jnp.exp(inter - m_t)
            qb = q.astype(BF16)
            vb = v.astype(BF16)
            s = _dot_nt(qb, k.astype(BF16)) * jnp.exp(log_d - m_t)
            c_old = c_s[bb, h]
            n_old = n_s[bb, h]
            sv = _dot(s.astype(BF16), jnp.concatenate([vb, ones_v], axis=1))
            cn = jnp.concatenate([c_old, n_old, zpad_cn], axis=0).astype(BF16)
            qc = _dot_nt(qb, cn)
            num = sv[:, 0:HD] + w_carry * qc[:, 0:HD]
            den = sv[:, HD:HD + 1] + w_carry * qc[:, HD:HD + 1]
            hh = num / jnp.maximum(jnp.abs(den), jnp.exp(-m_t))
            y_ref[bb, :, 64 * h:64 * h + 64] = _sigmoid(o) * hh
            b_end = b_col[L - 1:L, :]
            log_w_row = b_end - b_row + ig_row
            m_new = jnp.maximum(b_end + m_prev, jnp.max(log_w_row, axis=1, keepdims=True))
            w_col = jnp.exp(b_end - b_col + ig_col - m_new)
            decay = jnp.exp(b_end + m_prev - m_new)
            c_s[bb, h] = decay * c_old + _dot_tn((v * w_col).astype(BF16), k.astype(BF16))
            n_s[bb, h] = decay * n_old + jnp.sum(k * w_col, axis=0, keepdims=True)
            m_next = jnp.where(lane == h, m_new, m_next)
        m_s[bb] = m_next

    @pl.when(ci == pl.num_programs(1) - 1)
    def _():
        c_ref[...] = c_s[...]
        n_ref[...] = n_s[...]
        m_ref[...] = m_s[...]


def _mlstm(u_ml, u_sm, u_smt, bcol, brow, c0, n0, m0, L, t_real):
    b, t, _ = u_ml.shape
    nc = t // L
    assert t_real == L or nc == 1
    nbb = ML_BATCH_ROWS
    assert b % nbb == 0
    return pl.pallas_call(
        functools.partial(_mlstm_body, L=L, t_real=t_real, nbb=nbb), grid=(b // nbb, nc),
        in_specs=[pl.BlockSpec((nbb, L, 1024), lambda i, c: (i, c, 0)),
                  pl.BlockSpec((nbb, L, 128), lambda i, c: (i, c, 0)),
                  pl.BlockSpec((nbb, 8, L), lambda i, c: (i, 0, c)),
                  pl.BlockSpec((1, 128), lambda i, c: (0, 0)),
                  pl.BlockSpec((8, 1), lambda i, c: (0, 0)),
                  pl.BlockSpec((nbb, 4, 64, 64), lambda i, c: (i, 0, 0, 0)),
                  pl.BlockSpec((nbb, 4, 1, 64), lambda i, c: (i, 0, 0, 0)),
                  pl.BlockSpec((nbb, 1, 128), lambda i, c: (i, 0, 0))],
        out_specs=[pl.BlockSpec((nbb, L, 256), lambda i, c: (i, c, 0)),
                   pl.BlockSpec((nbb, 4, 64, 64), lambda i, c: (i, 0, 0, 0)),
                   pl.BlockSpec((nbb, 4, 1, 64), lambda i, c: (i, 0, 0, 0)),
                   pl.BlockSpec((nbb, 1, 128), lambda i, c: (i, 0, 0))],
        out_shape=[jax.ShapeDtypeStruct((b, t, 256), F32), jax.ShapeDtypeStruct((b, 4, 64, 64), F32),
                   jax.ShapeDtypeStruct((b, 4, 1, 64), F32), jax.ShapeDtypeStruct((b, 1, 128), F32)],
        scratch_shapes=[pltpu.VMEM((nbb, 4, 64, 64), F32), pltpu.VMEM((nbb, 4, 1, 64), F32),
                        pltpu.VMEM((nbb, 1, 128), F32)],
        compiler_params=_cparams(("parallel", "arbitrary")), name="mlstm",
    )(u_ml, u_sm, u_smt, bcol, brow, c0, n0, m0)


def _lru_body(u_ref, cb_ref, h0_ref, cw_ref, cbias_ref, wa_ref, ba_ref, wx_ref, bx_ref, lam_ref,
              y_ref, tail_ref, hl_ref, tail_s, h_s, a_s, hs_s, *, tt, nb):
    i = pl.program_id(0)

    @pl.when(i == 0)
    def _():
        tail_s[...] = cb_ref[...]
        h_s[...] = h0_ref[...]

    x = u_ref[:, :, 0:LRU_W]
    g = u_ref[:, :, LRU_W:2 * LRU_W]
    xe = jnp.concatenate([tail_s[...], x], axis=0)
    xc = cbias_ref[...] + xe[0:tt] * cw_ref[0:1, :]
    for j in range(1, CONV_W):
        xc = xc + xe[j:j + tt] * cw_ref[j:j + 1, :]
    xc2 = xc.reshape(tt * nb, LRU_W)
    xcb = xc2.astype(BF16)
    r = _sigmoid(_dot(xcb, wa_ref[...]) + ba_ref[...])
    ig = _sigmoid(_dot(xcb, wx_ref[...]) + bx_ref[...])
    log_a = -LRU_C * r * _softplus(-lam_ref[...])
    a = jnp.exp(log_a)
    mult = jnp.sqrt(jnp.tanh(-log_a) * (a * a + 1.0))
    a_s[...] = a.reshape(tt, nb, LRU_W)
    hs_s[...] = (mult * (ig * xc2)).reshape(tt, nb, LRU_W)

    def step(t, h):
        h = a_s[t] * h + hs_s[t]
        hs_s[t] = h
        return h

    h_last = lax.fori_loop(0, tt, step, h_s[...], unroll=min(8, tt))
    h_s[...] = h_last
    y_ref[...] = hs_s[...] * _gelu(g)
    tail_s[...] = xe[tt:tt + CONV_W - 1]

    @pl.when(i == pl.num_programs(0) - 1)
    def _():
        tail_ref[...] = tail_s[...]
        hl_ref[...] = h_s[...]


def _lru(u_tm, cb_tm, h0, cw, cbias, wa_bd, ba, wx_bd, bx, lam, tt):
    t, nb, _ = u_tm.shape
    full = lambda a: pl.BlockSpec(a.shape, lambda i: (0,) * a.ndim)
    return pl.pallas_call(
        functools.partial(_lru_body, tt=tt, nb=nb), grid=(t // tt,),
        in_specs=[pl.BlockSpec((tt, nb, 512), lambda i: (i, 0, 0)), full(cb_tm), full(h0), full(cw),
                  full(cbias), full(wa_bd), full(ba), full(wx_bd), full(bx), full(lam)],
        out_specs=[pl.BlockSpec((tt, nb, 256), lambda i: (i, 0, 0)),
                   pl.BlockSpec((3, nb, 256), lambda i: (0, 0, 0)),
                   pl.BlockSpec((nb, 256), lambda i: (0, 0))],
        out_shape=[jax.ShapeDtypeStruct((t, nb, 256), F32), jax.ShapeDtypeStruct((3, nb, 256), F32),
                   jax.ShapeDtypeStruct((nb, 256), F32)],
        scratch_shapes=[pltpu.VMEM((3, nb, 256), F32), pltpu.VMEM((nb, 256), F32),
                        pltpu.VMEM((tt, nb, 256), F32), pltpu.VMEM((tt, nb, 256), F32)],
        compiler_params=_cparams(("arbitrary",)), name="rglru",
    )(u_tm, cb_tm, h0, cw, cbias, wa_bd, ba, wx_bd, bx, lam)


def _compress_rows_body(pt_ref, pool_ref, bdk_ref, bdv_ref, posk_ref, posv_ref, b1k_ref, b1v_ref,
                        w2k_ref, w2v_ref, kc_ref, vc_ref, buf, sem, *, P):
    s = pl.program_id(0)
    ns = pl.num_programs(0)
    nblk = P * (PAGE // CMP_BLK)

    def page_copy(step, slot, p, c):
        return pltpu.make_async_copy(
            pool_ref.at[pt_ref[step * P + p], :, pl.ds(LANES * c, LANES)],
            buf.at[slot, c, pl.ds(p * PAGE, PAGE), :], sem.at[slot])

    def start_all(step, slot):
        def body(p, carry):
            page_copy(step, slot, p, 0).start()
            page_copy(step, slot, p, 1).start()
            return carry
        lax.fori_loop(0, P, body, 0)

    def wait_all(step, slot):
        def body(p, carry):
            page_copy(step, slot, p, 0).wait()
            page_copy(step, slot, p, 1).wait()
            return carry
        lax.fori_loop(0, P, body, 0)

    @pl.when(s == 0)
    def _():
        start_all(0, 0)

    slot = lax.rem(s, 2)

    @pl.when(s + 1 < ns)
    def _():
        start_all(s + 1, 1 - slot)

    wait_all(s, slot)

    acc_k = jnp.zeros((nblk, 256), F32)
    acc_v = jnp.zeros((nblk, 256), F32)
    for r in range(CMP_BLK):
        rows = pl.ds(r, nblk, stride=CMP_BLK)
        xk = (buf[slot, 0, rows, :] + posk_ref[r:r + 1, :]).astype(BF16)
        xv = (buf[slot, 1, rows, :] + posv_ref[r:r + 1, :]).astype(BF16)
        acc_k = acc_k + _dot(xk, bdk_ref[r])
        acc_v = acc_v + _dot(xv, bdv_ref[r])
    kc_ref[...] = _dot(_gelu(acc_k + b1k_ref[...]).astype(BF16), w2k_ref[...])
    vc_ref[...] = _dot(_gelu(acc_v + b1v_ref[...]).astype(BF16), w2v_ref[...])


def _compress_rows(pt_flat, pool3, cw, P):
    n_pages = pt_flat.shape[0]
    nblk = P * (PAGE // CMP_BLK)
    full = lambda a: pl.BlockSpec(a.shape, lambda i, pt: (0,) * a.ndim)
    ws = (cw["bdk"], cw["bdv"], cw["posk"], cw["posv"], cw["b1k"], cw["b1v"], cw["w2k"], cw["w2v"])
    grid_spec = pltpu.PrefetchScalarGridSpec(
        num_scalar_prefetch=1, grid=(n_pages // P,),
        in_specs=[pl.BlockSpec(memory_space=pl.ANY)] + [full(a) for a in ws],
        out_specs=[pl.BlockSpec((nblk, 128), lambda i, pt: (i, 0)),
                   pl.BlockSpec((nblk, 128), lambda i, pt: (i, 0))],
        scratch_shapes=[pltpu.VMEM((2, 2, P * PAGE, LANES), F32), pltpu.SemaphoreType.DMA((2,))])
    return pl.pallas_call(
        functools.partial(_compress_rows_body, P=P), grid_spec=grid_spec,
        out_shape=[jax.ShapeDtypeStruct((n_pages * 4, 128), F32)] * 2,
        compiler_params=_cparams(("arbitrary",)), name="compress_rows",
    )(pt_flat, pool3, *ws)


def _compress_pages_body(pt_ref, pool_ref, w_ref, pos_ref, b1_ref, w2_ref, o_ref, buf, sem, *, P, nstep):
    c = pl.program_id(0)
    s = pl.program_id(1)
    lin = c * nstep + s
    PG = NSA_G * P

    def tile_copy(cc, step, slot, p):
        return pltpu.make_async_copy(
            pool_ref.at[pl.ds((pt_ref[step * P + p] * 4 + cc) * NSA_G, NSA_G)],
            buf.at[slot, :, :, p, :], sem.at[slot])

    def start_all(cc, step, slot):
        def body(p, carry):
            tile_copy(cc, step, slot, p).start()
            return carry
        lax.fori_loop(0, P, body, 0, unroll=4)

    def wait_all(slot):
        pltpu.make_async_copy(buf.at[slot], buf.at[slot], sem.at[slot]).wait()

    @pl.when(lin == 0)
    def _():
        start_all(0, 0, 0)

    slot = lax.rem(lin, 2)
    nxt = lin + 1

    @pl.when(nxt < 2 * nstep)
    def _():
        start_all(lax.div(nxt, nstep), lax.rem(nxt, nstep), 1 - slot)

    wait_all(slot)

    acc = jnp.zeros((PG, 4 * CMP_HID), F32)
    for dp in range(HD // 2):
        chan = [jnp.concatenate([buf[slot, 0, d], buf[slot, 1, d]], axis=0) for d in (2 * dp, 2 * dp + 1)]
        lhs = jnp.concatenate(chan, axis=1) + pos_ref[dp:dp + 1, :]
        acc = acc + _dot(lhs.astype(BF16), w_ref[dp])
    h = _gelu(acc + b1_ref[...])
    out = _dot(h.astype(BF16), w2_ref[...])
    for blk in range(PAGE // CMP_BLK):
        cols = slice(HD * blk, HD * blk + HD)
        o_ref[blk] = jnp.concatenate([out[0:P, cols], out[P:2 * P, cols]], axis=1)


def _compress_pages(pt_flat, pool_t, cw, P):
    n_pages = pt_flat.shape[0]
    nstep = n_pages // P
    PG = NSA_G * P
    grid_spec = pltpu.PrefetchScalarGridSpec(
        num_scalar_prefetch=1, grid=(2, nstep),
        in_specs=[pl.BlockSpec(memory_space=pl.ANY),
                  pl.BlockSpec((None, HD // 2, 256, 512), lambda c, s, pt: (c, 0, 0, 0)),
                  pl.BlockSpec((None, HD // 2, 256), lambda c, s, pt: (c, 0, 0)),
                  pl.BlockSpec((None, 1, 512), lambda c, s, pt: (c, 0, 0)),
                  pl.BlockSpec((None, 512, 256), lambda c, s, pt: (c, 0, 0))],
        out_specs=pl.BlockSpec((None, PAGE // CMP_BLK, P, LANES), lambda c, s, pt: (c, 0, s, 0)),
        scratch_shapes=[pltpu.VMEM((2, NSA_G, HD, P, LANES), F32), pltpu.SemaphoreType.DMA((2,))])
    return pl.pallas_call(
        functools.partial(_compress_pages_body, P=P, nstep=nstep), grid_spec=grid_spec,
        out_shape=jax.ShapeDtypeStruct((2, PAGE // CMP_BLK, n_pages, LANES), F32),
        compiler_params=_cparams(("arbitrary", "arbitrary")), name="compress_pages",
    )(pt_flat, pool_t, cw["pw"], cw["ppos"], cw["pb1"], cw["pw2"])


def _half_mask(g):
    lane = lax.broadcasted_iota(I32, (1, LANES), 1)
    return (lane >= 64 * g) & (lane < 64 * g + 64)


def _gate_expand(gates, gexp_ref):
    hi, lo = _split2(gates)
    return [_dot(hi, gexp_ref[c]) + _dot(lo, gexp_ref[c]) for c in range(3)]


def _nsa_prompt_body(q_ref, sm_ref, kc_ref, vc_ref, kvs_ref, kaug_ref, win_ref, gexp_ref, o_ref,
                     s_scr, mx_scr, acc_scr, *, T):
    QT = 128
    M = NSA_H * QT
    CK = KEY_CHUNK
    q0 = pl.program_id(1) * QT
    lane = lax.broadcasted_iota(I32, (1, LANES), 1)
    left = lane < 64
    rowm = lax.broadcasted_iota(I32, (M, 1), 0)
    qposm = q0 + (rowm & (QT - 1))
    slopem = jnp.zeros((M, 1), F32)
    for h in range(NSA_H):
        slopem = jnp.where((rowm >= QT * h) & (rowm < QT * (h + 1)), SLOPES[h], slopem)

    nce = T // CMP_BLK // 2
    blk = jnp.where(lane < nce, 2 * lane, 2 * (lane - nce) + 1)
    c_end = jnp.where(lane < 2 * nce, blk * CMP_BLK + (CMP_BLK - 1), 1 << 30)
    dist_c = qposm - c_end
    mask_c = dist_c >= 0
    zpad = jnp.zeros((LANES - 2 * nce, LANES), F32)
    kc = jnp.concatenate([kc_ref[...], zpad], axis=0).astype(BF16)
    vc = jnp.concatenate([vc_ref[...], zpad], axis=0).astype(BF16)

    qs = [jnp.where(_half_mask(g), q_ref[:, 128 * r:128 * r + 128], 0)
          for g in range(NSA_G) for r in range(NSA_R)]
    qm = jnp.concatenate(qs, axis=0)

    s = _dot_nt(qm, kc) * (HD ** -0.5) - slopem * dist_c.astype(F32)
    p = _masked_softmax(s, mask_c)
    o_cmp = _dot(p.astype(BF16), vc)

    ns = T // SLC_BLK
    jrow = lax.broadcasted_iota(I32, (ns, 1), 0)
    cur = (q0 + lane) // SLC_BLK
    in_sel = (lane >= AUG_SEL) & (lane < AUG_SEL + ns)
    valid = jrow <= cur
    forced = (jrow == 0) | (jrow == cur) | (jrow == cur - 1)
    sel_bias = []
    for g in range(NSA_G):
        b0 = g * NSA_R * QT
        imp = p[b0:b0 + QT] + p[b0 + QT:b0 + 2 * QT] + p[b0 + 2 * QT:b0 + 3 * QT] + p[b0 + 3 * QT:b0 + 4 * QT]
        imp_t = imp.T
        pooled = imp_t[0:nce] + imp_t[nce:2 * nce]
        score = jnp.where(valid, pooled + FORCE_BONUS * forced.astype(F32), NEG)
        cnt = jnp.zeros((ns, LANES), F32)
        for i2 in range(ns):
            si = score[i2:i2 + 1, :]
            beats = (si > score) | ((si == score) & (i2 < jrow))
            cnt = cnt + beats.astype(F32)
        sel_t = ((cnt < N_SEL) & (score > 0.5 * NEG)).astype(F32)
        selp = jnp.concatenate([jnp.zeros((AUG_SEL, LANES), F32), sel_t,
                                jnp.zeros((LANES - AUG_SEL - ns, LANES), F32)], axis=0).T
        sel_bias.append(jnp.where(in_sel, (selp - 1.0) * (-NEG), 0.0))

    shift = -(q0 + QT - 1).astype(F32)

    def aug(h, base):
        sl = SLOPES[h]
        a = jnp.where(lane == AUG_HI, sl * SLC_BLK, jnp.where(lane == AUG_LO, sl,
                      jnp.where(lane == AUG_ONE, sl * shift, base)))
        return a.astype(BF16)

    qsc = [(qs[h].astype(F32) * (HD ** -0.5)).astype(BF16) for h in range(NSA_H)]
    zero_t = jnp.zeros((QT, LANES), F32)
    qsel = jnp.concatenate([jnp.concatenate([qsc[h], aug(h, sel_bias[h // NSA_R])], axis=1)
                            for h in range(NSA_H)], axis=0)
    qwin = jnp.concatenate([jnp.concatenate([qsc[h], aug(h, zero_t)], axis=1) for h in range(NSA_H)], axis=0)
    ones_k = jnp.ones((CK, LANES), BF16)

    def key_chunk(c):
        r0 = pl.multiple_of(c * CK, CK)
        return jnp.concatenate([kvs_ref[pl.ds(r0, CK), 0:128], kaug_ref[pl.ds(r0, CK), :]], axis=1)

    def fold(sc):
        out = sc[:, 0:LANES]
        for t in range(1, CK // LANES):
            out = jnp.maximum(out, sc[:, LANES * t:LANES * (t + 1)])
        return out

    nk = q0 // CK + 1
    mx_scr[...] = jnp.full((M, LANES), NEG, F32)

    def body1(c, carry):
        sc = _dot_nt(qsel, key_chunk(c))
        s_scr[c] = sc
        mx_scr[...] = jnp.maximum(mx_scr[...], fold(sc))
        return carry

    lax.fori_loop(0, nk - 1, body1, 0)
    cl = nk - 1
    kpos_l = cl * CK + lax.broadcasted_iota(I32, (1, CK), 1)
    sc = jnp.where(kpos_l <= qposm, _dot_nt(qsel, key_chunk(cl)), NEG)
    s_scr[cl] = sc
    m = jnp.max(jnp.maximum(mx_scr[...], fold(sc)), axis=1, keepdims=True)

    acc_scr[...] = jnp.zeros((M, 2 * LANES), F32)

    def body2(c, carry):
        e = jnp.exp(s_scr[c] - m).astype(BF16)
        r0 = pl.multiple_of(c * CK, CK)
        vo = jnp.concatenate([kvs_ref[pl.ds(r0, CK), 128:256], ones_k], axis=1)
        acc_scr[...] += _dot(e, vo)
        return carry

    lax.fori_loop(0, nk, body2, 0)
    acc = acc_scr[...]
    o_slc = acc[:, 0:128] / acc[:, 128:129]

    w0 = pl.multiple_of(jnp.maximum(q0 - WINDOW, 0), QT)
    WK = WINDOW + QT
    dist_w = qposm - (w0 + lax.broadcasted_iota(I32, (1, WK), 1))
    mask_w = (dist_w >= 0) & (dist_w < WINDOW)
    kwin = jnp.concatenate([win_ref[pl.ds(w0, WK), 0:128], kaug_ref[pl.ds(w0, WK), :]], axis=1)
    sw = jnp.where(mask_w, _dot_nt(qwin, kwin), NEG)
    e = jnp.exp(sw - jnp.max(sw, axis=1, keepdims=True)).astype(BF16)
    vo = jnp.concatenate([win_ref[pl.ds(w0, WK), 128:256], jnp.ones((WK, LANES), BF16)], axis=1)
    accw = _dot(e, vo)
    o_win = accw[:, 0:128] / accw[:, 128:129]

    gates = _sigmoid(sm_ref[...])
    gc, gs, gw = _gate_expand(gates, gexp_ref)
    for r in range(NSA_R):
        sl = slice(128 * r, 128 * r + 128)
        r0 = slice(QT * r, QT * (r + 1))
        r1 = slice(QT * (NSA_R + r), QT * (NSA_R + r + 1))
        oc = jnp.where(left, o_cmp[r0], o_cmp[r1])
        os_ = jnp.where(left, o_slc[r0], o_slc[r1])
        ow = jnp.where(left, o_win[r0], o_win[r1])
        o_ref[:, sl] = gc[:, sl] * oc + gs[:, sl] * os_ + gw[:, sl] * ow


def _nsa_prompt(q, sm, kc_eo, vc_eo, kvs_bf, win_bf, kaug, gexp):
    b, t, _ = q.shape
    return pl.pallas_call(
        functools.partial(_nsa_prompt_body, T=t), grid=(b, t // 128),
        in_specs=[pl.BlockSpec((None, 128, 512), lambda i, j: (i, j, 0)),
                  pl.BlockSpec((None, 128, 128), lambda i, j: (i, j, 0)),
                  pl.BlockSpec((None, t // CMP_BLK, 128), lambda i, j: (i, 0, 0)),
                  pl.BlockSpec((None, t // CMP_BLK, 128), lambda i, j: (i, 0, 0)),
                  pl.BlockSpec((None, t, 256), lambda i, j: (i, 0, 0)),
                  pl.BlockSpec(kaug.shape, lambda i, j: (0, 0)),
                  pl.BlockSpec((None, t, 256), lambda i, j: (i, 0, 0)),
                  pl.BlockSpec(gexp.shape, lambda i, j: (0, 0, 0))],
        out_specs=pl.BlockSpec((None, 128, 512), lambda i, j: (i, j, 0)),
        out_shape=jax.ShapeDtypeStruct((b, t, 512), F32),
        scratch_shapes=[pltpu.VMEM((t // KEY_CHUNK, NSA_H * 128, KEY_CHUNK), F32),
                        pltpu.VMEM((NSA_H * 128, LANES), F32), pltpu.VMEM((NSA_H * 128, 2 * LANES), F32)],
        compiler_params=_cparams(("parallel", "arbitrary")), name="nsa_prompt",
    )(q, sm, kc_eo, vc_eo, kvs_bf, kaug, win_bf, gexp)


def _sel_block_of_lane(lane, n_pages):
    return jnp.where(lane < n_pages, 2 * lane,
                     jnp.where(lane < 2 * n_pages, 2 * (lane - n_pages) + 1,
                               jnp.where(lane == 2 * n_pages, 2 * n_pages, 1 << 20)))


def _nsa_sample_sel_body(q_ref, kc_ref, vc_ref, wbuf_ref, wnew_ref, oc_ref, ow_ref, sc_ref, *, past, n_pages):
    R8 = 8
    per_page = PAGE // CMP_BLK
    ncmp = per_page * n_pages
    qpos = past + lax.broadcasted_iota(I32, (R8, 1), 0)
    lane_c = lax.broadcasted_iota(I32, (1, ncmp), 1)
    blk = (lane_c % n_pages) * per_page + lane_c // n_pages
    dist_c = qpos - (blk * CMP_BLK + (CMP_BLK - 1))
    mask_c = dist_c >= 0
    dist_cf = dist_c.astype(F32)
    kc = kc_ref[...].reshape(ncmp, LANES).astype(BF16)
    vc = vc_ref[...].reshape(ncmp, LANES).astype(BF16)

    wb = wbuf_ref.shape[3]
    wk = wbuf_ref[0].reshape(NSA_G * HD, wb).astype(BF16)
    wv = wbuf_ref[1].reshape(NSA_G * HD, wb).astype(BF16)
    nk = wnew_ref[:, 0:128].astype(BF16)
    nv = wnew_ref[:, 128:256].astype(BF16)
    dist_w1 = qpos - (past - wb + lax.broadcasted_iota(I32, (1, wb), 1))
    mask_w1 = (dist_w1 >= 0) & (dist_w1 < WINDOW)
    dist_w2 = qpos - (past + lax.broadcasted_iota(I32, (1, R8), 1))
    mask_w2 = (dist_w2 >= 0) & (dist_w2 < WINDOW)

    ns_l = 3 * LANES
    assert 2 * n_pages + 1 <= ns_l
    lane_s = lax.broadcasted_iota(I32, (1, ns_l), 1)
    jmap = _sel_block_of_lane(lane_s, n_pages)
    cur = qpos // SLC_BLK
    lane16 = lax.broadcasted_iota(I32, (1, LANES), 1)
    left = lane16 < 64

    oc = [[None] * NSA_R for _ in range(NSA_G)]
    ow = [[None] * NSA_R for _ in range(NSA_G)]
    for g in range(NSA_G):
        hm = _half_mask(g)
        imp = jnp.zeros((R8, ncmp), F32)
        for r in range(NSA_R):
            slope = SLOPES[g * NSA_R + r]
            qh = jnp.where(hm, q_ref[:, 128 * r:128 * r + 128], 0).astype(BF16)
            s = _dot_nt(qh, kc) * (HD ** -0.5) - slope * dist_cf
            p = _masked_softmax(s, mask_c)
            imp = imp + p
            oc[g][r] = _dot(p.astype(BF16), vc)
            s1 = jnp.where(mask_w1, _dot(qh, wk) * (HD ** -0.5) - slope * dist_w1.astype(F32), NEG)
            s2 = jnp.where(mask_w2, _dot_nt(qh, nk) * (HD ** -0.5) - slope * dist_w2.astype(F32), NEG)
            mx = jnp.maximum(jnp.max(s1, axis=1, keepdims=True), jnp.max(s2, axis=1, keepdims=True))
            e1 = jnp.exp(s1 - mx)
            e2 = jnp.exp(s2 - mx)
            den = jnp.sum(e1, axis=1, keepdims=True) + jnp.sum(e2, axis=1, keepdims=True)
            p1 = jnp.where(mask_w1, e1 / den, 0.0)
            p2 = jnp.where(mask_w2, e2 / den, 0.0)
            ow[g][r] = _dot_nt(p1.astype(BF16), wv) + _dot(p2.astype(BF16), nv)
        np_ = n_pages
        pooled = jnp.concatenate([imp[:, 0:np_] + imp[:, np_:2 * np_], imp[:, 2 * np_:3 * np_] + imp[:, 3 * np_:],
                                  jnp.zeros((R8, ns_l - 2 * np_), F32)], axis=1)
        valid = jmap <= cur
        forced = (jmap == 0) | (jmap == cur) | (jmap == cur - 1)
        score = jnp.where(valid, pooled + FORCE_BONUS * forced.astype(F32), NEG)
        sc_ref[g] = jnp.where(lane_s <= 2 * np_, score, -jnp.inf)
    for r in range(NSA_R):
        sl = slice(128 * r, 128 * r + 128)
        oc_ref[:, sl] = jnp.where(left, oc[0][r], oc[1][r])
        ow_ref[:, sl] = jnp.where(left, ow[0][r], ow[1][r])


def _nsa_sample_sel(q8, kvc, win_t, layer, wnew8, past):
    b = q8.shape[0]
    n_pages = past // PAGE
    per_page = PAGE // CMP_BLK
    wb = win_t.shape[-1]
    return pl.pallas_call(
        functools.partial(_nsa_sample_sel_body, past=past, n_pages=n_pages), grid=(b,),
        in_specs=[pl.BlockSpec((None, 8, 512), lambda i: (i, 0, 0)),
                  pl.BlockSpec((None, per_page, n_pages, 128), lambda i: (0, 0, i, 0)),
                  pl.BlockSpec((None, per_page, n_pages, 128), lambda i: (1, 0, i, 0)),
                  pl.BlockSpec((None, None, 2, NSA_G, HD, wb), lambda i: (layer, i, 0, 0, 0, 0)),
                  pl.BlockSpec((None, 8, 256), lambda i: (i, 0, 0))],
        out_specs=[pl.BlockSpec((None, 8, 512), lambda i: (i, 0, 0)),
                   pl.BlockSpec((None, 8, 512), lambda i: (i, 0, 0)),
                   pl.BlockSpec((None, 2, 8, 3 * LANES), lambda i: (i, 0, 0, 0))],
        out_shape=[jax.ShapeDtypeStruct((b, 8, 512), F32), jax.ShapeDtypeStruct((b, 8, 512), F32),
                   jax.ShapeDtypeStruct((b, 2, 8, 3 * LANES), F32)],
        compiler_params=_cparams(("parallel",)), name="nsa_sample_sel",
    )(q8, kvc, kvc, win_t, wnew8)


def _topk_body(sc_ref, idx_ref, *, n_pages):
    score = sc_ref[...]
    rows, width = score.shape
    blk_f = _sel_block_of_lane(lax.broadcasted_iota(I32, (1, width), 1), n_pages).astype(F32)
    lane_o = lax.broadcasted_iota(I32, (1, LANES), 1)
    idx_acc = jnp.zeros((rows, LANES), F32)
    for n in range(N_SEL):
        mx = jnp.max(score, axis=1, keepdims=True)
        pick = jnp.min(jnp.where(score == mx, blk_f, 2e6), axis=1, keepdims=True)
        idx_acc = jnp.where(lane_o == n, jnp.where(mx > 0.5 * NEG, pick, -1.0), idx_acc)
        score = jnp.where(blk_f == pick, -jnp.inf, score)
    idx_ref[...] = idx_acc.astype(I32)


def _topk(score2, n_pages):
    rows, width = score2.shape
    return pl.pallas_call(
        functools.partial(_topk_body, n_pages=n_pages), grid=(1,),
        in_specs=[pl.BlockSpec((rows, width), lambda i: (0, 0))],
        out_specs=pl.BlockSpec((rows, LANES), lambda i: (0, 0)),
        out_shape=jax.ShapeDtypeStruct((rows, LANES), I32),
        compiler_params=_cparams(("arbitrary",)), name="topk",
    )(score2)


def _nsa_sample_slc_body(idx_ref, tile_ref, pool_ref, q_ref, knew_ref, sm_ref, oc_ref, ow_ref, gexp_ref,
                         o_ref, kbuf, vbuf, sem, *, T, past):
    b = pl.program_id(0)
    nb = pl.num_programs(0)
    ns_past = past // SLC_BLK
    per_page = PAGE // SLC_BLK
    nslot = NSA_G * T * N_SEL

    def start_all(bb, buf_slot):
        def body(i, carry):
            kt = tile_ref[bb * nslot + i]
            pltpu.make_async_copy(pool_ref.at[kt], kbuf.at[buf_slot, i], sem.at[0, buf_slot]).start()
            pltpu.make_async_copy(pool_ref.at[kt + NSA_G], vbuf.at[buf_slot, i], sem.at[1, buf_slot]).start()
            return carry
        lax.fori_loop(0, nslot, body, 0, unroll=4)

    @pl.when(b == 0)
    def _():
        start_all(0, 0)

    cur_slot = lax.rem(b, 2)

    @pl.when(b + 1 < nb)
    def _():
        start_all(b + 1, 1 - cur_slot)

    pltpu.make_async_copy(kbuf.at[cur_slot], kbuf.at[cur_slot], sem.at[0, cur_slot]).wait()
    pltpu.make_async_copy(vbuf.at[cur_slot], vbuf.at[cur_slot], sem.at[1, cur_slot]).wait()

    R8 = 8
    NK = N_SEL * PAGE
    lane_k = lax.broadcasted_iota(I32, (1, NK), 1)
    slot_k = lane_k // PAGE
    within = lane_k - slot_k * PAGE
    half_k = within // SLC_BLK
    off_k = within - half_k * SLC_BLK
    rowi = lax.broadcasted_iota(I32, (R8, 1), 0)
    knew = knew_ref[:, 0:128].astype(BF16)
    vnew = knew_ref[:, 128:256].astype(BF16)
    kpos_new = past + lax.broadcasted_iota(I32, (1, R8), 1)
    zhalf = jnp.zeros((R8, HD), F32)
    o_acc = [jnp.zeros((R8, LANES), F32) for _ in range(NSA_R)]
    for g in range(NSA_G):
        hm = _half_mask(g)
        slope = jnp.zeros((R8, 1), F32)
        for r in range(NSA_R):
            slope = jnp.where(rowi == r, SLOPES[g * NSA_R + r], slope)
        for t in range(T):
            qpos = past + t
            qm = jnp.zeros((R8, LANES), F32)
            for r in range(NSA_R):
                qm = jnp.where(rowi == r, q_ref[t:t + 1, 128 * r:128 * r + 128], qm)
            qm = jnp.where(hm, qm, 0.0)
            qc = (qm[:, 0:HD] + qm[:, HD:2 * HD]).astype(BF16)
            base = (g * T + t) * N_SEL
            kt = jnp.concatenate([kbuf[cur_slot, base + n] for n in range(N_SEL)], axis=1).astype(BF16)
            vt = jnp.concatenate([vbuf[cur_slot, base + n] for n in range(N_SEL)], axis=1).astype(BF16)
            kpos = off_k
            ok_i = jnp.zeros((1, NK), I32)
            has_new = jnp.int32(0)
            for n in range(N_SEL):
                j = idx_ref[((b * NSA_G + g) * 8 + t) * N_SEL + n]
                in_blk = (slot_k == n) & (half_k == lax.rem(j, per_page))
                kpos = jnp.where(in_blk, j * SLC_BLK + off_k, kpos)
                ok_i = jnp.where(in_blk, ((j >= 0) & (j < ns_past)).astype(I32), ok_i)
                has_new = has_new + (j == ns_past).astype(I32)
            dist1 = qpos - kpos
            mask1 = (ok_i > 0) & (dist1 >= 0)
            dist2 = qpos - kpos_new
            mask2 = (dist2 >= 0) & ((jnp.zeros((1, R8), I32) + has_new) > 0)
            s1 = jnp.where(mask1, _dot(qc, kt) * (HD ** -0.5) - slope * dist1.astype(F32), NEG)
            s2 = jnp.where(mask2, _dot_nt(qm.astype(BF16), knew) * (HD ** -0.5) - slope * dist2.astype(F32), NEG)
            mx = jnp.maximum(jnp.max(s1, axis=1, keepdims=True), jnp.max(s2, axis=1, keepdims=True))
            e1 = jnp.exp(s1 - mx)
            e2 = jnp.exp(s2 - mx)
            den = jnp.sum(e1, axis=1, keepdims=True) + jnp.sum(e2, axis=1, keepdims=True)
            p1 = jnp.where(mask1, e1 / den, 0.0)
            p2 = jnp.where(mask2, e2 / den, 0.0)
            o1 = _dot_nt(p1.astype(BF16), vt)
            o1 = jnp.concatenate([o1, zhalf] if g == 0 else [zhalf, o1], axis=1)
            o = o1 + jnp.where(hm, _dot(p2.astype(BF16), vnew), 0.0)
            for r in range(NSA_R):
                o_acc[r] = jnp.where((rowi == t) & hm, o[r:r + 1, :], o_acc[r])
    gates = _sigmoid(sm_ref[...])
    gc, gs, gw = _gate_expand(gates, gexp_ref)
    for r in range(NSA_R):
        sl = slice(128 * r, 128 * r + 128)
        o_ref[:, sl] = gc[:, sl] * oc_ref[:, sl] + gs[:, sl] * o_acc[r] + gw[:, sl] * ow_ref[:, sl]


def _nsa_sample_slc(idx, pt_flat, pool_t, q8, kvnew8, sm8, oc, ow, gexp, T, past):
    b = q8.shape[0]
    n_pages = pt_flat.shape[0] // b
    row = lambda w: pl.BlockSpec((None, 8, w), lambda i, a, c: (i, 0, 0))
    nslot = NSA_G * T * N_SEL
    page = jnp.clip(idx[:, :, 0:T], 0, past // SLC_BLK - 1) // (PAGE // SLC_BLK)
    phys = jnp.take_along_axis(pt_flat.reshape(b, 1, 1, n_pages), page.reshape(b, 1, 1, -1), axis=3)
    grp = jnp.arange(NSA_G, dtype=I32).reshape(1, NSA_G, 1, 1)
    tiles = ((phys.reshape(page.shape) * 4 + 2) * NSA_G + grp).reshape(-1)
    idx_flat = idx.reshape(-1)
    grid_spec = pltpu.PrefetchScalarGridSpec(
        num_scalar_prefetch=2, grid=(b,),
        in_specs=[pl.BlockSpec(memory_space=pl.ANY), row(512), row(256), row(128), row(512), row(512),
                  pl.BlockSpec(gexp.shape, lambda i, a, c: (0, 0, 0))],
        out_specs=row(512),
        scratch_shapes=[pltpu.VMEM((2, nslot, HD, PAGE), F32), pltpu.VMEM((2, nslot, HD, PAGE), F32),
                        pltpu.SemaphoreType.DMA((2, 2))])
    return pl.pallas_call(
        functools.partial(_nsa_sample_slc_body, T=T, past=past), grid_spec=grid_spec,
        out_shape=jax.ShapeDtypeStruct((b, 8, 512), F32),
        compiler_params=_cparams(("arbitrary",)), name="nsa_sample_slc",
    )(idx_flat, tiles, pool_t, q8, kvnew8, sm8, oc, ow, gexp)


def _mix_out_body(ya_ref, yb_ref, yc_ref, x_ref, g_ref, w_ref, pool_ref, poolt_ref, o_ref):
    y = jnp.concatenate([ya_ref[...], yb_ref[...], yc_ref[...]], axis=-1)
    ms = _dot_split(y * y, pool_ref[...])
    rb = _dot_split(lax.rsqrt(ms + EPS), poolt_ref[...])
    yn = (y * rb * g_ref[...]).astype(BF16)
    o_ref[...] = x_ref[...] + _dot(yn, w_ref[...])


def _mix_out(ya, yb, yc, x2, g, w_bf, pool, poolt, tm):
    n = x2.shape[0]
    row = lambda w: pl.BlockSpec((tm, w), lambda i: (i, 0))
    full = lambda a: pl.BlockSpec(a.shape, lambda i: (0,) * a.ndim)
    return pl.pallas_call(
        _mix_out_body, grid=(n // tm,),
        in_specs=[row(256), row(256), row(512), row(1024), full(g), full(w_bf), full(pool), full(poolt)],
        out_specs=row(1024), out_shape=jax.ShapeDtypeStruct((n, 1024), F32),
        compiler_params=_cparams(("parallel",)), name="mix_out",
    )(ya, yb, yc, x2, g, w_bf, pool, poolt)


def _xattn_body(q_ref, kv_ref, o_ref):
    for h in range(XA_H):
        if len(kv_ref.shape) == 2:
            k = kv_ref[:, 128 * h:128 * h + 128].astype(BF16)
            v = kv_ref[:, XA_INNER + 128 * h:XA_INNER + 128 * h + 128].astype(BF16)
        else:
            k = kv_ref[:, 0, h, :].astype(BF16)
            v = kv_ref[:, 1, h, :].astype(BF16)
        s = _dot_nt(q_ref[:, 128 * h:128 * h + 128], k) * (XA_HD ** -0.5)
        e = jnp.exp(s - jnp.max(s, axis=-1, keepdims=True))
        a = e / jnp.sum(e, axis=-1, keepdims=True)
        o_ref[:, 128 * h:128 * h + 128] = _dot(a.astype(BF16), v).astype(BF16)


def _xattn(q, mkv, tq, layer=None):
    b, t, _ = q.shape
    if layer is None:
        kv_spec = pl.BlockSpec((None, N_MEM, 1024), lambda i, j: (i, 0, 0))
    else:
        kv_spec = pl.BlockSpec((None, None, N_MEM, 2, XA_H, XA_HD), lambda i, j: (layer, i, 0, 0, 0, 0))
    return pl.pallas_call(
        _xattn_body, grid=(b, t // tq),
        in_specs=[pl.BlockSpec((None, tq, 512), lambda i, j: (i, j, 0)), kv_spec],
        out_specs=pl.BlockSpec((None, tq, 512), lambda i, j: (i, j, 0)),
        out_shape=jax.ShapeDtypeStruct((b, t, 512), BF16),
        compiler_params=_cparams(("parallel", "arbitrary")), name="xattn",
    )(q, mkv)


def _moe_body(x_ref, g_ref, rwh_ref, rwl_ref, rb_ref, wg_ref, wu_ref, wd_ref, o_ref, zn_s, comb_s, acc_s):
    e = pl.program_id(1)
    lane = lax.broadcasted_iota(I32, (1, LANES), 1)

    @pl.when(e == 0)
    def _():
        z = _rms(x_ref[...], g_ref[...])
        zh, zl = _split2(z)
        zn_s[...] = zh
        logits = _dot(zh, rwh_ref[...]) + _dot(zh, rwl_ref[...]) + _dot(zl, rwh_ref[...]) + rb_ref[...]
        is_g = (lane >= N_EXP) & (lane < N_EXP + N_GROUPS)
        gl = jnp.where(is_g, logits, -jnp.inf)
        gmax = jnp.max(gl, axis=-1, keepdims=True)
        gw = 1.0 / jnp.sum(jnp.exp(gl - gmax), axis=-1, keepdims=True)
        lanef = lane.astype(F32)
        grpf = (lane // EXP_PER_GROUP).astype(F32)
        gsel = jnp.min(jnp.where(gl == gmax, lanef, 1e6), axis=-1, keepdims=True) - N_EXP
        in_grp = (lane < N_EXP) & (grpf == gsel)
        le = jnp.where(in_grp, logits, -jnp.inf)
        v1 = jnp.max(le, axis=-1, keepdims=True)
        i1 = jnp.min(jnp.where(le == v1, lanef, 1e6), axis=-1, keepdims=True)
        le2 = jnp.where(lanef == i1, -jnp.inf, le)
        v2 = jnp.max(le2, axis=-1, keepdims=True)
        i2 = jnp.min(jnp.where(le2 == v2, lanef, 1e6), axis=-1, keepdims=True)
        e2 = jnp.exp(v2 - v1)
        w1 = 1.0 / (1.0 + e2)
        w2 = e2 / (1.0 + e2)
        comb_s[...] = gw * (jnp.where(lanef == i1, w1, 0.0) + jnp.where(lanef == i2, w2, 0.0))
        acc_s[...] = jnp.zeros_like(acc_s)

    zn = zn_s[...]
    hg = _dot(zn, wg_ref[...])
    hu = _dot(zn, wu_ref[...])
    ce = jnp.sum(jnp.where(lane == e, comb_s[...], 0.0), axis=-1, keepdims=True)
    h = (hg * _sigmoid(hg)) * hu * ce
    acc_s[...] += _dot(h.astype(BF16), wd_ref[...])

    @pl.when(e == pl.num_programs(1) - 1)
    def _():
        o_ref[...] = x_ref[...] + acc_s[...]


def _moe(x2, g, rwh, rwl, rb, wg, wu, wd, layer, tm):
    n = x2.shape[0]
    full = lambda a: pl.BlockSpec(a.shape, lambda i, e: (0,) * a.ndim)
    return pl.pallas_call(
        _moe_body, grid=(n // tm, N_EXP),
        in_specs=[pl.BlockSpec((tm, 1024), lambda i, e: (i, 0)), full(g), full(rwh), full(rwl), full(rb),
                  pl.BlockSpec((None, None, 1024, EXP_FF), lambda i, e: (layer, e, 0, 0)),
                  pl.BlockSpec((None, None, 1024, EXP_FF), lambda i, e: (layer, e, 0, 0)),
                  pl.BlockSpec((None, None, EXP_FF, 1024), lambda i, e: (layer, e, 0, 0))],
        out_specs=pl.BlockSpec((tm, 1024), lambda i, e: (i, 0)),
        out_shape=jax.ShapeDtypeStruct((n, 1024), F32),
        scratch_shapes=[pltpu.VMEM((tm, 1024), BF16), pltpu.VMEM((tm, 128), F32), pltpu.VMEM((tm, 1024), F32)],
        compiler_params=_cparams(("parallel", "arbitrary")), name="moe",
    )(x2, g, rwh, rwl, rb, wg, wu, wd)


def _q_perm():
    idx = []
    for r in range(NSA_R):
        for g in range(NSA_G):
            h = g * NSA_R + r
            idx.extend(range(64 * h, 64 * h + 64))
    return np.asarray(idx)


def _prep_layer(w, l):
    p = {"layer": l}
    win = w["w_in"][l]
    qp = _q_perm()
    o_q = 1032 + 512
    cols = [win[:, 0:1024], win[:, 1032:1032 + 512], win[:, o_q:o_q + 512][:, qp],
            win[:, o_q + 512:o_q + 512 + 768], win[:, 1024:1032], win[:, o_q + 1280:o_q + 1304],
            jnp.zeros((D_MODEL, 128 - 32), F32)]
    p["w_in"] = jnp.concatenate(cols, axis=1).astype(BF16)
    p["w_st"] = win[:, 1024:1032].T.astype(BF16)
    p["norm_mix"] = w["norm_mix"][l][None]
    bias8 = jnp.concatenate([w["ml_i_bias"][l], w["ml_f_bias"][l]])
    p["ml_bcol"] = jnp.zeros((1, 128), F32).at[0, 0:8].set(bias8)
    p["ml_brow"] = bias8[:, None]
    p["conv_w"] = w["conv_w"][l]
    p["conv_b"] = w["conv_b"][l][None]
    bd = lambda m: jax.scipy.linalg.block_diag(*[m[i] for i in range(m.shape[0])])
    p["lru_wa"] = bd(w["lru_wa"][l]).astype(BF16)
    p["lru_wx"] = bd(w["lru_wx"][l]).astype(BF16)
    p["lru_ba"] = w["lru_ba"][l][None]
    p["lru_bx"] = w["lru_bx"][l][None]
    p["lru_lambda"] = w["lru_lambda"][l][None]
    cw = {}
    eye4 = jnp.eye(PAGE // CMP_BLK, dtype=F32)
    pw, ppos, pb1, pw2 = [], [], [], []
    for c, nm in ((0, "k"), (1, "v")):
        w1 = w["phi_w1"][l, c].reshape(CMP_BLK, HD, CMP_HID)
        z = jnp.zeros_like(w1)
        cw["bd" + nm] = jnp.concatenate([jnp.concatenate([w1, z], axis=2),
                                         jnp.concatenate([z, w1], axis=2)], axis=1).astype(BF16)
        cw["pos" + nm] = jnp.tile(w["cmp_pos"][l, c], (1, 2))
        cw["b1" + nm] = jnp.tile(w["phi_b1"][l, c], 2)[None]
        w2 = w["phi_w2"][l, c]
        z2 = jnp.zeros_like(w2)
        cw["w2" + nm] = jnp.concatenate([jnp.concatenate([w2, z2], axis=1),
                                         jnp.concatenate([z2, w2], axis=1)], axis=0).astype(BF16)
        w3 = w1.transpose(1, 0, 2).reshape(HD // 2, 2, CMP_BLK, CMP_HID)
        pw.append(jnp.einsum("pdrh,bc->pdbrch", w3, eye4).reshape(HD // 2, 256, 512).astype(BF16))
        pos3 = w["cmp_pos"][l, c].T.reshape(HD // 2, 2, 1, CMP_BLK)
        ppos.append(jnp.broadcast_to(pos3, (HD // 2, 2, 4, CMP_BLK)).reshape(HD // 2, 256))
        pb1.append(jnp.tile(w["phi_b1"][l, c], 4)[None])
        pw2.append(jnp.kron(eye4, w2).astype(BF16))
    cw["pw"] = jnp.stack(pw)
    cw["ppos"] = jnp.stack(ppos)
    cw["pb1"] = jnp.stack(pb1)
    cw["pw2"] = jnp.stack(pw2)
    p["cmp"] = cw
    perm = np.concatenate([np.arange(512), 512 + qp])
    p["mix_norm"] = w["mix_norm"][l][perm][None]
    p["w_out"] = w["w_out"][l][perm, :].astype(BF16)
    p["norm_xa"] = w["norm_xa"][l][None]
    p["norm_mem"] = w["norm_mem"][l][None]
    p["xa_wq"] = w["xa_wq"][l].astype(BF16)
    p["xa_wkv"] = w["xa_wkv"][l].astype(BF16)
    p["xa_wo"] = w["xa_wo"][l].astype(BF16)
    p["norm_ffn"] = w["norm_ffn"][l][None]
    rw = jnp.concatenate([w["router_ew"][l], w["router_gw"][l], jnp.zeros((D_MODEL, 128 - 20), F32)], axis=1)
    rwh = rw.astype(BF16)
    p["rwh"] = rwh
    p["rwl"] = (rw - rwh.astype(F32)).astype(BF16)
    p["rb"] = jnp.concatenate([w["router_eb"][l], w["router_gb"][l], jnp.zeros((128 - 20,), F32)])[None]
    return p


def _constants(T):
    c = {}
    head = np.arange(1024) // HD
    pool = np.zeros((1024, 128), np.float32)
    pool[np.arange(1024), head] = 1.0 / HD
    poolt = np.zeros((128, 1024), np.float32)
    poolt[head, np.arange(1024)] = 1.0
    c["pool"] = jnp.asarray(pool, BF16)
    c["poolt"] = jnp.asarray(poolt, BF16)
    gexp = np.zeros((3, 128, 512), np.float32)
    for cc in range(3):
        for g in range(NSA_G):
            for r in range(NSA_R):
                gexp[cc, 8 + cc * 8 + g * 4 + r, 128 * r + 64 * g:128 * r + 64 * g + 64] = 1.0
    c["gexp"] = jnp.asarray(gexp, BF16)
    kpos = np.arange(T)
    kaug = np.zeros((T, 128), np.float32)
    kaug[:, AUG_HI] = kpos // SLC_BLK
    kaug[:, AUG_LO] = kpos % SLC_BLK
    kaug[:, AUG_ONE] = 1.0
    kaug[kpos, AUG_SEL + kpos // SLC_BLK] = 1.0
    c["kaug"] = jnp.asarray(kaug, BF16)
    return c


def _even_odd(a):
    return jnp.concatenate([a[:, 0::2], a[:, 1::2]], axis=1)


def _pad_rows(a, rows):
    return jnp.pad(a, ((0, 0), (0, rows - a.shape[1]), (0, 0)))


def _dense_tail(x2, p, ew, B, T, mkv, tm, tq, tm_moe, cache_layer=None):
    n = x2.shape[0]
    q = _norm_mm(x2, p["norm_xa"], p["xa_wq"], tm, BF16).reshape(B, T, XA_INNER)
    if T < tq:
        o = _xattn(_pad_rows(q, tq), mkv, tq, cache_layer)[:, 0:T]
    else:
        o = _xattn(q, mkv, tq, cache_layer)
    x2 = _mm_res(o.reshape(n, XA_INNER), p["xa_wo"], x2, tm)
    return _moe(x2, p["norm_ffn"], p["rwh"], p["rwl"], p["rb"], ew[0], ew[1], ew[2], p["layer"], tm_moe)


def _layer_prompt(x2, p, c, ew, B, T, mem2):
    n = B * T
    ml, lru, q, kv, win, sm, smt, kvs_bf, win_bf = _in_proj(x2, p["norm_mix"], p["w_in"], p["w_st"], 256)
    smt_b = smt.reshape(8, B, T).transpose(1, 0, 2)
    ya, C, nn, mm = _mlstm(ml.reshape(B, T, 1024), sm.reshape(B, T, 128), smt_b, p["ml_bcol"], p["ml_brow"],
                           jnp.zeros((B, 4, 64, 64), F32), jnp.zeros((B, 4, 1, 64), F32),
                           jnp.zeros((B, 1, 128), F32), ML_CHUNK, ML_CHUNK)
    u_tm = lru.reshape(B, T, 512).transpose(1, 0, 2)
    yb_tm, tail_tm, h_last = _lru(u_tm, jnp.zeros((3, B, 256), F32), jnp.zeros((B, 256), F32), p["conv_w"],
                                  p["conv_b"], p["lru_wa"], p["lru_ba"], p["lru_wx"], p["lru_bx"],
                                  p["lru_lambda"], 256)
    yb = yb_tm.transpose(1, 0, 2).reshape(n, 256)
    n_pages = n // PAGE
    kc, vc = _compress_rows(jnp.arange(n_pages, dtype=I32), kv.reshape(n_pages, PAGE, 512), p["cmp"], 16)
    nc = T // CMP_BLK
    kc_eo = _even_odd(kc.reshape(B, nc, 128))
    vc_eo = _even_odd(vc.reshape(B, nc, 128))
    yc = _nsa_prompt(q.reshape(B, T, 512), sm.reshape(B, T, 128), kc_eo, vc_eo, kvs_bf.reshape(B, T, 256),
                     win_bf.reshape(B, T, 256), c["kaug"], c["gexp"])
    x2 = _mix_out(ya.reshape(n, 256), yb, yc.reshape(n, 512), x2, p["mix_norm"], p["w_out"], c["pool"],
                  c["poolt"], 256)
    mkv = _norm_mm(mem2, p["norm_mem"], p["xa_wkv"], 256, F32)
    x2 = _dense_tail(x2, p, ew, B, T, mkv.reshape(B, N_MEM, 1024), 256, 512, 1024)
    st = (kv.reshape(B, T, 4, NSA_G, HD), win.reshape(B, T, 2, NSA_G, HD)[:, T - WINDOW:],
          C, nn.reshape(B, 4, 64), mm[:, 0, 0:4], h_last, tail_tm.transpose(1, 0, 2))
    return x2, st, mkv.reshape(B, N_MEM, 2, XA_H, XA_HD)


def _layer_sample(x2, p, c, ew, B, T, pool_t, pt_flat, win_buf, win_t, C0, n0, m0, conv0, h0, mkv, past):
    n = B * T
    _pad_rows8 = lambda a: _pad_rows(a, 8)
    ml, lru, q, kv, win, sm, smt, _, _ = _in_proj(x2, p["norm_mix"], p["w_in"], p["w_st"], n)
    sm8 = _pad_rows8(sm.reshape(B, T, 128))
    smt_b = jnp.pad(smt.reshape(8, B, T).transpose(1, 0, 2), ((0, 0), (0, 0), (0, 8 - T)))
    m0p = jnp.pad(m0[:, None, :], ((0, 0), (0, 0), (0, 128 - ML_H)))
    ya, C, nn, mm = _mlstm(_pad_rows8(ml.reshape(B, T, 1024)), sm8, smt_b, p["ml_bcol"], p["ml_brow"],
                           C0, n0[:, :, None, :], m0p, 8, T)
    ya = ya[:, 0:T]
    u_tm = lru.reshape(B, T, 512).transpose(1, 0, 2)
    yb_tm, tail_tm, h_last = _lru(u_tm, conv0.transpose(1, 0, 2), h0, p["conv_w"], p["conv_b"], p["lru_wa"],
                                  p["lru_ba"], p["lru_wx"], p["lru_bx"], p["lru_lambda"], T)
    yb = yb_tm.transpose(1, 0, 2).reshape(n, 256)
    n_pages = past // PAGE
    kvc = _compress_pages(pt_flat, pool_t, p["cmp"], CMP_STEP_PAGES)
    wb = win_buf.shape[1]
    q8 = _pad_rows8(q.reshape(B, T, 512).astype(F32))
    win3 = win.reshape(B, T, 256)
    oc, ow, sc = _nsa_sample_sel(q8, kvc, win_t, p["layer"], _pad_rows8(win3), past)
    idx = _topk(sc.reshape(B * NSA_G * 8, 3 * LANES), n_pages)[:, 0:N_SEL].reshape(B, NSA_G, 8, N_SEL)
    kv3 = kv.reshape(B, T, 512)
    yc8 = _nsa_sample_slc(idx, pt_flat, pool_t, q8, _pad_rows8(kv3[:, :, 256:512]), sm8, oc, ow,
                          c["gexp"], T, past)
    yc = yc8[:, 0:T].reshape(n, 512)
    x2 = _mix_out(ya.reshape(n, 256), yb, yc, x2, p["mix_norm"], p["w_out"], c["pool"], c["poolt"], n)
    x2 = _dense_tail(x2, p, ew, B, T, mkv, n, 16, n, cache_layer=p["layer"])
    win_new = jnp.concatenate([win_buf.reshape(B, wb, 256), win3], axis=1)[:, T:]
    st = (kv.reshape(B, T, 4, NSA_G, HD), win_new.reshape(B, wb, 2, NSA_G, HD), C, nn.reshape(B, 4, 64),
          mm[:, 0, 0:4], h_last, tail_tm.transpose(1, 0, 2))
    return x2, st


def kernel(x_prompt, x_sample, cache_nsa_kv, state_nsa_win, state_mlstm_C, state_mlstm_n, state_mlstm_m, state_rglru_h, state_rglru_conv, cache_mem_kv, page_table, mem_prompt, norm_mix, w_in, ml_i_bias, ml_f_bias, conv_w, conv_b, lru_wa, lru_ba, lru_wx, lru_bx, lru_lambda, phi_w1, phi_b1, phi_w2, cmp_pos, mix_norm, w_out, norm_xa, norm_mem, xa_wq, xa_wkv, xa_wo, norm_ffn, router_gw, router_gb, router_ew, router_eb, exp_w_gate, exp_w_up, exp_w_down, final_norm):
    w = dict(norm_mix=norm_mix, w_in=w_in, ml_i_bias=ml_i_bias, ml_f_bias=ml_f_bias, conv_w=conv_w,
             conv_b=conv_b, lru_wa=lru_wa, lru_ba=lru_ba, lru_wx=lru_wx, lru_bx=lru_bx, lru_lambda=lru_lambda,
             phi_w1=phi_w1, phi_b1=phi_b1, phi_w2=phi_w2, cmp_pos=cmp_pos, mix_norm=mix_norm, w_out=w_out,
             norm_xa=norm_xa, norm_mem=norm_mem, xa_wq=xa_wq, xa_wkv=xa_wkv, xa_wo=xa_wo, norm_ffn=norm_ffn,
             router_gw=router_gw, router_gb=router_gb, router_ew=router_ew, router_eb=router_eb)
    depth = w_in.shape[0]
    B, T, _ = x_prompt.shape
    BS, TS, _ = x_sample.shape
    n_pages = page_table.shape[1]
    past = n_pages * PAGE
    n_phys = cache_nsa_kv.shape[1]
    consts = _constants(T)
    layers = [_prep_layer(w, l) for l in range(depth)]
    ew = (exp_w_gate.astype(BF16), exp_w_up.astype(BF16), exp_w_down.astype(BF16))
    fin = final_norm[None]

    x2 = x_prompt.reshape(B * T, D_MODEL)
    mem2 = mem_prompt.reshape(B * N_MEM, D_MODEL)
    outs_p, mem_p = [], []
    for l in range(depth):
        x2, st, mkv = _layer_prompt(x2, layers[l], consts, ew, B, T, mem2)
        outs_p.append(st)
        mem_p.append(mkv)
    y_prompt = _final_norm(x2, fin, 256).reshape(B, T, D_MODEL)

    pool_t = jnp.transpose(cache_nsa_kv, (0, 1, 3, 4, 5, 2)).reshape(depth * n_phys * 4 * NSA_G, HD, PAGE)
    win_t = jnp.transpose(state_nsa_win, (0, 1, 3, 4, 5, 2))
    xs = x_sample.reshape(BS * TS, D_MODEL)
    outs_s = []
    for l in range(depth):
        pt_flat = page_table.reshape(-1) + l * n_phys
        xs, st = _layer_sample(xs, layers[l], consts, ew, BS, TS, pool_t, pt_flat, state_nsa_win[l], win_t,
                               state_mlstm_C[l], state_mlstm_n[l], state_mlstm_m[l], state_rglru_conv[l],
                               state_rglru_h[l], cache_mem_kv, past)
        outs_s.append(st)
    y_sample = _final_norm(xs, fin, BS * TS).reshape(BS, TS, D_MODEL)

    sp = [jnp.stack(a) for a in zip(*outs_p)]
    ss = [jnp.stack(a) for a in zip(*outs_s)]
    return (y_prompt, y_sample, sp[0], ss[0], sp[1], ss[1], sp[2], ss[2], sp[3], ss[3], sp[4], ss[4],
            sp[5], ss[5], sp[6], ss[6], jnp.stack(mem_p))
```

```python
import functools
import math

import jax
import jax.numpy as jnp
import numpy as np
from jax import lax
from jax.experimental import pallas as pl
from jax.experimental.pallas import tpu as pltpu

F32 = jnp.float32
BF16 = jnp.bfloat16
I32 = jnp.int32

D_MODEL = 1024
DEPTH = 2
PAGE = 128
HD = 64
ML_W = 256
ML_H = 4
LRU_W = 256
LRU_C = 8.0
CONV_W = 4
NSA_W = 512
NSA_H = 8
NSA_G = 2
NSA_R = 4
CMP_BLK = 32
CMP_HID = 128
SLC_BLK = 64
N_SEL = 16
WINDOW = 512
FORCE_BONUS = 100.0
XA_H = 4
XA_HD = 128
XA_INNER = 512
N_MEM = 256
N_GROUPS = 4
EXP_PER_GROUP = 4
N_EXP = 16
EXP_FF = 256
EPS = 1e-6
NEG = -1e30
SLOPES = tuple(2.0 ** (-(h + 1)) for h in range(NSA_H))

LANES = 128
SUBLANES = 8
VMEM_LIMIT = 56 * 1024 * 1024

IN_ML = (0, 1024)
IN_LRU = (1024, 1536)
IN_Q = (1536, 2048)
IN_KV = (2048, 2560)
IN_WIN = (2560, 2816)
IN_SMALL = (2816, 2944)
IN_COLS_P = 2944

AUG_HI = 0
AUG_LO = 1
AUG_ONE = 2
AUG_SEL = 8
KEY_CHUNK = 512
ML_BATCH_ROWS = 1
ML_CHUNK = 256
CMP_STEP_PAGES = 128


def _cparams(sem):
    return pltpu.CompilerParams(dimension_semantics=sem, vmem_limit_bytes=VMEM_LIMIT)


def _dot(a, b):
    return jnp.dot(a, b, preferred_element_type=F32)


def _dot_nt(a, b):
    return lax.dot_general(a, b, (((1,), (1,)), ((), ())), preferred_element_type=F32)


def _dot_tn(a, b):
    return lax.dot_general(a, b, (((0,), (0,)), ((), ())), preferred_element_type=F32)


def _split2(x):
    hi = x.astype(BF16)
    lo = (x - hi.astype(F32)).astype(BF16)
    return hi, lo


def _dot_split(x, w_bf):
    hi, lo = _split2(x)
    return _dot(hi, w_bf) + _dot(lo, w_bf)


def _split3(x):
    hi = x.astype(BF16)
    r1 = x - hi.astype(F32)
    mid = r1.astype(BF16)
    lo = (r1 - mid.astype(F32)).astype(BF16)
    return hi, mid, lo


def _dot3_rhs(w_bf, x):
    hi, mid, lo = _split3(x)
    return _dot(w_bf, hi) + _dot(w_bf, mid) + _dot(w_bf, lo)


def _dot3_lhs(x, w_bf):
    hi, mid, lo = _split3(x)
    return _dot(hi, w_bf) + _dot(mid, w_bf) + _dot(lo, w_bf)


def _sigmoid(x):
    return 1.0 / (1.0 + jnp.exp(-x))


def _gelu(x):
    return 0.5 * x * (1.0 + jnp.tanh(0.7978845608028654 * (x + 0.044715 * (x * x * x))))


def _softplus(x):
    return jnp.maximum(x, 0.0) + jnp.log1p(jnp.exp(-jnp.abs(x)))


def _log_sigmoid(x):
    return -_softplus(-x)


def _rms(x, g):
    return x * lax.rsqrt(jnp.mean(x * x, axis=-1, keepdims=True) + EPS) * g


def _masked_softmax(s, mask):
    sm = jnp.where(mask, s, NEG)
    e = jnp.exp(sm - jnp.max(sm, axis=-1, keepdims=True))
    p = e / jnp.sum(e, axis=-1, keepdims=True)
    return jnp.where(mask, p, 0.0)


def _in_proj_body(x_ref, g_ref, w_ref, wst_ref, ml_ref, lru_ref, q_ref, kv_ref, win_ref, sm_ref,
                  smt_ref, kvsb_ref, winb_ref):
    hb = _rms(x_ref[...], g_ref[...]).astype(BF16)

    def mm(rng):
        return _dot(hb, w_ref[:, rng[0]:rng[1]])

    ml_ref[...] = mm(IN_ML)
    lru_ref[...] = mm(IN_LRU)
    q_ref[...] = mm(IN_Q).astype(BF16)
    kv = mm(IN_KV)
    kv_ref[...] = kv
    kvsb_ref[...] = kv[:, 256:512].astype(BF16)
    win = mm(IN_WIN)
    win_ref[...] = win
    winb_ref[...] = win.astype(BF16)
    sm_ref[...] = mm(IN_SMALL)
    smt_ref[...] = _dot_nt(wst_ref[...], hb)


def _in_proj(x2, g, w_p, w_st, tm):
    n = x2.shape[0]
    row = lambda w: pl.BlockSpec((tm, w), lambda i: (i, 0))
    full = lambda a: pl.BlockSpec(a.shape, lambda i: (0,) * a.ndim)
    shapes = [(1024, F32), (512, F32), (512, BF16), (512, F32), (256, F32), (128, F32)]
    out_shape = [jax.ShapeDtypeStruct((n, w), dt) for w, dt in shapes]
    out_specs = [row(w) for w, _ in shapes]
    out_shape += [jax.ShapeDtypeStruct((8, n), F32), jax.ShapeDtypeStruct((n, 256), BF16),
                  jax.ShapeDtypeStruct((n, 256), BF16)]
    out_specs += [pl.BlockSpec((8, tm), lambda i: (0, i)), row(256), row(256)]
    return pl.pallas_call(
        _in_proj_body, grid=(n // tm,),
        in_specs=[row(D_MODEL), full(g), full(w_p), full(w_st)],
        out_specs=out_specs, out_shape=out_shape,
        compiler_params=_cparams(("parallel",)), name="in_proj",
    )(x2, g, w_p, w_st)


def _norm_mm_body(x_ref, g_ref, w_ref, o_ref):
    o_ref[...] = _dot(_rms(x_ref[...], g_ref[...]).astype(BF16), w_ref[...]).astype(o_ref.dtype)


def _norm_mm(x2, g, w_bf, tm, out_dtype):
    n, k = x2.shape
    m = w_bf.shape[1]
    return pl.pallas_call(
        _norm_mm_body, grid=(n // tm,),
        in_specs=[pl.BlockSpec((tm, k), lambda i: (i, 0)), pl.BlockSpec((1, k), lambda i: (0, 0)),
                  pl.BlockSpec((k, m), lambda i: (0, 0))],
        out_specs=pl.BlockSpec((tm, m), lambda i: (i, 0)),
        out_shape=jax.ShapeDtypeStruct((n, m), out_dtype),
        compiler_params=_cparams(("parallel",)), name="norm_mm",
    )(x2, g, w_bf)


def _final_norm_body(x_ref, g_ref, o_ref):
    o_ref[...] = _rms(x_ref[...], g_ref[...])


def _final_norm(x2, g, tm):
    n, k = x2.shape
    return pl.pallas_call(
        _final_norm_body, grid=(n // tm,),
        in_specs=[pl.BlockSpec((tm, k), lambda i: (i, 0)), pl.BlockSpec((1, k), lambda i: (0, 0))],
        out_specs=pl.BlockSpec((tm, k), lambda i: (i, 0)),
        out_shape=jax.ShapeDtypeStruct((n, k), F32),
        compiler_params=_cparams(("parallel",)), name="final_norm",
    )(x2, g)


def _mlstm_body(u_ref, sm_ref, smt_ref, bcol_ref, brow_ref, c0_ref, n0_ref, m0_ref,
                y_ref, c_ref, n_ref, m_ref, c_s, n_s, m_s, *, L, t_real, nbb):
    ci = pl.program_id(1)

    @pl.when(ci == 0)
    def _():
        c_s[...] = c0_ref[...]
        n_s[...] = n0_ref[...]
        m_s[...] = m0_ref[...]

    row = lax.broadcasted_iota(I32, (L, L), 0)
    col = lax.broadcasted_iota(I32, (L, L), 1)
    causal = col <= row
    lane = lax.broadcasted_iota(I32, (1, LANES), 1)
    real_col = lax.broadcasted_iota(I32, (L, 1), 0) < t_real
    real_row = lax.broadcasted_iota(I32, (1, L), 1) < t_real
    tri_lo = jnp.where(causal, 1.0, 0.0).astype(BF16)
    tri_up = jnp.where(row <= col, 1.0, 0.0).astype(BF16)
    ones_v = jnp.ones((L, HD), BF16)
    zpad_cn = jnp.zeros((LANES - HD - 1, HD), F32)
    for bb in range(nbb):
        sm = sm_ref[bb] + bcol_ref[...]
        smt = smt_ref[bb] + brow_ref[...]
        m_all = m_s[bb]
        m_next = m_all
        b_cols = _dot3_rhs(tri_lo, jnp.where(real_col, _log_sigmoid(sm), 0.0))
        b_rows = _dot3_lhs(jnp.where(real_row, _log_sigmoid(smt), 0.0), tri_up)
        for h in range(ML_H):
            q = u_ref[bb, :, 64 * h:64 * h + 64]
            k = u_ref[bb, :, 256 + 64 * h:256 + 64 * h + 64] * (HD ** -0.5)
            v = u_ref[bb, :, 512 + 64 * h:512 + 64 * h + 64]
            o = u_ref[bb, :, 768 + 64 * h:768 + 64 * h + 64]
            ig_col = jnp.where(real_col, sm[:, h:h + 1], NEG)
            ig_row = jnp.where(real_row, smt[h:h + 1, :], NEG)
            b_col = b_cols[:, 4 + h:5 + h]
            b_row = b_rows[4 + h:5 + h, :]
            m_prev = m_all[:, h:h + 1]
            log_d = jnp.where(causal, b_col - b_row + ig_row, NEG)
            inter = b_col + m_prev
            m_t = jnp.maximum(inter, jnp.max(log_d, axis=1, keepdims=True))
            w_carry = jnp.exp(inter - m_t)
            qb = q.astype(BF16)
            vb = v.astype(BF16)
            s = _dot_nt(qb, k.astype(BF16)) * jnp.exp(log_d - m_t)
            c_old = c_s[bb, h]
            n_old = n_s[bb, h]
            sv = _dot(s.astype(BF16), jnp.concatenate([vb, ones_v], axis=1))
            cn = jnp.concatenate([c_old, n_old, zpad_cn], axis=0).astype(BF16)
            qc = _dot_nt(qb, cn)
            num = sv[:, 0:HD] + w_carry * qc[:, 0:HD]
            den = sv[:, HD:HD + 1] + w_carry * qc[:, HD:HD + 1]
            hh = num / jnp.maximum(jnp.abs(den), jnp.exp(-m_t))
            y_ref[bb, :, 64 * h:64 * h + 64] = _sigmoid(o) * hh
            b_end = b_col[L - 1:L, :]
            log_w_row = b_end - b_row + ig_row
            m_new = jnp.maximum(b_end + m_prev, jnp.max(log_w_row, axis=1, keepdims=True))
            w_col = jnp.exp(b_end - b_col + ig_col - m_new)
            decay = jnp.exp(b_end + m_prev - m_new)
            c_s[bb, h] = decay * c_old + _dot_tn((v * w_col).astype(BF16), k.astype(BF16))
            n_s[bb, h] = decay * n_old + jnp.sum(k * w_col, axis=0, keepdims=True)
            m_next = jnp.where(lane == h, m_new, m_next)
        m_s[bb] = m_next

    @pl.when(ci == pl.num_programs(1) - 1)
    def _():
        c_ref[...] = c_s[...]
        n_ref[...] = n_s[...]
        m_ref[...] = m_s[...]


def _mlstm(u_ml, u_sm, u_smt, bcol, brow, c0, n0, m0, L, t_real):
    b, t, _ = u_ml.shape
    nc = t // L
    assert t_real == L or nc == 1
    nbb = ML_BATCH_ROWS
    assert b % nbb == 0
    return pl.pallas_call(
        functools.partial(_mlstm_body, L=L, t_real=t_real, nbb=nbb), grid=(b // nbb, nc),
        in_specs=[pl.BlockSpec((nbb, L, 1024), lambda i, c: (i, c, 0)),
                  pl.BlockSpec((nbb, L, 128), lambda i, c: (i, c, 0)),
                  pl.BlockSpec((nbb, 8, L), lambda i, c: (i, 0, c)),
                  pl.BlockSpec((1, 128), lambda i, c: (0, 0)),
                  pl.BlockSpec((8, 1), lambda i, c: (0, 0)),
                  pl.BlockSpec((nbb, 4, 64, 64), lambda i, c: (i, 0, 0, 0)),
                  pl.BlockSpec((nbb, 4, 1, 64), lambda i, c: (i, 0, 0, 0)),
                  pl.BlockSpec((nbb, 1, 128), lambda i, c: (i, 0, 0))],
        out_specs=[pl.BlockSpec((nbb, L, 256), lambda i, c: (i, c, 0)),
                   pl.BlockSpec((nbb, 4, 64, 64), lambda i, c: (i, 0, 0, 0)),
                   pl.BlockSpec((nbb, 4, 1, 64), lambda i, c: (i, 0, 0, 0)),
                   pl.BlockSpec((nbb, 1, 128), lambda i, c: (i, 0, 0))],
        out_shape=[jax.ShapeDtypeStruct((b, t, 256), F32), jax.ShapeDtypeStruct((b, 4, 64, 64), F32),
                   jax.ShapeDtypeStruct((b, 4, 1, 64), F32), jax.ShapeDtypeStruct((b, 1, 128), F32)],
        scratch_shapes=[pltpu.VMEM((nbb, 4, 64, 64), F32), pltpu.VMEM((nbb, 4, 1, 64), F32),
                        pltpu.VMEM((nbb, 1, 128), F32)],
        compiler_params=_cparams(("parallel", "arbitrary")), name="mlstm",
    )(u_ml, u_sm, u_smt, bcol, brow, c0, n0, m0)


def _lru_body(u_ref, cb_ref, h0_ref, cw_ref, cbias_ref, wa_ref, ba_ref, wx_ref, bx_ref, lam_ref,
              y_ref, tail_ref, hl_ref, tail_s, h_s, a_s, hs_s, *, tt, nb):
    i = pl.program_id(0)

    @pl.when(i == 0)
    def _():
        tail_s[...] = cb_ref[...]
        h_s[...] = h0_ref[...]

    x = u_ref[:, :, 0:LRU_W]
    g = u_ref[:, :, LRU_W:2 * LRU_W]
    xe = jnp.concatenate([tail_s[...], x], axis=0)
    xc = cbias_ref[...] + xe[0:tt] * cw_ref[0:1, :]
    for j in range(1, CONV_W):
        xc = xc + xe[j:j + tt] * cw_ref[j:j + 1, :]
    xc2 = xc.reshape(tt * nb, LRU_W)
    xcb = xc2.astype(BF16)
    r = _sigmoid(_dot(xcb, wa_ref[...]) + ba_ref[...])
    ig = _sigmoid(_dot(xcb, wx_ref[...]) + bx_ref[...])
    log_a = -LRU_C * r * _softplus(-lam_ref[...])
    a = jnp.exp(log_a)
    mult = jnp.sqrt(jnp.tanh(-log_a) * (a * a + 1.0))
    a_s[...] = a.reshape(tt, nb, LRU_W)
    hs_s[...] = (mult * (ig * xc2)).reshape(tt, nb, LRU_W)

    def step(t, h):
        h = a_s[t] * h + hs_s[t]
        hs_s[t] = h
        return h

    h_last = lax.fori_loop(0, tt, step, h_s[...], unroll=min(8, tt))
    h_s[...] = h_last
    y_ref[...] = hs_s[...] * _gelu(g)
    tail_s[...] = xe[tt:tt + CONV_W - 1]

    @pl.when(i == pl.num_programs(0) - 1)
    def _():
        tail_ref[...] = tail_s[...]
        hl_ref[...] = h_s[...]


def _lru(u_tm, cb_tm, h0, cw, cbias, wa_bd, ba, wx_bd, bx, lam, tt):
    t, nb, _ = u_tm.shape
    full = lambda a: pl.BlockSpec(a.shape, lambda i: (0,) * a.ndim)
    return pl.pallas_call(
        functools.partial(_lru_body, tt=tt, nb=nb), grid=(t // tt,),
        in_specs=[pl.BlockSpec((tt, nb, 512), lambda i: (i, 0, 0)), full(cb_tm), full(h0), full(cw),
                  full(cbias), full(wa_bd), full(ba), full(wx_bd), full(bx), full(lam)],
        out_specs=[pl.BlockSpec((tt, nb, 256), lambda i: (i, 0, 0)),
                   pl.BlockSpec((3, nb, 256), lambda i: (0, 0, 0)),
                   pl.BlockSpec((nb, 256), lambda i: (0, 0))],
        out_shape=[jax.ShapeDtypeStruct((t, nb, 256), F32), jax.ShapeDtypeStruct((3, nb, 256), F32),
                   jax.ShapeDtypeStruct((nb, 256), F32)],
        scratch_shapes=[pltpu.VMEM((3, nb, 256), F32), pltpu.VMEM((nb, 256), F32),
                        pltpu.VMEM((tt, nb, 256), F32), pltpu.VMEM((tt, nb, 256), F32)],
        compiler_params=_cparams(("arbitrary",)), name="rglru",
    )(u_tm, cb_tm, h0, cw, cbias, wa_bd, ba, wx_bd, bx, lam)


def _compress_rows_body(pt_ref, pool_ref, bdk_ref, bdv_ref, posk_ref, posv_ref, b1k_ref, b1v_ref,
                        w2k_ref, w2v_ref, kc_ref, vc_ref, buf, sem, *, P):
    s = pl.program_id(0)
    ns = pl.num_programs(0)
    nblk = P * (PAGE // CMP_BLK)

    def page_copy(step, slot, p, c):
        return pltpu.make_async_copy(
            pool_ref.at[pt_ref[step * P + p], :, pl.ds(LANES * c, LANES)],
            buf.at[slot, c, pl.ds(p * PAGE, PAGE), :], sem.at[slot])

    def start_all(step, slot):
        def body(p, carry):
            page_copy(step, slot, p, 0).start()
            page_copy(step, slot, p, 1).start()
            return carry
        lax.fori_loop(0, P, body, 0)

    def wait_all(step, slot):
        def body(p, carry):
            page_copy(step, slot, p, 0).wait()
            page_copy(step, slot, p, 1).wait()
            return carry
        lax.fori_loop(0, P, body, 0)

    @pl.when(s == 0)
    def _():
        start_all(0, 0)

    slot = lax.rem(s, 2)

    @pl.when(s + 1 < ns)
    def _():
        start_all(s + 1, 1 - slot)

    wait_all(s, slot)

    acc_k = jnp.zeros((nblk, 256), F32)
    acc_v = jnp.zeros((nblk, 256), F32)
    for r in range(CMP_BLK):
        rows = pl.ds(r, nblk, stride=CMP_BLK)
        xk = (buf[slot, 0, rows, :] + posk_ref[r:r + 1, :]).astype(BF16)
        xv = (buf[slot, 1, rows, :] + posv_ref[r:r + 1, :]).astype(BF16)
        acc_k = acc_k + _dot(xk, bdk_ref[r])
        acc_v = acc_v + _dot(xv, bdv_ref[r])
    kc_ref[...] = _dot(_gelu(acc_k + b1k_ref[...]).astype(BF16), w2k_ref[...])
    vc_ref[...] = _dot(_gelu(acc_v + b1v_ref[...]).astype(BF16), w2v_ref[...])


def _compress_rows(pt_flat, pool3, cw, P):
    n_pages = pt_flat.shape[0]
    nblk = P * (PAGE // CMP_BLK)
    full = lambda a: pl.BlockSpec(a.shape, lambda i, pt: (0,) * a.ndim)
    ws = (cw["bdk"], cw["bdv"], cw["posk"], cw["posv"], cw["b1k"], cw["b1v"], cw["w2k"], cw["w2v"])
    grid_spec = pltpu.PrefetchScalarGridSpec(
        num_scalar_prefetch=1, grid=(n_pages // P,),
        in_specs=[pl.BlockSpec(memory_space=pl.ANY)] + [full(a) for a in ws],
        out_specs=[pl.BlockSpec((nblk, 128), lambda i, pt: (i, 0)),
                   pl.BlockSpec((nblk, 128), lambda i, pt: (i, 0))],
        scratch_shapes=[pltpu.VMEM((2, 2, P * PAGE, LANES), F32), pltpu.SemaphoreType.DMA((2,))])
    return pl.pallas_call(
        functools.partial(_compress_rows_body, P=P), grid_spec=grid_spec,
        out_shape=[jax.ShapeDtypeStruct((n_pages * 4, 128), F32)] * 2,
        compiler_params=_cparams(("arbitrary",)), name="compress_rows",
    )(pt_flat, pool3, *ws)


def _compress_pages_body(pt_ref, pool_ref, w_ref, pos_ref, b1_ref, w2_ref, o_ref, buf, sem, *, P, nstep):
    c = pl.program_id(0)
    s = pl.program_id(1)
    lin = c * nstep + s
    PG = NSA_G * P

    def tile_copy(cc, step, slot, p):
        return pltpu.make_async_copy(
            pool_ref.at[pl.ds((pt_ref[step * P + p] * 4 + cc) * NSA_G, NSA_G)],
            buf.at[slot, :, :, p, :], sem.at[slot])

    def start_all(cc, step, slot):
        def body(p, carry):
            tile_copy(cc, step, slot, p).start()
            return carry
        lax.fori_loop(0, P, body, 0, unroll=4)

    def wait_all(slot):
        pltpu.make_async_copy(buf.at[slot], buf.at[slot], sem.at[slot]).wait()

    @pl.when(lin == 0)
    def _():
        start_all(0, 0, 0)

    slot = lax.rem(lin, 2)
    nxt = lin + 1

    @pl.when(nxt < 2 * nstep)
    def _():
        start_all(lax.div(nxt, nstep), lax.rem(nxt, nstep), 1 - slot)

    wait_all(slot)

    acc = jnp.zeros((PG, 4 * CMP_HID), F32)
    for dp in range(HD // 2):
        chan = [jnp.concatenate([buf[slot, 0, d], buf[slot, 1, d]], axis=0) for d in (2 * dp, 2 * dp + 1)]
        lhs = jnp.concatenate(chan, axis=1) + pos_ref[dp:dp + 1, :]
        acc = acc + _dot(lhs.astype(BF16), w_ref[dp])
    h = _gelu(acc + b1_ref[...])
    out = _dot(h.astype(BF16), w2_ref[...])
    for blk in range(PAGE // CMP_BLK):
        cols = slice(HD * blk, HD * blk + HD)
        o_ref[blk] = jnp.concatenate([out[0:P, cols], out[P:2 * P, cols]], axis=1)


def _compress_pages(pt_flat, pool_t, cw, P):
    n_pages = pt_flat.shape[0]
    nstep = n_pages // P
    PG = NSA_G * P
    grid_spec = pltpu.PrefetchScalarGridSpec(
        num_scalar_prefetch=1, grid=(2, nstep),
        in_specs=[pl.BlockSpec(memory_space=pl.ANY),
                  pl.BlockSpec((None, HD // 2, 256, 512), lambda c, s, pt: (c, 0, 0, 0)),
                  pl.BlockSpec((None, HD // 2, 256), lambda c, s, pt: (c, 0, 0)),
                  pl.BlockSpec((None, 1, 512), lambda c, s, pt: (c, 0, 0)),
                  pl.BlockSpec((None, 512, 256), lambda c, s, pt: (c, 0, 0))],
        out_specs=pl.BlockSpec((None, PAGE // CMP_BLK, P, LANES), lambda c, s, pt: (c, 0, s, 0)),
        scratch_shapes=[pltpu.VMEM((2, NSA_G, HD, P, LANES), F32), pltpu.SemaphoreType.DMA((2,))])
    return pl.pallas_call(
        functools.partial(_compress_pages_body, P=P, nstep=nstep), grid_spec=grid_spec,
        out_shape=jax.ShapeDtypeStruct((2, PAGE // CMP_BLK, n_pages, LANES), F32),
        compiler_params=_cparams(("arbitrary", "arbitrary")), name="compress_pages",
    )(pt_flat, pool_t, cw["pw"], cw["ppos"], cw["pb1"], cw["pw2"])


def _half_mask(g):
    lane = lax.broadcasted_iota(I32, (1, LANES), 1)
    return (lane >= 64 * g) & (lane < 64 * g + 64)


def _gate_expand(gates, gexp_ref):
    hi, lo = _split2(gates)
    return [_dot(hi, gexp_ref[c]) + _dot(lo, gexp_ref[c]) for c in range(3)]


def _nsa_prompt_body(q_ref, sm_ref, kc_ref, vc_ref, kvs_ref, kaug_ref, win_ref, gexp_ref, o_ref,
                     s_scr, mx_scr, acc_scr, *, T):
    QT = 128
    M = NSA_H * QT
    CK = KEY_CHUNK
    q0 = pl.program_id(1) * QT
    lane = lax.broadcasted_iota(I32, (1, LANES), 1)
    left = lane < 64
    rowm = lax.broadcasted_iota(I32, (M, 1), 0)
    qposm = q0 + (rowm & (QT - 1))
    slopem = jnp.zeros((M, 1), F32)
    for h in range(NSA_H):
        slopem = jnp.where((rowm >= QT * h) & (rowm < QT * (h + 1)), SLOPES[h], slopem)

    nce = T // CMP_BLK // 2
    blk = jnp.where(lane < nce, 2 * lane, 2 * (lane - nce) + 1)
    c_end = jnp.where(lane < 2 * nce, blk * CMP_BLK + (CMP_BLK - 1), 1 << 30)
    dist_c = qposm - c_end
    mask_c = dist_c >= 0
    zpad = jnp.zeros((LANES - 2 * nce, LANES), F32)
    kc = jnp.concatenate([kc_ref[...], zpad], axis=0).astype(BF16)
    vc = jnp.concatenate([vc_ref[...], zpad], axis=0).astype(BF16)

    qs = [jnp.where(_half_mask(g), q_ref[:, 128 * r:128 * r + 128], 0)
          for g in range(NSA_G) for r in range(NSA_R)]
    qm = jnp.concatenate(qs, axis=0)

    s = _dot_nt(qm, kc) * (HD ** -0.5) - slopem * dist_c.astype(F32)
    p = _masked_softmax(s, mask_c)
    o_cmp = _dot(p.astype(BF16), vc)

    ns = T // SLC_BLK
    jrow = lax.broadcasted_iota(I32, (ns, 1), 0)
    cur = (q0 + lane) // SLC_BLK
    in_sel = (lane >= AUG_SEL) & (lane < AUG_SEL + ns)
    valid = jrow <= cur
    forced = (jrow == 0) | (jrow == cur) | (jrow == cur - 1)
    sel_bias = []
    for g in range(NSA_G):
        b0 = g * NSA_R * QT
        imp = p[b0:b0 + QT] + p[b0 + QT:b0 + 2 * QT] + p[b0 + 2 * QT:b0 + 3 * QT] + p[b0 + 3 * QT:b0 + 4 * QT]
        imp_t = imp.T
        pooled = imp_t[0:nce] + imp_t[nce:2 * nce]
        score = jnp.where(valid, pooled + FORCE_BONUS * forced.astype(F32), NEG)
        cnt = jnp.zeros((ns, LANES), F32)
        for i2 in range(ns):
            si = score[i2:i2 + 1, :]
            beats = (si > score) | ((si == score) & (i2 < jrow))
            cnt = cnt + beats.astype(F32)
        sel_t = ((cnt < N_SEL) & (score > 0.5 * NEG)).astype(F32)
        selp = jnp.concatenate([jnp.zeros((AUG_SEL, LANES), F32), sel_t,
                                jnp.zeros((LANES - AUG_SEL - ns, LANES), F32)], axis=0).T
        sel_bias.append(jnp.where(in_sel, (selp - 1.0) * (-NEG), 0.0))

    shift = -(q0 + QT - 1).astype(F32)

    def aug(h, base):
        sl = SLOPES[h]
        a = jnp.where(lane == AUG_HI, sl * SLC_BLK, jnp.where(lane == AUG_LO, sl,
                      jnp.where(lane == AUG_ONE, sl * shift, base)))
        return a.astype(BF16)

    qsc = [(qs[h].astype(F32) * (HD ** -0.5)).astype(BF16) for h in range(NSA_H)]
    zero_t = jnp.zeros((QT, LANES), F32)
    qsel = jnp.concatenate([jnp.concatenate([qsc[h], aug(h, sel_bias[h // NSA_R])], axis=1)
                            for h in range(NSA_H)], axis=0)
    qwin = jnp.concatenate([jnp.concatenate([qsc[h], aug(h, zero_t)], axis=1) for h in range(NSA_H)], axis=0)
    ones_k = jnp.ones((CK, LANES), BF16)

    def key_chunk(c):
        r0 = pl.multiple_of(c * CK, CK)
        return jnp.concatenate([kvs_ref[pl.ds(r0, CK), 0:128], kaug_ref[pl.ds(r0, CK), :]], axis=1)

    def fold(sc):
        out = sc[:, 0:LANES]
        for t in range(1, CK // LANES):
            out = jnp.maximum(out, sc[:, LANES * t:LANES * (t + 1)])
        return out

    nk = q0 // CK + 1
    mx_scr[...] = jnp.full((M, LANES), NEG, F32)

    def body1(c, carry):
        sc = _dot_nt(qsel, key_chunk(c))
        s_scr[c] = sc
        mx_scr[...] = jnp.maximum(mx_scr[...], fold(sc))
        return carry

    lax.fori_loop(0, nk - 1, body1, 0)
    cl = nk - 1
    kpos_l = cl * CK + lax.broadcasted_iota(I32, (1, CK), 1)
    sc = jnp.where(kpos_l <= qposm, _dot_nt(qsel, key_chunk(cl)), NEG)
    s_scr[cl] = sc
    m = jnp.max(jnp.maximum(mx_scr[...], fold(sc)), axis=1, keepdims=True)

    acc_scr[...] = jnp.zeros((M, 2 * LANES), F32)

    def body2(c, carry):
        e = jnp.exp(s_scr[c] - m).astype(BF16)
        r0 = pl.multiple_of(c * CK, CK)
        vo = jnp.concatenate([kvs_ref[pl.ds(r0, CK), 128:256], ones_k], axis=1)
        acc_scr[...] += _dot(e, vo)
        return carry

    lax.fori_loop(0, nk, body2, 0)
    acc = acc_scr[...]
    o_slc = acc[:, 0:128] / acc[:, 128:129]

    w0 = pl.multiple_of(jnp.maximum(q0 - WINDOW, 0), QT)
    WK = WINDOW + QT
    dist_w = qposm - (w0 + lax.broadcasted_iota(I32, (1, WK), 1))
    mask_w = (dist_w >= 0) & (dist_w < WINDOW)
    kwin = jnp.concatenate([win_ref[pl.ds(w0, WK), 0:128], kaug_ref[pl.ds(w0, WK), :]], axis=1)
    sw = jnp.where(mask_w, _dot_nt(qwin, kwin), NEG)
    e = jnp.exp(sw - jnp.max(sw, axis=1, keepdims=True)).astype(BF16)
    vo = jnp.concatenate([win_ref[pl.ds(w0, WK), 128:256], jnp.ones((WK, LANES), BF16)], axis=1)
    accw = _dot(e, vo)
    o_win = accw[:, 0:128] / accw[:, 128:129]

    gates = _sigmoid(sm_ref[...])
    gc, gs, gw = _gate_expand(gates, gexp_ref)
    for r in range(NSA_R):
        sl = slice(128 * r, 128 * r + 128)
        r0 = slice(QT * r, QT * (r + 1))
        r1 = slice(QT * (NSA_R + r), QT * (NSA_R + r + 1))
        oc = jnp.where(left, o_cmp[r0], o_cmp[r1])
        os_ = jnp.where(left, o_slc[r0], o_slc[r1])
        ow = jnp.where(left, o_win[r0], o_win[r1])
        o_ref[:, sl] = gc[:, sl] * oc + gs[:, sl] * os_ + gw[:, sl] * ow


def _nsa_prompt(q, sm, kc_eo, vc_eo, kvs_bf, win_bf, kaug, gexp):
    b, t, _ = q.shape
    return pl.pallas_call(
        functools.partial(_nsa_prompt_body, T=t), grid=(b, t // 128),
        in_specs=[pl.BlockSpec((None, 128, 512), lambda i, j: (i, j, 0)),
                  pl.BlockSpec((None, 128, 128), lambda i, j: (i, j, 0)),
                  pl.BlockSpec((None, t // CMP_BLK, 128), lambda i, j: (i, 0, 0)),
                  pl.BlockSpec((None, t // CMP_BLK, 128), lambda i, j: (i, 0, 0)),
                  pl.BlockSpec((None, t, 256), lambda i, j: (i, 0, 0)),
                  pl.BlockSpec(kaug.shape, lambda i, j: (0, 0)),
                  pl.BlockSpec((None, t, 256), lambda i, j: (i, 0, 0)),
                  pl.BlockSpec(gexp.shape, lambda i, j: (0, 0, 0))],
        out_specs=pl.BlockSpec((None, 128, 512), lambda i, j: (i, j, 0)),
        out_shape=jax.ShapeDtypeStruct((b, t, 512), F32),
        scratch_shapes=[pltpu.VMEM((t // KEY_CHUNK, NSA_H * 128, KEY_CHUNK), F32),
                        pltpu.VMEM((NSA_H * 128, LANES), F32), pltpu.VMEM((NSA_H * 128, 2 * LANES), F32)],
        compiler_params=_cparams(("parallel", "arbitrary")), name="nsa_prompt",
    )(q, sm, kc_eo, vc_eo, kvs_bf, kaug, win_bf, gexp)


def _sel_block_of_lane(lane, n_pages):
    return jnp.where(lane < n_pages, 2 * lane,
                     jnp.where(lane < 2 * n_pages, 2 * (lane - n_pages) + 1,
                               jnp.where(lane == 2 * n_pages, 2 * n_pages, 1 << 20)))


def _nsa_sample_sel_body(q_ref, kc_ref, vc_ref, wbuf_ref, wnew_ref, oc_ref, ow_ref, sc_ref, *, past, n_pages):
    R8 = 8
    per_page = PAGE // CMP_BLK
    ncmp = per_page * n_pages
    qpos = past + lax.broadcasted_iota(I32, (R8, 1), 0)
    lane_c = lax.broadcasted_iota(I32, (1, ncmp), 1)
    blk = (lane_c % n_pages) * per_page + lane_c // n_pages
    dist_c = qpos - (blk * CMP_BLK + (CMP_BLK - 1))
    mask_c = dist_c >= 0
    dist_cf = dist_c.astype(F32)
    kc = kc_ref[...].reshape(ncmp, LANES).astype(BF16)
    vc = vc_ref[...].reshape(ncmp, LANES).astype(BF16)

    wb = wbuf_ref.shape[3]
    wk = wbuf_ref[0].reshape(NSA_G * HD, wb).astype(BF16)
    wv = wbuf_ref[1].reshape(NSA_G * HD, wb).astype(BF16)
    nk = wnew_ref[:, 0:128].astype(BF16)
    nv = wnew_ref[:, 128:256].astype(BF16)
    dist_w1 = qpos - (past - wb + lax.broadcasted_iota(I32, (1, wb), 1))
    mask_w1 = (dist_w1 >= 0) & (dist_w1 < WINDOW)
    dist_w2 = qpos - (past + lax.broadcasted_iota(I32, (1, R8), 1))
    mask_w2 = (dist_w2 >= 0) & (dist_w2 < WINDOW)

    ns_l = 3 * LANES
    assert 2 * n_pages + 1 <= ns_l
    lane_s = lax.broadcasted_iota(I32, (1, ns_l), 1)
    jmap = _sel_block_of_lane(lane_s, n_pages)
    cur = qpos // SLC_BLK
    lane16 = lax.broadcasted_iota(I32, (1, LANES), 1)
    left = lane16 < 64

    oc = [[None] * NSA_R for _ in range(NSA_G)]
    ow = [[None] * NSA_R for _ in range(NSA_G)]
    for g in range(NSA_G):
        hm = _half_mask(g)
        imp = jnp.zeros((R8, ncmp), F32)
        for r in range(NSA_R):
            slope = SLOPES[g * NSA_R + r]
            qh = jnp.where(hm, q_ref[:, 128 * r:128 * r + 128], 0).astype(BF16)
            s = _dot_nt(qh, kc) * (HD ** -0.5) - slope * dist_cf
            p = _masked_softmax(s, mask_c)
            imp = imp + p
            oc[g][r] = _dot(p.astype(BF16), vc)
            s1 = jnp.where(mask_w1, _dot(qh, wk) * (HD ** -0.5) - slope * dist_w1.astype(F32), NEG)
            s2 = jnp.where(mask_w2, _dot_nt(qh, nk) * (HD ** -0.5) - slope * dist_w2.astype(F32), NEG)
            mx = jnp.maximum(jnp.max(s1, axis=1, keepdims=True), jnp.max(s2, axis=1, keepdims=True))
            e1 = jnp.exp(s1 - mx)
            e2 = jnp.exp(s2 - mx)
            den = jnp.sum(e1, axis=1, keepdims=True) + jnp.sum(e2, axis=1, keepdims=True)
            p1 = jnp.where(mask_w1, e1 / den, 0.0)
            p2 = jnp.where(mask_w2, e2 / den, 0.0)
            ow[g][r] = _dot_nt(p1.astype(BF16), wv) + _dot(p2.astype(BF16), nv)
        np_ = n_pages
        pooled = jnp.concatenate([imp[:, 0:np_] + imp[:, np_:2 * np_], imp[:, 2 * np_:3 * np_] + imp[:, 3 * np_:],
                                  jnp.zeros((R8, ns_l - 2 * np_), F32)], axis=1)
        valid = jmap <= cur
        forced = (jmap == 0) | (jmap == cur) | (jmap == cur - 1)
        score = jnp.where(valid, pooled + FORCE_BONUS * forced.astype(F32), NEG)
        sc_ref[g] = jnp.where(lane_s <= 2 * np_, score, -jnp.inf)
    for r in range(NSA_R):
        sl = slice(128 * r, 128 * r + 128)
        oc_ref[:, sl] = jnp.where(left, oc[0][r], oc[1][r])
        ow_ref[:, sl] = jnp.where(left, ow[0][r], ow[1][r])


def _nsa_sample_sel(q8, kvc, win_t, layer, wnew8, past):
    b = q8.shape[0]
    n_pages = past // PAGE
    per_page = PAGE // CMP_BLK
    wb = win_t.shape[-1]
    return pl.pallas_call(
        functools.partial(_nsa_sample_sel_body, past=past, n_pages=n_pages), grid=(b,),
        in_specs=[pl.BlockSpec((None, 8, 512), lambda i: (i, 0, 0)),
                  pl.BlockSpec((None, per_page, n_pages, 128), lambda i: (0, 0, i, 0)),
                  pl.BlockSpec((None, per_page, n_pages, 128), lambda i: (1, 0, i, 0)),
                  pl.BlockSpec((None, None, 2, NSA_G, HD, wb), lambda i: (layer, i, 0, 0, 0, 0)),
                  pl.BlockSpec((None, 8, 256), lambda i: (i, 0, 0))],
        out_specs=[pl.BlockSpec((None, 8, 512), lambda i: (i, 0, 0)),
                   pl.BlockSpec((None, 8, 512), lambda i: (i, 0, 0)),
                   pl.BlockSpec((None, 2, 8, 3 * LANES), lambda i: (i, 0, 0, 0))],
        out_shape=[jax.ShapeDtypeStruct((b, 8, 512), F32), jax.ShapeDtypeStruct((b, 8, 512), F32),
                   jax.ShapeDtypeStruct((b, 2, 8, 3 * LANES), F32)],
        compiler_params=_cparams(("parallel",)), name="nsa_sample_sel",
    )(q8, kvc, kvc, win_t, wnew8)


def _topk_body(sc_ref, idx_ref, *, n_pages):
    score = sc_ref[...]
    rows, width = score.shape
    blk_f = _sel_block_of_lane(lax.broadcasted_iota(I32, (1, width), 1), n_pages).astype(F32)
    lane_o = lax.broadcasted_iota(I32, (1, LANES), 1)
    idx_acc = jnp.zeros((rows, LANES), F32)
    for n in range(N_SEL):
        mx = jnp.max(score, axis=1, keepdims=True)
        pick = jnp.min(jnp.where(score == mx, blk_f, 2e6), axis=1, keepdims=True)
        idx_acc = jnp.where(lane_o == n, jnp.where(mx > 0.5 * NEG, pick, -1.0), idx_acc)
        score = jnp.where(blk_f == pick, -jnp.inf, score)
    idx_ref[...] = idx_acc.astype(I32)


def _topk(score2, n_pages):
    rows, width = score2.shape
    return pl.pallas_call(
        functools.partial(_topk_body, n_pages=n_pages), grid=(1,),
        in_specs=[pl.BlockSpec((rows, width), lambda i: (0, 0))],
        out_specs=pl.BlockSpec((rows, LANES), lambda i: (0, 0)),
        out_shape=jax.ShapeDtypeStruct((rows, LANES), I32),
        compiler_params=_cparams(("arbitrary",)), name="topk",
    )(score2)


def _nsa_sample_slc_body(idx_ref, tile_ref, pool_ref, q_ref, knew_ref, sm_ref, oc_ref, ow_ref, gexp_ref,
                         o_ref, kbuf, vbuf, sem, *, T, past):
    b = pl.program_id(0)
    nb = pl.num_programs(0)
    ns_past = past // SLC_BLK
    per_page = PAGE // SLC_BLK
    nslot = NSA_G * T * N_SEL

    def start_all(bb, buf_slot):
        def body(i, carry):
            kt = tile_ref[bb * nslot + i]
            pltpu.make_async_copy(pool_ref.at[kt], kbuf.at[buf_slot, i], sem.at[0, buf_slot]).start()
            pltpu.make_async_copy(pool_ref.at[kt + NSA_G], vbuf.at[buf_slot, i], sem.at[1, buf_slot]).start()
            return carry
        lax.fori_loop(0, nslot, body, 0, unroll=4)

    @pl.when(b == 0)
    def _():
        start_all(0, 0)

    cur_slot = lax.rem(b, 2)

    @pl.when(b + 1 < nb)
    def _():
        start_all(b + 1, 1 - cur_slot)

    pltpu.make_async_copy(kbuf.at[cur_slot], kbuf.at[cur_slot], sem.at[0, cur_slot]).wait()
    pltpu.make_async_copy(vbuf.at[cur_slot], vbuf.at[cur_slot], sem.at[1, cur_slot]).wait()

    R8 = 8
    NK = N_SEL * PAGE
    lane_k = lax.broadcasted_iota(I32, (1, NK), 1)
    slot_k = lane_k // PAGE
    within = lane_k - slot_k * PAGE
    half_k = within // SLC_BLK
    off_k = within - half_k * SLC_BLK
    rowi = lax.broadcasted_iota(I32, (R8, 1), 0)
    knew = knew_ref[:, 0:128].astype(BF16)
    vnew = knew_ref[:, 128:256].astype(BF16)
    kpos_new = past + lax.broadcasted_iota(I32, (1, R8), 1)
    zhalf = jnp.zeros((R8, HD), F32)
    o_acc = [jnp.zeros((R8, LANES), F32) for _ in range(NSA_R)]
    for g in range(NSA_G):
        hm = _half_mask(g)
        slope = jnp.zeros((R8, 1), F32)
        for r in range(NSA_R):
            slope = jnp.where(rowi == r, SLOPES[g * NSA_R + r], slope)
        for t in range(T):
            qpos = past + t
            qm = jnp.zeros((R8, LANES), F32)
            for r in range(NSA_R):
                qm = jnp.where(rowi == r, q_ref[t:t + 1, 128 * r:128 * r + 128], qm)
            qm = jnp.where(hm, qm, 0.0)
            qc = (qm[:, 0:HD] + qm[:, HD:2 * HD]).astype(BF16)
            base = (g * T + t) * N_SEL
            kt = jnp.concatenate([kbuf[cur_slot, base + n] for n in range(N_SEL)], axis=1).astype(BF16)
            vt = jnp.concatenate([vbuf[cur_slot, base + n] for n in range(N_SEL)], axis=1).astype(BF16)
            kpos = off_k
            ok_i = jnp.zeros((1, NK), I32)
            has_new = jnp.int32(0)
            for n in range(N_SEL):
                j = idx_ref[((b * NSA_G + g) * 8 + t) * N_SEL + n]
                in_blk = (slot_k == n) & (half_k == lax.rem(j, per_page))
                kpos = jnp.where(in_blk, j * SLC_BLK + off_k, kpos)
                ok_i = jnp.where(in_blk, ((j >= 0) & (j < ns_past)).astype(I32), ok_i)
                has_new = has_new + (j == ns_past).astype(I32)
            dist1 = qpos - kpos
            mask1 = (ok_i > 0) & (dist1 >= 0)
            dist2 = qpos - kpos_new
            mask2 = (dist2 >= 0) & ((jnp.zeros((1, R8), I32) + has_new) > 0)
            s1 = jnp.where(mask1, _dot(qc, kt) * (HD ** -0.5) - slope * dist1.astype(F32), NEG)
            s2 = jnp.where(mask2, _dot_nt(qm.astype(BF16), knew) * (HD ** -0.5) - slope * dist2.astype(F32), NEG)
            mx = jnp.maximum(jnp.max(s1, axis=1, keepdims=True), jnp.max(s2, axis=1, keepdims=True))
            e1 = jnp.exp(s1 - mx)
            e2 = jnp.exp(s2 - mx)
            den = jnp.sum(e1, axis=1, keepdims=True) + jnp.sum(e2, axis=1, keepdims=True)
            p1 = jnp.where(mask1, e1 / den, 0.0)
            p2 = jnp.where(mask2, e2 / den, 0.0)
            o1 = _dot_nt(p1.astype(BF16), vt)
            o1 = jnp.concatenate([o1, zhalf] if g == 0 else [zhalf, o1], axis=1)
            o = o1 + jnp.where(hm, _dot(p2.astype(BF16), vnew), 0.0)
            for r in range(NSA_R):
                o_acc[r] = jnp.where((rowi == t) & hm, o[r:r + 1, :], o_acc[r])
    gates = _sigmoid(sm_ref[...])
    gc, gs, gw = _gate_expand(gates, gexp_ref)
    for r in range(NSA_R):
        sl = slice(128 * r, 128 * r + 128)
        o_ref[:, sl] = gc[:, sl] * oc_ref[:, sl] + gs[:, sl] * o_acc[r] + gw[:, sl] * ow_ref[:, sl]


def _nsa_sample_slc(idx, pt_flat, pool_t, q8, kvnew8, sm8, oc, ow, gexp, T, past):
    b = q8.shape[0]
    n_pages = pt_flat.shape[0] // b
    row = lambda w: pl.BlockSpec((None, 8, w), lambda i, a, c: (i, 0, 0))
    nslot = NSA_G * T * N_SEL
    page = jnp.clip(idx[:, :, 0:T], 0, past // SLC_BLK - 1) // (PAGE // SLC_BLK)
    phys = jnp.take_along_axis(pt_flat.reshape(b, 1, 1, n_pages), page.reshape(b, 1, 1, -1), axis=3)
    grp = jnp.arange(NSA_G, dtype=I32).reshape(1, NSA_G, 1, 1)
    tiles = ((phys.reshape(page.shape) * 4 + 2) * NSA_G + grp).reshape(-1)
    idx_flat = idx.reshape(-1)
    grid_spec = pltpu.PrefetchScalarGridSpec(
        num_scalar_prefetch=2, grid=(b,),
        in_specs=[pl.BlockSpec(memory_space=pl.ANY), row(512), row(256), row(128), row(512), row(512),
                  pl.BlockSpec(gexp.shape, lambda i, a, c: (0, 0, 0))],
        out_specs=row(512),
        scratch_shapes=[pltpu.VMEM((2, nslot, HD, PAGE), F32), pltpu.VMEM((2, nslot, HD, PAGE), F32),
                        pltpu.SemaphoreType.DMA((2, 2))])
    return pl.pallas_call(
        functools.partial(_nsa_sample_slc_body, T=T, past=past), grid_spec=grid_spec,
        out_shape=jax.ShapeDtypeStruct((b, 8, 512), F32),
        compiler_params=_cparams(("arbitrary",)), name="nsa_sample_slc",
    )(idx_flat, tiles, pool_t, q8, kvnew8, sm8, oc, ow, gexp)


def _mix_out_body(ya_ref, yb_ref, yc_ref, x_ref, g_ref, w_ref, pool_ref, poolt_ref, gq_ref, wq_ref,
                  o_ref, q_ref):
    y = jnp.concatenate([ya_ref[...], yb_ref[...], yc_ref[...]], axis=-1)
    ms = _dot_split(y * y, pool_ref[...])
    rb = _dot_split(lax.rsqrt(ms + EPS), poolt_ref[...])
    yn = (y * rb * g_ref[...]).astype(BF16)
    x1 = x_ref[...] + _dot(yn, w_ref[...])
    o_ref[...] = x1
    q_ref[...] = _dot(_rms(x1, gq_ref[...]).astype(BF16), wq_ref[...]).astype(BF16)


def _mix_out(ya, yb, yc, x2, g, w_bf, pool, poolt, gq, wq_bf, tm):
    n = x2.shape[0]
    row = lambda w: pl.BlockSpec((tm, w), lambda i: (i, 0))
    full = lambda a: pl.BlockSpec(a.shape, lambda i: (0,) * a.ndim)
    return pl.pallas_call(
        _mix_out_body, grid=(n // tm,),
        in_specs=[row(256), row(256), row(512), row(1024), full(g), full(w_bf), full(pool), full(poolt),
                  full(gq), full(wq_bf)],
        out_specs=[row(1024), row(XA_INNER)],
        out_shape=[jax.ShapeDtypeStruct((n, 1024), F32), jax.ShapeDtypeStruct((n, XA_INNER), BF16)],
        compiler_params=_cparams(("parallel",)), name="mix_out",
    )(ya, yb, yc, x2, g, w_bf, pool, poolt, gq, wq_bf)


def _xattn_body(q_ref, kv_ref, o_ref):
    for h in range(XA_H):
        if len(kv_ref.shape) == 2:
            k = kv_ref[:, 128 * h:128 * h + 128].astype(BF16)
            v = kv_ref[:, XA_INNER + 128 * h:XA_INNER + 128 * h + 128].astype(BF16)
        else:
            k = kv_ref[:, 0, h, :].astype(BF16)
            v = kv_ref[:, 1, h, :].astype(BF16)
        s = _dot_nt(q_ref[:, 128 * h:128 * h + 128], k) * (XA_HD ** -0.5)
        e = jnp.exp(s - jnp.max(s, axis=-1, keepdims=True))
        a = e / jnp.sum(e, axis=-1, keepdims=True)
        o_ref[:, 128 * h:128 * h + 128] = _dot(a.astype(BF16), v).astype(BF16)


def _xattn(q, mkv, tq, layer=None):
    b, t, _ = q.shape
    if layer is None:
        kv_spec = pl.BlockSpec((None, N_MEM, 1024), lambda i, j: (i, 0, 0))
    else:
        kv_spec = pl.BlockSpec((None, None, N_MEM, 2, XA_H, XA_HD), lambda i, j: (layer, i, 0, 0, 0, 0))
    return pl.pallas_call(
        _xattn_body, grid=(b, t // tq),
        in_specs=[pl.BlockSpec((None, tq, 512), lambda i, j: (i, j, 0)), kv_spec],
        out_specs=pl.BlockSpec((None, tq, 512), lambda i, j: (i, j, 0)),
        out_shape=jax.ShapeDtypeStruct((b, t, 512), BF16),
        compiler_params=_cparams(("parallel", "arbitrary")), name="xattn",
    )(q, mkv)


def _moe_body(x_ref, a_ref, wo_ref, g_ref, rwh_ref, rwl_ref, rb_ref, wg_ref, wu_ref, wd_ref, o_ref,
              zn_s, comb_s, acc_s):
    e = pl.program_id(1)
    lane = lax.broadcasted_iota(I32, (1, LANES), 1)

    @pl.when(e == 0)
    def _():
        x1 = x_ref[...] + _dot(a_ref[...], wo_ref[...])
        acc_s[...] = x1
        z = _rms(x1, g_ref[...])
        zh, zl = _split2(z)
        zn_s[...] = zh
        logits = _dot(zh, rwh_ref[...]) + _dot(zh, rwl_ref[...]) + _dot(zl, rwh_ref[...]) + rb_ref[...]
        is_g = (lane >= N_EXP) & (lane < N_EXP + N_GROUPS)
        gl = jnp.where(is_g, logits, -jnp.inf)
        gmax = jnp.max(gl, axis=-1, keepdims=True)
        gw = 1.0 / jnp.sum(jnp.exp(gl - gmax), axis=-1, keepdims=True)
        lanef = lane.astype(F32)
        grpf = (lane // EXP_PER_GROUP).astype(F32)
        gsel = jnp.min(jnp.where(gl == gmax, lanef, 1e6), axis=-1, keepdims=True) - N_EXP
        in_grp = (lane < N_EXP) & (grpf == gsel)
        le = jnp.where(in_grp, logits, -jnp.inf)
        v1 = jnp.max(le, axis=-1, keepdims=True)
        i1 = jnp.min(jnp.where(le == v1, lanef, 1e6), axis=-1, keepdims=True)
        le2 = jnp.where(lanef == i1, -jnp.inf, le)
        v2 = jnp.max(le2, axis=-1, keepdims=True)
        i2 = jnp.min(jnp.where(le2 == v2, lanef, 1e6), axis=-1, keepdims=True)
        e2 = jnp.exp(v2 - v1)
        w1 = 1.0 / (1.0 + e2)
        w2 = e2 / (1.0 + e2)
        comb_s[...] = gw * (jnp.where(lanef == i1, w1, 0.0) + jnp.where(lanef == i2, w2, 0.0))

    zn = zn_s[...]
    hg = _dot(zn, wg_ref[...])
    hu = _dot(zn, wu_ref[...])
    ce = jnp.sum(jnp.where(lane == e, comb_s[...], 0.0), axis=-1, keepdims=True)
    h = (hg * _sigmoid(hg)) * hu * ce
    acc_s[...] += _dot(h.astype(BF16), wd_ref[...])

    @pl.when(e == pl.num_programs(1) - 1)
    def _():
        o_ref[...] = acc_s[...]


def _moe(x2, a_bf, wo, g, rwh, rwl, rb, wg, wu, wd, layer, tm):
    n = x2.shape[0]
    full = lambda a: pl.BlockSpec(a.shape, lambda i, e: (0,) * a.ndim)
    return pl.pallas_call(
        _moe_body, grid=(n // tm, N_EXP),
        in_specs=[pl.BlockSpec((tm, 1024), lambda i, e: (i, 0)), pl.BlockSpec((tm, XA_INNER), lambda i, e: (i, 0)),
                  full(wo), full(g), full(rwh), full(rwl), full(rb),
                  pl.BlockSpec((None, None, 1024, EXP_FF), lambda i, e: (layer, e, 0, 0)),
                  pl.BlockSpec((None, None, 1024, EXP_FF), lambda i, e: (layer, e, 0, 0)),
                  pl.BlockSpec((None, None, EXP_FF, 1024), lambda i, e: (layer, e, 0, 0))],
        out_specs=pl.BlockSpec((tm, 1024), lambda i, e: (i, 0)),
        out_shape=jax.ShapeDtypeStruct((n, 1024), F32),
        scratch_shapes=[pltpu.VMEM((tm, 1024), BF16), pltpu.VMEM((tm, 128), F32), pltpu.VMEM((tm, 1024), F32)],
        compiler_params=_cparams(("parallel", "arbitrary")), name="moe",
    )(x2, a_bf, wo, g, rwh, rwl, rb, wg, wu, wd)


def _q_perm():
    idx = []
    for r in range(NSA_R):
        for g in range(NSA_G):
            h = g * NSA_R + r
            idx.extend(range(64 * h, 64 * h + 64))
    return np.asarray(idx)


def _prep_layer(w, l):
    p = {"layer": l}
    win = w["w_in"][l]
    qp = _q_perm()
    o_q = 1032 + 512
    cols = [win[:, 0:1024], win[:, 1032:1032 + 512], win[:, o_q:o_q + 512][:, qp],
            win[:, o_q + 512:o_q + 512 + 768], win[:, 1024:1032], win[:, o_q + 1280:o_q + 1304],
            jnp.zeros((D_MODEL, 128 - 32), F32)]
    p["w_in"] = jnp.concatenate(cols, axis=1).astype(BF16)
    p["w_st"] = win[:, 1024:1032].T.astype(BF16)
    p["norm_mix"] = w["norm_mix"][l][None]
    bias8 = jnp.concatenate([w["ml_i_bias"][l], w["ml_f_bias"][l]])
    p["ml_bcol"] = jnp.zeros((1, 128), F32).at[0, 0:8].set(bias8)
    p["ml_brow"] = bias8[:, None]
    p["conv_w"] = w["conv_w"][l]
    p["conv_b"] = w["conv_b"][l][None]
    bd = lambda m: jax.scipy.linalg.block_diag(*[m[i] for i in range(m.shape[0])])
    p["lru_wa"] = bd(w["lru_wa"][l]).astype(BF16)
    p["lru_wx"] = bd(w["lru_wx"][l]).astype(BF16)
    p["lru_ba"] = w["lru_ba"][l][None]
    p["lru_bx"] = w["lru_bx"][l][None]
    p["lru_lambda"] = w["lru_lambda"][l][None]
    cw = {}
    eye4 = jnp.eye(PAGE // CMP_BLK, dtype=F32)
    pw, ppos, pb1, pw2 = [], [], [], []
    for c, nm in ((0, "k"), (1, "v")):
        w1 = w["phi_w1"][l, c].reshape(CMP_BLK, HD, CMP_HID)
        z = jnp.zeros_like(w1)
        cw["bd" + nm] = jnp.concatenate([jnp.concatenate([w1, z], axis=2),
                                         jnp.concatenate([z, w1], axis=2)], axis=1).astype(BF16)
        cw["pos" + nm] = jnp.tile(w["cmp_pos"][l, c], (1, 2))
        cw["b1" + nm] = jnp.tile(w["phi_b1"][l, c], 2)[None]
        w2 = w["phi_w2"][l, c]
        z2 = jnp.zeros_like(w2)
        cw["w2" + nm] = jnp.concatenate([jnp.concatenate([w2, z2], axis=1),
                                         jnp.concatenate([z2, w2], axis=1)], axis=0).astype(BF16)
        w3 = w1.transpose(1, 0, 2).reshape(HD // 2, 2, CMP_BLK, CMP_HID)
        pw.append(jnp.einsum("pdrh,bc->pdbrch", w3, eye4).reshape(HD // 2, 256, 512).astype(BF16))
        pos3 = w["cmp_pos"][l, c].T.reshape(HD // 2, 2, 1, CMP_BLK)
        ppos.append(jnp.broadcast_to(pos3, (HD // 2, 2, 4, CMP_BLK)).reshape(HD // 2, 256))
        pb1.append(jnp.tile(w["phi_b1"][l, c], 4)[None])
        pw2.append(jnp.kron(eye4, w2).astype(BF16))
    cw["pw"] = jnp.stack(pw)
    cw["ppos"] = jnp.stack(ppos)
    cw["pb1"] = jnp.stack(pb1)
    cw["pw2"] = jnp.stack(pw2)
    p["cmp"] = cw
    perm = np.concatenate([np.arange(512), 512 + qp])
    p["mix_norm"] = w["mix_norm"][l][perm][None]
    p["w_out"] = w["w_out"][l][perm, :].astype(BF16)
    p["norm_xa"] = w["norm_xa"][l][None]
    p["norm_mem"] = w["norm_mem"][l][None]
    p["xa_wq"] = w["xa_wq"][l].astype(BF16)
    p["xa_wkv"] = w["xa_wkv"][l].astype(BF16)
    p["xa_wo"] = w["xa_wo"][l].astype(BF16)
    p["norm_ffn"] = w["norm_ffn"][l][None]
    rw = jnp.concatenate([w["router_ew"][l], w["router_gw"][l], jnp.zeros((D_MODEL, 128 - 20), F32)], axis=1)
    rwh = rw.astype(BF16)
    p["rwh"] = rwh
    p["rwl"] = (rw - rwh.astype(F32)).astype(BF16)
    p["rb"] = jnp.concatenate([w["router_eb"][l], w["router_gb"][l], jnp.zeros((128 - 20,), F32)])[None]
    return p


def _constants(T):
    c = {}
    head = np.arange(1024) // HD
    pool = np.zeros((1024, 128), np.float32)
    pool[np.arange(1024), head] = 1.0 / HD
    poolt = np.zeros((128, 1024), np.float32)
    poolt[head, np.arange(1024)] = 1.0
    c["pool"] = jnp.asarray(pool, BF16)
    c["poolt"] = jnp.asarray(poolt, BF16)
    gexp = np.zeros((3, 128, 512), np.float32)
    for cc in range(3):
        for g in range(NSA_G):
            for r in range(NSA_R):
                gexp[cc, 8 + cc * 8 + g * 4 + r, 128 * r + 64 * g:128 * r + 64 * g + 64] = 1.0
    c["gexp"] = jnp.asarray(gexp, BF16)
    kpos = np.arange(T)
    kaug = np.zeros((T, 128), np.float32)
    kaug[:, AUG_HI] = kpos // SLC_BLK
    kaug[:, AUG_LO] = kpos % SLC_BLK
    kaug[:, AUG_ONE] = 1.0
    kaug[kpos, AUG_SEL + kpos // SLC_BLK] = 1.0
    c["kaug"] = jnp.asarray(kaug, BF16)
    return c


def _even_odd(a):
    return jnp.concatenate([a[:, 0::2], a[:, 1::2]], axis=1)


def _pad_rows(a, rows):
    return jnp.pad(a, ((0, 0), (0, rows - a.shape[1]), (0, 0)))


def _dense_tail(x2, q2, p, ew, B, T, mkv, tq, tm_moe, cache_layer=None):
    n = x2.shape[0]
    q = q2.reshape(B, T, XA_INNER)
    if T < tq:
        o = _xattn(_pad_rows(q, tq), mkv, tq, cache_layer)[:, 0:T]
    else:
        o = _xattn(q, mkv, tq, cache_layer)
    return _moe(x2, o.reshape(n, XA_INNER), p["xa_wo"], p["norm_ffn"], p["rwh"], p["rwl"], p["rb"],
                ew[0], ew[1], ew[2], p["layer"], tm_moe)


def _layer_prompt(x2, p, c, ew, B, T, mem2):
    n = B * T
    ml, lru, q, kv, win, sm, smt, kvs_bf, win_bf = _in_proj(x2, p["norm_mix"], p["w_in"], p["w_st"], 256)
    smt_b = smt.reshape(8, B, T).transpose(1, 0, 2)
    ya, C, nn, mm = _mlstm(ml.reshape(B, T, 1024), sm.reshape(B, T, 128), smt_b, p["ml_bcol"], p["ml_brow"],
                           jnp.zeros((B, 4, 64, 64), F32), jnp.zeros((B, 4, 1, 64), F32),
                           jnp.zeros((B, 1, 128), F32), ML_CHUNK, ML_CHUNK)
    u_tm = lru.reshape(B, T, 512).transpose(1, 0, 2)
    yb_tm, tail_tm, h_last = _lru(u_tm, jnp.zeros((3, B, 256), F32), jnp.zeros((B, 256), F32), p["conv_w"],
                                  p["conv_b"], p["lru_wa"], p["lru_ba"], p["lru_wx"], p["lru_bx"],
                                  p["lru_lambda"], 256)
    yb = yb_tm.transpose(1, 0, 2).reshape(n, 256)
    n_pages = n // PAGE
    kc, vc = _compress_rows(jnp.arange(n_pages, dtype=I32), kv.reshape(n_pages, PAGE, 512), p["cmp"], 16)
    nc = T // CMP_BLK
    kc_eo = _even_odd(kc.reshape(B, nc, 128))
    vc_eo = _even_odd(vc.reshape(B, nc, 128))
    yc = _nsa_prompt(q.reshape(B, T, 512), sm.reshape(B, T, 128), kc_eo, vc_eo, kvs_bf.reshape(B, T, 256),
                     win_bf.reshape(B, T, 256), c["kaug"], c["gexp"])
    x2, xq = _mix_out(ya.reshape(n, 256), yb, yc.reshape(n, 512), x2, p["mix_norm"], p["w_out"], c["pool"],
                      c["poolt"], p["norm_xa"], p["xa_wq"], 256)
    mkv = _norm_mm(mem2, p["norm_mem"], p["xa_wkv"], 256, F32)
    x2 = _dense_tail(x2, xq, p, ew, B, T, mkv.reshape(B, N_MEM, 1024), 512, 1024)
    st = (kv.reshape(B, T, 4, NSA_G, HD), win.reshape(B, T, 2, NSA_G, HD)[:, T - WINDOW:],
          C, nn.reshape(B, 4, 64), mm[:, 0, 0:4], h_last, tail_tm.transpose(1, 0, 2))
    return x2, st, mkv.reshape(B, N_MEM, 2, XA_H, XA_HD)


def _layer_sample(x2, p, c, ew, B, T, pool_t, pt_flat, win_buf, win_t, C0, n0, m0, conv0, h0, mkv, past):
    n = B * T
    _pad_rows8 = lambda a: _pad_rows(a, 8)
    ml, lru, q, kv, win, sm, smt, _, _ = _in_proj(x2, p["norm_mix"], p["w_in"], p["w_st"], n)
    sm8 = _pad_rows8(sm.reshape(B, T, 128))
    smt_b = jnp.pad(smt.reshape(8, B, T).transpose(1, 0, 2), ((0, 0), (0, 0), (0, 8 - T)))
    m0p = jnp.pad(m0[:, None, :], ((0, 0), (0, 0), (0, 128 - ML_H)))
    ya, C, nn, mm = _mlstm(_pad_rows8(ml.reshape(B, T, 1024)), sm8, smt_b, p["ml_bcol"], p["ml_brow"],
                           C0, n0[:, :, None, :], m0p, 8, T)
    ya = ya[:, 0:T]
    u_tm = lru.reshape(B, T, 512).transpose(1, 0, 2)
    yb_tm, tail_tm, h_last = _lru(u_tm, conv0.transpose(1, 0, 2), h0, p["conv_w"], p["conv_b"], p["lru_wa"],
                                  p["lru_ba"], p["lru_wx"], p["lru_bx"], p["lru_lambda"], T)
    yb = yb_tm.transpose(1, 0, 2).reshape(n, 256)
    n_pages = past // PAGE
    kvc = _compress_pages(pt_flat, pool_t, p["cmp"], CMP_STEP_PAGES)
    wb = win_buf.shape[1]
    q8 = _pad_rows8(q.reshape(B, T, 512).astype(F32))
    win3 = win.reshape(B, T, 256)
    oc, ow, sc = _nsa_sample_sel(q8, kvc, win_t, p["layer"], _pad_rows8(win3), past)
    idx = _topk(sc.reshape(B * NSA_G * 8, 3 * LANES), n_pages)[:, 0:N_SEL].reshape(B, NSA_G, 8, N_SEL)
    kv3 = kv.reshape(B, T, 512)
    yc8 = _nsa_sample_slc(idx, pt_flat, pool_t, q8, _pad_rows8(kv3[:, :, 256:512]), sm8, oc, ow,
                          c["gexp"], T, past)
    yc = yc8[:, 0:T].reshape(n, 512)
    x2, xq = _mix_out(ya.reshape(n, 256), yb, yc, x2, p["mix_norm"], p["w_out"], c["pool"], c["poolt"],
                      p["norm_xa"], p["xa_wq"], n)
    x2 = _dense_tail(x2, xq, p, ew, B, T, mkv, 16, n, cache_layer=p["layer"])
    win_new = jnp.concatenate([win_buf.reshape(B, wb, 256), win3], axis=1)[:, T:]
    st = (kv.reshape(B, T, 4, NSA_G, HD), win_new.reshape(B, wb, 2, NSA_G, HD), C, nn.reshape(B, 4, 64),
          mm[:, 0, 0:4], h_last, tail_tm.transpose(1, 0, 2))
    return x2, st


def kernel(x_prompt, x_sample, cache_nsa_kv, state_nsa_win, state_mlstm_C, state_mlstm_n, state_mlstm_m, state_rglru_h, state_rglru_conv, cache_mem_kv, page_table, mem_prompt, norm_mix, w_in, ml_i_bias, ml_f_bias, conv_w, conv_b, lru_wa, lru_ba, lru_wx, lru_bx, lru_lambda, phi_w1, phi_b1, phi_w2, cmp_pos, mix_norm, w_out, norm_xa, norm_mem, xa_wq, xa_wkv, xa_wo, norm_ffn, router_gw, router_gb, router_ew, router_eb, exp_w_gate, exp_w_up, exp_w_down, final_norm):
    w = dict(norm_mix=norm_mix, w_in=w_in, ml_i_bias=ml_i_bias, ml_f_bias=ml_f_bias, conv_w=conv_w,
             conv_b=conv_b, lru_wa=lru_wa, lru_ba=lru_ba, lru_wx=lru_wx, lru_bx=lru_bx, lru_lambda=lru_lambda,
             phi_w1=phi_w1, phi_b1=phi_b1, phi_w2=phi_w2, cmp_pos=cmp_pos, mix_norm=mix_norm, w_out=w_out,
             norm_xa=norm_xa, norm_mem=norm_mem, xa_wq=xa_wq, xa_wkv=xa_wkv, xa_wo=xa_wo, norm_ffn=norm_ffn,
             router_gw=router_gw, router_gb=router_gb, router_ew=router_ew, router_eb=router_eb)
    depth = w_in.shape[0]
    B, T, _ = x_prompt.shape
    BS, TS, _ = x_sample.shape
    n_pages = page_table.shape[1]
    past = n_pages * PAGE
    n_phys = cache_nsa_kv.shape[1]
    consts = _constants(T)
    layers = [_prep_layer(w, l) for l in range(depth)]
    ew = (exp_w_gate.astype(BF16), exp_w_up.astype(BF16), exp_w_down.astype(BF16))
    fin = final_norm[None]

    x2 = x_prompt.reshape(B * T, D_MODEL)
    mem2 = mem_prompt.reshape(B * N_MEM, D_MODEL)
    outs_p, mem_p = [], []
    for l in range(depth):
        x2, st, mkv = _layer_prompt(x2, layers[l], consts, ew, B, T, mem2)
        outs_p.append(st)
        mem_p.append(mkv)
    y_prompt = _final_norm(x2, fin, 256).reshape(B, T, D_MODEL)

    pool_t = jnp.transpose(cache_nsa_kv, (0, 1, 3, 4, 5, 2)).reshape(depth * n_phys * 4 * NSA_G, HD, PAGE)
    win_t = jnp.transpose(state_nsa_win, (0, 1, 3, 4, 5, 2))
    xs = x_sample.reshape(BS * TS, D_MODEL)
    outs_s = []
    for l in range(depth):
        pt_flat = page_table.reshape(-1) + l * n_phys
        xs, st = _layer_sample(xs, layers[l], consts, ew, BS, TS, pool_t, pt_flat, state_nsa_win[l], win_t,
                               state_mlstm_C[l], state_mlstm_n[l], state_mlstm_m[l], state_rglru_conv[l],
                               state_rglru_h[l], cache_mem_kv, past)
        outs_s.append(st)
    y_sample = _final_norm(xs, fin, BS * TS).reshape(BS, TS, D_MODEL)

    sp = [jnp.stack(a) for a in zip(*outs_p)]
    ss = [jnp.stack(a) for a in zip(*outs_s)]
    return (y_prompt, y_sample, sp[0], ss[0], sp[1], ss[1], sp[2], ss[2], sp[3], ss[3], sp[4], ss[4],
            sp[5], ss[5], sp[6], ss[6], jnp.stack(mem_p))
```

```python
import functools
import math

import jax
import jax.numpy as jnp
import numpy as np
from jax import lax
from jax.experimental import pallas as pl
from jax.experimental.pallas import tpu as pltpu

F32 = jnp.float32
BF16 = jnp.bfloat16
I32 = jnp.int32

D_MODEL = 1024
DEPTH = 2
PAGE = 128
HD = 64
ML_W = 256
ML_H = 4
LRU_W = 256
LRU_C = 8.0
CONV_W = 4
NSA_W = 512
NSA_H = 8
NSA_G = 2
NSA_R = 4
CMP_BLK = 32
CMP_HID = 128
SLC_BLK = 64
N_SEL = 16
WINDOW = 512
FORCE_BONUS = 100.0
XA_H = 4
XA_HD = 128
XA_INNER = 512
N_MEM = 256
N_GROUPS = 4
EXP_PER_GROUP = 4
N_EXP = 16
EXP_FF = 256
EPS = 1e-6
NEG = -1e30
SLOPES = tuple(2.0 ** (-(h + 1)) for h in range(NSA_H))

LANES = 128
SUBLANES = 8
VMEM_LIMIT = 56 * 1024 * 1024

IN_ML = (0, 1024)
IN_LRU = (1024, 1536)
IN_Q = (1536, 2048)
IN_KV = (2048, 2560)
IN_WIN = (2560, 2816)
IN_SMALL = (2816, 2944)
IN_COLS_P = 2944

AUG_HI = 0
AUG_LO = 1
AUG_ONE = 2
AUG_SEL = 8
KEY_CHUNK = 512
ML_BATCH_ROWS = 1
ML_CHUNK = 256
CMP_STEP_PAGES = 128


def _cparams(sem):
    return pltpu.CompilerParams(dimension_semantics=sem, vmem_limit_bytes=VMEM_LIMIT)


def _dot(a, b):
    return jnp.dot(a, b, preferred_element_type=F32)


def _dot_nt(a, b):
    return lax.dot_general(a, b, (((1,), (1,)), ((), ())), preferred_element_type=F32)


def _dot_tn(a, b):
    return lax.dot_general(a, b, (((0,), (0,)), ((), ())), preferred_element_type=F32)


def _split2(x):
    hi = x.astype(BF16)
    lo = (x - hi.astype(F32)).astype(BF16)
    return hi, lo


def _dot_split(x, w_bf):
    hi, lo = _split2(x)
    return _dot(hi, w_bf) + _dot(lo, w_bf)


def _split3(x):
    hi = x.astype(BF16)
    r1 = x - hi.astype(F32)
    mid = r1.astype(BF16)
    lo = (r1 - mid.astype(F32)).astype(BF16)
    return hi, mid, lo


def _dot3_rhs(w_bf, x):
    hi, mid, lo = _split3(x)
    return _dot(w_bf, hi) + _dot(w_bf, mid) + _dot(w_bf, lo)


def _dot3_lhs(x, w_bf):
    hi, mid, lo = _split3(x)
    return _dot(hi, w_bf) + _dot(mid, w_bf) + _dot(lo, w_bf)


def _sigmoid(x):
    return 1.0 / (1.0 + jnp.exp(-x))


def _gelu(x):
    return 0.5 * x * (1.0 + jnp.tanh(0.7978845608028654 * (x + 0.044715 * (x * x * x))))


def _softplus(x):
    return jnp.maximum(x, 0.0) + jnp.log1p(jnp.exp(-jnp.abs(x)))


def _log_sigmoid(x):
    return -_softplus(-x)


def _rms(x, g):
    return x * lax.rsqrt(jnp.mean(x * x, axis=-1, keepdims=True) + EPS) * g


def _masked_softmax(s, mask):
    sm = jnp.where(mask, s, NEG)
    e = jnp.exp(sm - jnp.max(sm, axis=-1, keepdims=True))
    p = e / jnp.sum(e, axis=-1, keepdims=True)
    return jnp.where(mask, p, 0.0)


def _in_proj_body(x_ref, g_ref, w_ref, wst_ref, ml_ref, lru_ref, q_ref, kv_ref, win_ref, sm_ref,
                  smt_ref, kvsb_ref, winb_ref):
    hb = _rms(x_ref[...], g_ref[...]).astype(BF16)

    def mm(rng):
        return _dot(hb, w_ref[:, rng[0]:rng[1]])

    ml_ref[...] = mm(IN_ML)
    lru_ref[...] = mm(IN_LRU)
    q_ref[...] = mm(IN_Q).astype(BF16)
    kv = mm(IN_KV)
    kv_ref[...] = kv
    kvsb_ref[...] = kv[:, 256:512].astype(BF16)
    win = mm(IN_WIN)
    win_ref[...] = win
    winb_ref[...] = win.astype(BF16)
    sm_ref[...] = mm(IN_SMALL)
    smt_ref[...] = _dot_nt(wst_ref[...], hb)


def _in_proj(x2, g, w_p, w_st, tm):
    n = x2.shape[0]
    row = lambda w: pl.BlockSpec((tm, w), lambda i: (i, 0))
    full = lambda a: pl.BlockSpec(a.shape, lambda i: (0,) * a.ndim)
    shapes = [(1024, F32), (512, F32), (512, BF16), (512, F32), (256, F32), (128, F32)]
    out_shape = [jax.ShapeDtypeStruct((n, w), dt) for w, dt in shapes]
    out_specs = [row(w) for w, _ in shapes]
    out_shape += [jax.ShapeDtypeStruct((8, n), F32), jax.ShapeDtypeStruct((n, 256), BF16),
                  jax.ShapeDtypeStruct((n, 256), BF16)]
    out_specs += [pl.BlockSpec((8, tm), lambda i: (0, i)), row(256), row(256)]
    return pl.pallas_call(
        _in_proj_body, grid=(n // tm,),
        in_specs=[row(D_MODEL), full(g), full(w_p), full(w_st)],
        out_specs=out_specs, out_shape=out_shape,
        compiler_params=_cparams(("parallel",)), name="in_proj",
    )(x2, g, w_p, w_st)


def _norm_mm_body(x_ref, g_ref, w_ref, o_ref):
    o_ref[...] = _dot(_rms(x_ref[...], g_ref[...]).astype(BF16), w_ref[...]).astype(o_ref.dtype)


def _norm_mm(x2, g, w_bf, tm, out_dtype):
    n, k = x2.shape
    m = w_bf.shape[1]
    return pl.pallas_call(
        _norm_mm_body, grid=(n // tm,),
        in_specs=[pl.BlockSpec((tm, k), lambda i: (i, 0)), pl.BlockSpec((1, k), lambda i: (0, 0)),
                  pl.BlockSpec((k, m), lambda i: (0, 0))],
        out_specs=pl.BlockSpec((tm, m), lambda i: (i, 0)),
        out_shape=jax.ShapeDtypeStruct((n, m), out_dtype),
        compiler_params=_cparams(("parallel",)), name="norm_mm",
    )(x2, g, w_bf)


def _mlstm_body(u_ref, sm_ref, smt_ref, bcol_ref, brow_ref, c0_ref, n0_ref, m0_ref,
                y_ref, c_ref, n_ref, m_ref, c_s, n_s, m_s, *, L, t_real, nbb):
    ci = pl.program_id(1)

    @pl.when(ci == 0)
    def _():
        c_s[...] = c0_ref[...]
        n_s[...] = n0_ref[...]
        m_s[...] = m0_ref[...]

    row = lax.broadcasted_iota(I32, (L, L), 0)
    col = lax.broadcasted_iota(I32, (L, L), 1)
    causal = col <= row
    lane = lax.broadcasted_iota(I32, (1, LANES), 1)
    real_col = lax.broadcasted_iota(I32, (L, 1), 0) < t_real
    real_row = lax.broadcasted_iota(I32, (1, L), 1) < t_real
    tri_lo = jnp.where(causal, 1.0, 0.0).astype(BF16)
    tri_up = jnp.where(row <= col, 1.0, 0.0).astype(BF16)
    ones_v = jnp.ones((L, HD), BF16)
    zpad_cn = jnp.zeros((LANES - HD - 1, HD), F32)
    for bb in range(nbb):
        sm = sm_ref[bb] + bcol_ref[...]
        smt = smt_ref[bb] + brow_ref[...]
        m_all = m_s[bb]
        m_next = m_all
        b_cols = _dot3_rhs(tri_lo, jnp.where(real_col, _log_sigmoid(sm), 0.0))
        b_rows = _dot3_lhs(jnp.where(real_row, _log_sigmoid(smt), 0.0), tri_up)
        for h in range(ML_H):
            q = u_ref[bb, :, 64 * h:64 * h + 64]
            k = u_ref[bb, :, 256 + 64 * h:256 + 64 * h + 64] * (HD ** -0.5)
            v = u_ref[bb, :, 512 + 64 * h:512 + 64 * h + 64]
            o = u_ref[bb, :, 768 + 64 * h:768 + 64 * h + 64]
            ig_col = jnp.where(real_col, sm[:, h:h + 1], NEG)
            ig_row = jnp.where(real_row, smt[h:h + 1, :], NEG)
            b_col = b_cols[:, 4 + h:5 + h]
            b_row = b_rows[4 + h:5 + h, :]
            m_prev = m_all[:, h:h + 1]
            log_d = jnp.where(causal, b_col - b_row + ig_row, NEG)
            inter = b_col + m_prev
            m_t = jnp.maximum(inter, jnp.max(log_d, axis=1, keepdims=True))
            w_carry = jnp.exp(inter - m_t)
            qb = q.astype(BF16)
            vb = v.astype(BF16)
            s = _dot_nt(qb, k.astype(BF16)) * jnp.exp(log_d - m_t)
            c_old = c_s[bb, h]
            n_old = n_s[bb, h]
            sv = _dot(s.astype(BF16), jnp.concatenate([vb, ones_v], axis=1))
            cn = jnp.concatenate([c_old, n_old, zpad_cn], axis=0).astype(BF16)
            qc = _dot_nt(qb, cn)
            num = sv[:, 0:HD] + w_carry * qc[:, 0:HD]
            den = sv[:, HD:HD + 1] + w_carry * qc[:, HD:HD + 1]
            hh = num / jnp.maximum(jnp.abs(den), jnp.exp(-m_t))
            y_ref[bb, :, 64 * h:64 * h + 64] = _sigmoid(o) * hh
            b_end = b_col[L - 1:L, :]
            log_w_row = b_end - b_row + ig_row
            m_new = jnp.maximum(b_end + m_prev, jnp.max(log_w_row, axis=1, keepdims=True))
            w_col = jnp.exp(b_end - b_col + ig_col - m_new)
            decay = jnp.exp(b_end + m_prev - m_new)
            c_s[bb, h] = decay * c_old + _dot_tn((v * w_col).astype(BF16), k.astype(BF16))
            n_s[bb, h] = decay * n_old + jnp.sum(k * w_col, axis=0, keepdims=True)
            m_next = jnp.where(lane == h, m_new, m_next)
        m_s[bb] = m_next

    @pl.when(ci == pl.num_programs(1) - 1)
    def _():
        c_ref[...] = c_s[...]
        n_ref[...] = n_s[...]
        m_ref[...] = m_s[...]


def _mlstm(u_ml, u_sm, u_smt, bcol, brow, c0, n0, m0, L, t_real):
    b, t, _ = u_ml.shape
    nc = t // L
    assert t_real == L or nc == 1
    nbb = ML_BATCH_ROWS
    assert b % nbb == 0
    return pl.pallas_call(
        functools.partial(_mlstm_body, L=L, t_real=t_real, nbb=nbb), grid=(b // nbb, nc),
        in_specs=[pl.BlockSpec((nbb, L, 1024), lambda i, c: (i, c, 0)),
                  pl.BlockSpec((nbb, L, 128), lambda i, c: (i, c, 0)),
                  pl.BlockSpec((nbb, 8, L), lambda i, c: (i, 0, c)),
                  pl.BlockSpec((1, 128), lambda i, c: (0, 0)),
                  pl.BlockSpec((8, 1), lambda i, c: (0, 0)),
                  pl.BlockSpec((nbb, 4, 64, 64), lambda i, c: (i, 0, 0, 0)),
                  pl.BlockSpec((nbb, 4, 1, 64), lambda i, c: (i, 0, 0, 0)),
                  pl.BlockSpec((nbb, 1, 128), lambda i, c: (i, 0, 0))],
        out_specs=[pl.BlockSpec((nbb, L, 256), lambda i, c: (i, c, 0)),
                   pl.BlockSpec((nbb, 4, 64, 64), lambda i, c: (i, 0, 0, 0)),
                   pl.BlockSpec((nbb, 4, 1, 64), lambda i, c: (i, 0, 0, 0)),
                   pl.BlockSpec((nbb, 1, 128), lambda i, c: (i, 0, 0))],
        out_shape=[jax.ShapeDtypeStruct((b, t, 256), F32), jax.ShapeDtypeStruct((b, 4, 64, 64), F32),
                   jax.ShapeDtypeStruct((b, 4, 1, 64), F32), jax.ShapeDtypeStruct((b, 1, 128), F32)],
        scratch_shapes=[pltpu.VMEM((nbb, 4, 64, 64), F32), pltpu.VMEM((nbb, 4, 1, 64), F32),
                        pltpu.VMEM((nbb, 1, 128), F32)],
        compiler_params=_cparams(("parallel", "arbitrary")), name="mlstm",
    )(u_ml, u_sm, u_smt, bcol, brow, c0, n0, m0)


def _lru_body(u_ref, cb_ref, h0_ref, cw_ref, cbias_ref, wa_ref, ba_ref, wx_ref, bx_ref, lam_ref,
              y_ref, tail_ref, hl_ref, tail_s, h_s, a_s, hs_s, *, tt, nb):
    i = pl.program_id(0)

    @pl.when(i == 0)
    def _():
        tail_s[...] = cb_ref[...]
        h_s[...] = h0_ref[...]

    x = u_ref[:, :, 0:LRU_W]
    g = u_ref[:, :, LRU_W:2 * LRU_W]
    xe = jnp.concatenate([tail_s[...], x], axis=0)
    xc = cbias_ref[...] + xe[0:tt] * cw_ref[0:1, :]
    for j in range(1, CONV_W):
        xc = xc + xe[j:j + tt] * cw_ref[j:j + 1, :]
    xc2 = xc.reshape(tt * nb, LRU_W)
    xcb = xc2.astype(BF16)
    r = _sigmoid(_dot(xcb, wa_ref[...]) + ba_ref[...])
    ig = _sigmoid(_dot(xcb, wx_ref[...]) + bx_ref[...])
    log_a = -LRU_C * r * _softplus(-lam_ref[...])
    a = jnp.exp(log_a)
    mult = jnp.sqrt(jnp.tanh(-log_a) * (a * a + 1.0))
    a_s[...] = a.reshape(tt, nb, LRU_W)
    hs_s[...] = (mult * (ig * xc2)).reshape(tt, nb, LRU_W)

    def step(t, h):
        h = a_s[t] * h + hs_s[t]
        hs_s[t] = h
        return h

    h_last = lax.fori_loop(0, tt, step, h_s[...], unroll=min(8, tt))
    h_s[...] = h_last
    y_ref[...] = hs_s[...] * _gelu(g)
    tail_s[...] = xe[tt:tt + CONV_W - 1]

    @pl.when(i == pl.num_programs(0) - 1)
    def _():
        tail_ref[...] = tail_s[...]
        hl_ref[...] = h_s[...]


def _lru(u_tm, cb_tm, h0, cw, cbias, wa_bd, ba, wx_bd, bx, lam, tt):
    t, nb, _ = u_tm.shape
    full = lambda a: pl.BlockSpec(a.shape, lambda i: (0,) * a.ndim)
    return pl.pallas_call(
        functools.partial(_lru_body, tt=tt, nb=nb), grid=(t // tt,),
        in_specs=[pl.BlockSpec((tt, nb, 512), lambda i: (i, 0, 0)), full(cb_tm), full(h0), full(cw),
                  full(cbias), full(wa_bd), full(ba), full(wx_bd), full(bx), full(lam)],
        out_specs=[pl.BlockSpec((tt, nb, 256), lambda i: (i, 0, 0)),
                   pl.BlockSpec((3, nb, 256), lambda i: (0, 0, 0)),
                   pl.BlockSpec((nb, 256), lambda i: (0, 0))],
        out_shape=[jax.ShapeDtypeStruct((t, nb, 256), F32), jax.ShapeDtypeStruct((3, nb, 256), F32),
                   jax.ShapeDtypeStruct((nb, 256), F32)],
        scratch_shapes=[pltpu.VMEM((3, nb, 256), F32), pltpu.VMEM((nb, 256), F32),
                        pltpu.VMEM((tt, nb, 256), F32), pltpu.VMEM((tt, nb, 256), F32)],
        compiler_params=_cparams(("arbitrary",)), name="rglru",
    )(u_tm, cb_tm, h0, cw, cbias, wa_bd, ba, wx_bd, bx, lam)


def _compress_rows_body(pt_ref, pool_ref, bdk_ref, bdv_ref, posk_ref, posv_ref, b1k_ref, b1v_ref,
                        w2k_ref, w2v_ref, kc_ref, vc_ref, buf, sem, *, P):
    s = pl.program_id(0)
    ns = pl.num_programs(0)
    nblk = P * (PAGE // CMP_BLK)

    def page_copy(step, slot, p, c):
        return pltpu.make_async_copy(
            pool_ref.at[pt_ref[step * P + p], :, pl.ds(LANES * c, LANES)],
            buf.at[slot, c, pl.ds(p * PAGE, PAGE), :], sem.at[slot])

    def start_all(step, slot):
        def body(p, carry):
            page_copy(step, slot, p, 0).start()
            page_copy(step, slot, p, 1).start()
            return carry
        lax.fori_loop(0, P, body, 0)

    def wait_all(step, slot):
        def body(p, carry):
            page_copy(step, slot, p, 0).wait()
            page_copy(step, slot, p, 1).wait()
            return carry
        lax.fori_loop(0, P, body, 0)

    @pl.when(s == 0)
    def _():
        start_all(0, 0)

    slot = lax.rem(s, 2)

    @pl.when(s + 1 < ns)
    def _():
        start_all(s + 1, 1 - slot)

    wait_all(s, slot)

    acc_k = jnp.zeros((nblk, 256), F32)
    acc_v = jnp.zeros((nblk, 256), F32)
    for r in range(CMP_BLK):
        rows = pl.ds(r, nblk, stride=CMP_BLK)
        xk = (buf[slot, 0, rows, :] + posk_ref[r:r + 1, :]).astype(BF16)
        xv = (buf[slot, 1, rows, :] + posv_ref[r:r + 1, :]).astype(BF16)
        acc_k = acc_k + _dot(xk, bdk_ref[r])
        acc_v = acc_v + _dot(xv, bdv_ref[r])
    kc_ref[...] = _dot(_gelu(acc_k + b1k_ref[...]).astype(BF16), w2k_ref[...])
    vc_ref[...] = _dot(_gelu(acc_v + b1v_ref[...]).astype(BF16), w2v_ref[...])


def _compress_rows(pt_flat, pool3, cw, P):
    n_pages = pt_flat.shape[0]
    nblk = P * (PAGE // CMP_BLK)
    full = lambda a: pl.BlockSpec(a.shape, lambda i, pt: (0,) * a.ndim)
    ws = (cw["bdk"], cw["bdv"], cw["posk"], cw["posv"], cw["b1k"], cw["b1v"], cw["w2k"], cw["w2v"])
    grid_spec = pltpu.PrefetchScalarGridSpec(
        num_scalar_prefetch=1, grid=(n_pages // P,),
        in_specs=[pl.BlockSpec(memory_space=pl.ANY)] + [full(a) for a in ws],
        out_specs=[pl.BlockSpec((nblk, 128), lambda i, pt: (i, 0)),
                   pl.BlockSpec((nblk, 128), lambda i, pt: (i, 0))],
        scratch_shapes=[pltpu.VMEM((2, 2, P * PAGE, LANES), F32), pltpu.SemaphoreType.DMA((2,))])
    return pl.pallas_call(
        functools.partial(_compress_rows_body, P=P), grid_spec=grid_spec,
        out_shape=[jax.ShapeDtypeStruct((n_pages * 4, 128), F32)] * 2,
        compiler_params=_cparams(("arbitrary",)), name="compress_rows",
    )(pt_flat, pool3, *ws)


def _compress_pages_body(pt_ref, pool_ref, w_ref, pos_ref, b1_ref, w2_ref, o_ref, buf, sem, *, P, nstep):
    c = pl.program_id(0)
    s = pl.program_id(1)
    lin = c * nstep + s
    PG = NSA_G * P

    def tile_copy(cc, step, slot, p):
        return pltpu.make_async_copy(
            pool_ref.at[pl.ds((pt_ref[step * P + p] * 4 + cc) * NSA_G, NSA_G)],
            buf.at[slot, :, :, p, :], sem.at[slot])

    def start_all(cc, step, slot):
        def body(p, carry):
            tile_copy(cc, step, slot, p).start()
            return carry
        lax.fori_loop(0, P, body, 0, unroll=4)

    def wait_all(slot):
        pltpu.make_async_copy(buf.at[slot], buf.at[slot], sem.at[slot]).wait()

    @pl.when(lin == 0)
    def _():
        start_all(0, 0, 0)

    slot = lax.rem(lin, 2)
    nxt = lin + 1

    @pl.when(nxt < 2 * nstep)
    def _():
        start_all(lax.div(nxt, nstep), lax.rem(nxt, nstep), 1 - slot)

    wait_all(slot)

    acc = jnp.zeros((PG, 4 * CMP_HID), F32)
    for dp in range(HD // 2):
        chan = [jnp.concatenate([buf[slot, 0, d], buf[slot, 1, d]], axis=0) for d in (2 * dp, 2 * dp + 1)]
        lhs = jnp.concatenate(chan, axis=1) + pos_ref[dp:dp + 1, :]
        acc = acc + _dot(lhs.astype(BF16), w_ref[dp])
    h = _gelu(acc + b1_ref[...])
    out = _dot(h.astype(BF16), w2_ref[...])
    for blk in range(PAGE // CMP_BLK):
        cols = slice(HD * blk, HD * blk + HD)
        o_ref[blk] = jnp.concatenate([out[0:P, cols], out[P:2 * P, cols]], axis=1)


def _compress_pages(pt_flat, pool_t, cw, P):
    n_pages = pt_flat.shape[0]
    nstep = n_pages // P
    PG = NSA_G * P
    grid_spec = pltpu.PrefetchScalarGridSpec(
        num_scalar_prefetch=1, grid=(2, nstep),
        in_specs=[pl.BlockSpec(memory_space=pl.ANY),
                  pl.BlockSpec((None, HD // 2, 256, 512), lambda c, s, pt: (c, 0, 0, 0)),
                  pl.BlockSpec((None, HD // 2, 256), lambda c, s, pt: (c, 0, 0)),
                  pl.BlockSpec((None, 1, 512), lambda c, s, pt: (c, 0, 0)),
                  pl.BlockSpec((None, 512, 256), lambda c, s, pt: (c, 0, 0))],
        out_specs=pl.BlockSpec((None, PAGE // CMP_BLK, P, LANES), lambda c, s, pt: (c, 0, s, 0)),
        scratch_shapes=[pltpu.VMEM((2, NSA_G, HD, P, LANES), F32), pltpu.SemaphoreType.DMA((2,))])
    return pl.pallas_call(
        functools.partial(_compress_pages_body, P=P, nstep=nstep), grid_spec=grid_spec,
        out_shape=jax.ShapeDtypeStruct((2, PAGE // CMP_BLK, n_pages, LANES), F32),
        compiler_params=_cparams(("arbitrary", "arbitrary")), name="compress_pages",
    )(pt_flat, pool_t, cw["pw"], cw["ppos"], cw["pb1"], cw["pw2"])


def _half_mask(g):
    lane = lax.broadcasted_iota(I32, (1, LANES), 1)
    return (lane >= 64 * g) & (lane < 64 * g + 64)


def _gate_expand(gates, gexp_ref):
    hi, lo = _split2(gates)
    return [_dot(hi, gexp_ref[c]) + _dot(lo, gexp_ref[c]) for c in range(3)]


def _nsa_prompt_body(q_ref, sm_ref, kc_ref, vc_ref, kvs_ref, kaug_ref, win_ref, gexp_ref, o_ref,
                     s_scr, mx_scr, acc_scr, *, T):
    QT = 128
    M = NSA_H * QT
    CK = KEY_CHUNK
    q0 = pl.program_id(1) * QT
    lane = lax.broadcasted_iota(I32, (1, LANES), 1)
    left = lane < 64
    rowm = lax.broadcasted_iota(I32, (M, 1), 0)
    qposm = q0 + (rowm & (QT - 1))
    slopem = jnp.zeros((M, 1), F32)
    for h in range(NSA_H):
        slopem = jnp.where((rowm >= QT * h) & (rowm < QT * (h + 1)), SLOPES[h], slopem)

    nce = T // CMP_BLK // 2
    blk = jnp.where(lane < nce, 2 * lane, 2 * (lane - nce) + 1)
    c_end = jnp.where(lane < 2 * nce, blk * CMP_BLK + (CMP_BLK - 1), 1 << 30)
    dist_c = qposm - c_end
    mask_c = dist_c >= 0
    zpad = jnp.zeros((LANES - 2 * nce, LANES), F32)
    kc = jnp.concatenate([kc_ref[...], zpad], axis=0).astype(BF16)
    vc = jnp.concatenate([vc_ref[...], zpad], axis=0).astype(BF16)

    qs = [jnp.where(_half_mask(g), q_ref[:, 128 * r:128 * r + 128], 0)
          for g in range(NSA_G) for r in range(NSA_R)]
    qm = jnp.concatenate(qs, axis=0)

    s = _dot_nt(qm, kc) * (HD ** -0.5) - slopem * dist_c.astype(F32)
    p = _masked_softmax(s, mask_c)
    o_cmp = _dot(p.astype(BF16), vc)

    ns = T // SLC_BLK
    jrow = lax.broadcasted_iota(I32, (ns, 1), 0)
    cur = (q0 + lane) // SLC_BLK
    in_sel = (lane >= AUG_SEL) & (lane < AUG_SEL + ns)
    valid = jrow <= cur
    forced = (jrow == 0) | (jrow == cur) | (jrow == cur - 1)
    sel_bias = []
    for g in range(NSA_G):
        b0 = g * NSA_R * QT
        imp = p[b0:b0 + QT] + p[b0 + QT:b0 + 2 * QT] + p[b0 + 2 * QT:b0 + 3 * QT] + p[b0 + 3 * QT:b0 + 4 * QT]
        imp_t = imp.T
        pooled = imp_t[0:nce] + imp_t[nce:2 * nce]
        score = jnp.where(valid, pooled + FORCE_BONUS * forced.astype(F32), NEG)
        cnt = jnp.zeros((ns, LANES), F32)
        for i2 in range(ns):
            si = score[i2:i2 + 1, :]
            beats = (si > score) | ((si == score) & (i2 < jrow))
            cnt = cnt + beats.astype(F32)
        sel_t = ((cnt < N_SEL) & (score > 0.5 * NEG)).astype(F32)
        selp = jnp.concatenate([jnp.zeros((AUG_SEL, LANES), F32), sel_t,
                                jnp.zeros((LANES - AUG_SEL - ns, LANES), F32)], axis=0).T
        sel_bias.append(jnp.where(in_sel, (selp - 1.0) * (-NEG), 0.0))

    shift = -(q0 + QT - 1).astype(F32)

    def aug(h, base):
        sl = SLOPES[h]
        a = jnp.where(lane == AUG_HI, sl * SLC_BLK, jnp.where(lane == AUG_LO, sl,
                      jnp.where(lane == AUG_ONE, sl * shift, base)))
        return a.astype(BF16)

    qsc = [(qs[h].astype(F32) * (HD ** -0.5)).astype(BF16) for h in range(NSA_H)]
    zero_t = jnp.zeros((QT, LANES), F32)
    qsel = jnp.concatenate([jnp.concatenate([qsc[h], aug(h, sel_bias[h // NSA_R])], axis=1)
                            for h in range(NSA_H)], axis=0)
    qwin = jnp.concatenate([jnp.concatenate([qsc[h], aug(h, zero_t)], axis=1) for h in range(NSA_H)], axis=0)
    ones_k = jnp.ones((CK, LANES), BF16)

    def key_chunk(c):
        r0 = pl.multiple_of(c * CK, CK)
        return jnp.concatenate([kvs_ref[pl.ds(r0, CK), 0:128], kaug_ref[pl.ds(r0, CK), :]], axis=1)

    def fold(sc):
        out = sc[:, 0:LANES]
        for t in range(1, CK // LANES):
            out = jnp.maximum(out, sc[:, LANES * t:LANES * (t + 1)])
        return out

    nk = q0 // CK + 1
    mx_scr[...] = jnp.full((M, LANES), NEG, F32)

    def body1(c, carry):
        sc = _dot_nt(qsel, key_chunk(c))
        s_scr[c] = sc
        mx_scr[...] = jnp.maximum(mx_scr[...], fold(sc))
        return carry

    lax.fori_loop(0, nk - 1, body1, 0)
    cl = nk - 1
    kpos_l = cl * CK + lax.broadcasted_iota(I32, (1, CK), 1)
    sc = jnp.where(kpos_l <= qposm, _dot_nt(qsel, key_chunk(cl)), NEG)
    s_scr[cl] = sc
    m = jnp.max(jnp.maximum(mx_scr[...], fold(sc)), axis=1, keepdims=True)

    acc_scr[...] = jnp.zeros((M, 2 * LANES), F32)

    def body2(c, carry):
        e = jnp.exp(s_scr[c] - m).astype(BF16)
        r0 = pl.multiple_of(c * CK, CK)
        vo = jnp.concatenate([kvs_ref[pl.ds(r0, CK), 128:256], ones_k], axis=1)
        acc_scr[...] += _dot(e, vo)
        return carry

    lax.fori_loop(0, nk, body2, 0)
    acc = acc_scr[...]
    o_slc = acc[:, 0:128] / acc[:, 128:129]

    w0 = pl.multiple_of(jnp.maximum(q0 - WINDOW, 0), QT)
    WK = WINDOW + QT
    dist_w = qposm - (w0 + lax.broadcasted_iota(I32, (1, WK), 1))
    mask_w = (dist_w >= 0) & (dist_w < WINDOW)
    kwin = jnp.concatenate([win_ref[pl.ds(w0, WK), 0:128], kaug_ref[pl.ds(w0, WK), :]], axis=1)
    sw = jnp.where(mask_w, _dot_nt(qwin, kwin), NEG)
    e = jnp.exp(sw - jnp.max(sw, axis=1, keepdims=True)).astype(BF16)
    vo = jnp.concatenate([win_ref[pl.ds(w0, WK), 128:256], jnp.ones((WK, LANES), BF16)], axis=1)
    accw = _dot(e, vo)
    o_win = accw[:, 0:128] / accw[:, 128:129]

    gates = _sigmoid(sm_ref[...])
    gc, gs, gw = _gate_expand(gates, gexp_ref)
    for r in range(NSA_R):
        sl = slice(128 * r, 128 * r + 128)
        r0 = slice(QT * r, QT * (r + 1))
        r1 = slice(QT * (NSA_R + r), QT * (NSA_R + r + 1))
        oc = jnp.where(left, o_cmp[r0], o_cmp[r1])
        os_ = jnp.where(left, o_slc[r0], o_slc[r1])
        ow = jnp.where(left, o_win[r0], o_win[r1])
        o_ref[:, sl] = gc[:, sl] * oc + gs[:, sl] * os_ + gw[:, sl] * ow


def _nsa_prompt(q, sm, kc_eo, vc_eo, kvs_bf, win_bf, kaug, gexp):
    b, t, _ = q.shape
    return pl.pallas_call(
        functools.partial(_nsa_prompt_body, T=t), grid=(b, t // 128),
        in_specs=[pl.BlockSpec((None, 128, 512), lambda i, j: (i, j, 0)),
                  pl.BlockSpec((None, 128, 128), lambda i, j: (i, j, 0)),
                  pl.BlockSpec((None, t // CMP_BLK, 128), lambda i, j: (i, 0, 0)),
                  pl.BlockSpec((None, t // CMP_BLK, 128), lambda i, j: (i, 0, 0)),
                  pl.BlockSpec((None, t, 256), lambda i, j: (i, 0, 0)),
                  pl.BlockSpec(kaug.shape, lambda i, j: (0, 0)),
                  pl.BlockSpec((None, t, 256), lambda i, j: (i, 0, 0)),
                  pl.BlockSpec(gexp.shape, lambda i, j: (0, 0, 0))],
        out_specs=pl.BlockSpec((None, 128, 512), lambda i, j: (i, j, 0)),
        out_shape=jax.ShapeDtypeStruct((b, t, 512), F32),
        scratch_shapes=[pltpu.VMEM((t // KEY_CHUNK, NSA_H * 128, KEY_CHUNK), F32),
                        pltpu.VMEM((NSA_H * 128, LANES), F32), pltpu.VMEM((NSA_H * 128, 2 * LANES), F32)],
        compiler_params=_cparams(("parallel", "arbitrary")), name="nsa_prompt",
    )(q, sm, kc_eo, vc_eo, kvs_bf, kaug, win_bf, gexp)


def _sel_block_of_lane(lane, n_pages):
    return jnp.where(lane < n_pages, 2 * lane,
                     jnp.where(lane < 2 * n_pages, 2 * (lane - n_pages) + 1,
                               jnp.where(lane == 2 * n_pages, 2 * n_pages, 1 << 20)))


def _nsa_sample_sel_body(q_ref, kc_ref, vc_ref, wbuf_ref, wnew_ref, oc_ref, ow_ref, sc_ref, *, past, n_pages):
    R8 = 8
    per_page = PAGE // CMP_BLK
    ncmp = per_page * n_pages
    qpos = past + lax.broadcasted_iota(I32, (R8, 1), 0)
    lane_c = lax.broadcasted_iota(I32, (1, ncmp), 1)
    blk = (lane_c % n_pages) * per_page + lane_c // n_pages
    dist_c = qpos - (blk * CMP_BLK + (CMP_BLK - 1))
    mask_c = dist_c >= 0
    dist_cf = dist_c.astype(F32)
    kc = kc_ref[...].reshape(ncmp, LANES).astype(BF16)
    vc = vc_ref[...].reshape(ncmp, LANES).astype(BF16)

    wb = wbuf_ref.shape[3]
    wk = wbuf_ref[0].reshape(NSA_G * HD, wb).astype(BF16)
    wv = wbuf_ref[1].reshape(NSA_G * HD, wb).astype(BF16)
    nk = wnew_ref[:, 0:128].astype(BF16)
    nv = wnew_ref[:, 128:256].astype(BF16)
    dist_w1 = qpos - (past - wb + lax.broadcasted_iota(I32, (1, wb), 1))
    mask_w1 = (dist_w1 >= 0) & (dist_w1 < WINDOW)
    dist_w2 = qpos - (past + lax.broadcasted_iota(I32, (1, R8), 1))
    mask_w2 = (dist_w2 >= 0) & (dist_w2 < WINDOW)

    ns_l = 3 * LANES
    assert 2 * n_pages + 1 <= ns_l
    lane_s = lax.broadcasted_iota(I32, (1, ns_l), 1)
    jmap = _sel_block_of_lane(lane_s, n_pages)
    cur = qpos // SLC_BLK
    lane16 = lax.broadcasted_iota(I32, (1, LANES), 1)
    left = lane16 < 64

    oc = [[None] * NSA_R for _ in range(NSA_G)]
    ow = [[None] * NSA_R for _ in range(NSA_G)]
    for g in range(NSA_G):
        hm = _half_mask(g)
        imp = jnp.zeros((R8, ncmp), F32)
        for r in range(NSA_R):
            slope = SLOPES[g * NSA_R + r]
            qh = jnp.where(hm, q_ref[:, 128 * r:128 * r + 128], 0).astype(BF16)
            s = _dot_nt(qh, kc) * (HD ** -0.5) - slope * dist_cf
            p = _masked_softmax(s, mask_c)
            imp = imp + p
            oc[g][r] = _dot(p.astype(BF16), vc)
            s1 = jnp.where(mask_w1, _dot(qh, wk) * (HD ** -0.5) - slope * dist_w1.astype(F32), NEG)
            s2 = jnp.where(mask_w2, _dot_nt(qh, nk) * (HD ** -0.5) - slope * dist_w2.astype(F32), NEG)
            mx = jnp.maximum(jnp.max(s1, axis=1, keepdims=True), jnp.max(s2, axis=1, keepdims=True))
            e1 = jnp.exp(s1 - mx)
            e2 = jnp.exp(s2 - mx)
            den = jnp.sum(e1, axis=1, keepdims=True) + jnp.sum(e2, axis=1, keepdims=True)
            p1 = jnp.where(mask_w1, e1 / den, 0.0)
            p2 = jnp.where(mask_w2, e2 / den, 0.0)
            ow[g][r] = _dot_nt(p1.astype(BF16), wv) + _dot(p2.astype(BF16), nv)
        np_ = n_pages
        pooled = jnp.concatenate([imp[:, 0:np_] + imp[:, np_:2 * np_], imp[:, 2 * np_:3 * np_] + imp[:, 3 * np_:],
                                  jnp.zeros((R8, ns_l - 2 * np_), F32)], axis=1)
        valid = jmap <= cur
        forced = (jmap == 0) | (jmap == cur) | (jmap == cur - 1)
        score = jnp.where(valid, pooled + FORCE_BONUS * forced.astype(F32), NEG)
        sc_ref[g] = jnp.where(lane_s <= 2 * np_, score, -jnp.inf)
    for r in range(NSA_R):
        sl = slice(128 * r, 128 * r + 128)
        oc_ref[:, sl] = jnp.where(left, oc[0][r], oc[1][r])
        ow_ref[:, sl] = jnp.where(left, ow[0][r], ow[1][r])


def _nsa_sample_sel(q8, kvc, win_t, layer, wnew8, past):
    b = q8.shape[0]
    n_pages = past // PAGE
    per_page = PAGE // CMP_BLK
    wb = win_t.shape[-1]
    return pl.pallas_call(
        functools.partial(_nsa_sample_sel_body, past=past, n_pages=n_pages), grid=(b,),
        in_specs=[pl.BlockSpec((None, 8, 512), lambda i: (i, 0, 0)),
                  pl.BlockSpec((None, per_page, n_pages, 128), lambda i: (0, 0, i, 0)),
                  pl.BlockSpec((None, per_page, n_pages, 128), lambda i: (1, 0, i, 0)),
                  pl.BlockSpec((None, None, 2, NSA_G, HD, wb), lambda i: (layer, i, 0, 0, 0, 0)),
                  pl.BlockSpec((None, 8, 256), lambda i: (i, 0, 0))],
        out_specs=[pl.BlockSpec((None, 8, 512), lambda i: (i, 0, 0)),
                   pl.BlockSpec((None, 8, 512), lambda i: (i, 0, 0)),
                   pl.BlockSpec((None, 2, 8, 3 * LANES), lambda i: (i, 0, 0, 0))],
        out_shape=[jax.ShapeDtypeStruct((b, 8, 512), F32), jax.ShapeDtypeStruct((b, 8, 512), F32),
                   jax.ShapeDtypeStruct((b, 2, 8, 3 * LANES), F32)],
        compiler_params=_cparams(("parallel",)), name="nsa_sample_sel",
    )(q8, kvc, kvc, win_t, wnew8)


def _topk_body(sc_ref, idx_ref, *, n_pages):
    score = sc_ref[...]
    rows, width = score.shape
    blk_f = _sel_block_of_lane(lax.broadcasted_iota(I32, (1, width), 1), n_pages).astype(F32)
    lane_o = lax.broadcasted_iota(I32, (1, LANES), 1)
    idx_acc = jnp.zeros((rows, LANES), F32)
    for n in range(N_SEL):
        mx = jnp.max(score, axis=1, keepdims=True)
        pick = jnp.min(jnp.where(score == mx, blk_f, 2e6), axis=1, keepdims=True)
        idx_acc = jnp.where(lane_o == n, jnp.where(mx > 0.5 * NEG, pick, -1.0), idx_acc)
        score = jnp.where(blk_f == pick, -jnp.inf, score)
    idx_ref[...] = idx_acc.astype(I32)


def _topk(score2, n_pages):
    rows, width = score2.shape
    return pl.pallas_call(
        functools.partial(_topk_body, n_pages=n_pages), grid=(1,),
        in_specs=[pl.BlockSpec((rows, width), lambda i: (0, 0))],
        out_specs=pl.BlockSpec((rows, LANES), lambda i: (0, 0)),
        out_shape=jax.ShapeDtypeStruct((rows, LANES), I32),
        compiler_params=_cparams(("arbitrary",)), name="topk",
    )(score2)


def _nsa_sample_slc_body(idx_ref, tile_ref, pool_ref, q_ref, knew_ref, sm_ref, oc_ref, ow_ref, gexp_ref,
                         o_ref, kbuf, vbuf, sem, *, T, past):
    b = pl.program_id(0)
    nb = pl.num_programs(0)
    ns_past = past // SLC_BLK
    per_page = PAGE // SLC_BLK
    nslot = NSA_G * T * N_SEL

    def start_all(bb, buf_slot):
        def body(i, carry):
            kt = tile_ref[bb * nslot + i]
            pltpu.make_async_copy(pool_ref.at[kt], kbuf.at[buf_slot, i], sem.at[0, buf_slot]).start()
            pltpu.make_async_copy(pool_ref.at[kt + NSA_G], vbuf.at[buf_slot, i], sem.at[1, buf_slot]).start()
            return carry
        lax.fori_loop(0, nslot, body, 0, unroll=4)

    @pl.when(b == 0)
    def _():
        start_all(0, 0)

    cur_slot = lax.rem(b, 2)

    @pl.when(b + 1 < nb)
    def _():
        start_all(b + 1, 1 - cur_slot)

    pltpu.make_async_copy(kbuf.at[cur_slot], kbuf.at[cur_slot], sem.at[0, cur_slot]).wait()
    pltpu.make_async_copy(vbuf.at[cur_slot], vbuf.at[cur_slot], sem.at[1, cur_slot]).wait()

    R8 = 8
    NK = N_SEL * PAGE
    lane_k = lax.broadcasted_iota(I32, (1, NK), 1)
    slot_k = lane_k // PAGE
    within = lane_k - slot_k * PAGE
    half_k = within // SLC_BLK
    off_k = within - half_k * SLC_BLK
    rowi = lax.broadcasted_iota(I32, (R8, 1), 0)
    knew = knew_ref[:, 0:128].astype(BF16)
    vnew = knew_ref[:, 128:256].astype(BF16)
    kpos_new = past + lax.broadcasted_iota(I32, (1, R8), 1)
    zhalf = jnp.zeros((R8, HD), F32)
    o_acc = [jnp.zeros((R8, LANES), F32) for _ in range(NSA_R)]
    for g in range(NSA_G):
        hm = _half_mask(g)
        slope = jnp.zeros((R8, 1), F32)
        for r in range(NSA_R):
            slope = jnp.where(rowi == r, SLOPES[g * NSA_R + r], slope)
        for t in range(T):
            qpos = past + t
            qm = jnp.zeros((R8, LANES), F32)
            for r in range(NSA_R):
                qm = jnp.where(rowi == r, q_ref[t:t + 1, 128 * r:128 * r + 128], qm)
            qm = jnp.where(hm, qm, 0.0)
            qc = (qm[:, 0:HD] + qm[:, HD:2 * HD]).astype(BF16)
            base = (g * T + t) * N_SEL
            kt = jnp.concatenate([kbuf[cur_slot, base + n] for n in range(N_SEL)], axis=1).astype(BF16)
            vt = jnp.concatenate([vbuf[cur_slot, base + n] for n in range(N_SEL)], axis=1).astype(BF16)
            kpos = off_k
            ok_i = jnp.zeros((1, NK), I32)
            has_new = jnp.int32(0)
            for n in range(N_SEL):
                j = idx_ref[((b * NSA_G + g) * 8 + t) * N_SEL + n]
                in_blk = (slot_k == n) & (half_k == lax.rem(j, per_page))
                kpos = jnp.where(in_blk, j * SLC_BLK + off_k, kpos)
                ok_i = jnp.where(in_blk, ((j >= 0) & (j < ns_past)).astype(I32), ok_i)
                has_new = has_new + (j == ns_past).astype(I32)
            dist1 = qpos - kpos
            mask1 = (ok_i > 0) & (dist1 >= 0)
            dist2 = qpos - kpos_new
            mask2 = (dist2 >= 0) & ((jnp.zeros((1, R8), I32) + has_new) > 0)
            s1 = jnp.where(mask1, _dot(qc, kt) * (HD ** -0.5) - slope * dist1.astype(F32), NEG)
            s2 = jnp.where(mask2, _dot_nt(qm.astype(BF16), knew) * (HD ** -0.5) - slope * dist2.astype(F32), NEG)
            mx = jnp.maximum(jnp.max(s1, axis=1, keepdims=True), jnp.max(s2, axis=1, keepdims=True))
            e1 = jnp.exp(s1 - mx)
            e2 = jnp.exp(s2 - mx)
            den = jnp.sum(e1, axis=1, keepdims=True) + jnp.sum(e2, axis=1, keepdims=True)
            p1 = jnp.where(mask1, e1 / den, 0.0)
            p2 = jnp.where(mask2, e2 / den, 0.0)
            o1 = _dot_nt(p1.astype(BF16), vt)
            o1 = jnp.concatenate([o1, zhalf] if g == 0 else [zhalf, o1], axis=1)
            o = o1 + jnp.where(hm, _dot(p2.astype(BF16), vnew), 0.0)
            for r in range(NSA_R):
                o_acc[r] = jnp.where((rowi == t) & hm, o[r:r + 1, :], o_acc[r])
    gates = _sigmoid(sm_ref[...])
    gc, gs, gw = _gate_expand(gates, gexp_ref)
    for r in range(NSA_R):
        sl = slice(128 * r, 128 * r + 128)
        o_ref[:, sl] = gc[:, sl] * oc_ref[:, sl] + gs[:, sl] * o_acc[r] + gw[:, sl] * ow_ref[:, sl]


def _nsa_sample_slc(idx, pt_flat, pool_t, q8, kvnew8, sm8, oc, ow, gexp, T, past):
    b = q8.shape[0]
    n_pages = pt_flat.shape[0] // b
    row = lambda w: pl.BlockSpec((None, 8, w), lambda i, a, c: (i, 0, 0))
    nslot = NSA_G * T * N_SEL
    page = jnp.clip(idx[:, :, 0:T], 0, past // SLC_BLK - 1) // (PAGE // SLC_BLK)
    phys = jnp.take_along_axis(pt_flat.reshape(b, 1, 1, n_pages), page.reshape(b, 1, 1, -1), axis=3)
    grp = jnp.arange(NSA_G, dtype=I32).reshape(1, NSA_G, 1, 1)
    tiles = ((phys.reshape(page.shape) * 4 + 2) * NSA_G + grp).reshape(-1)
    idx_flat = idx.reshape(-1)
    grid_spec = pltpu.PrefetchScalarGridSpec(
        num_scalar_prefetch=2, grid=(b,),
        in_specs=[pl.BlockSpec(memory_space=pl.ANY), row(512), row(256), row(128), row(512), row(512),
                  pl.BlockSpec(gexp.shape, lambda i, a, c: (0, 0, 0))],
        out_specs=row(512),
        scratch_shapes=[pltpu.VMEM((2, nslot, HD, PAGE), F32), pltpu.VMEM((2, nslot, HD, PAGE), F32),
                        pltpu.SemaphoreType.DMA((2, 2))])
    return pl.pallas_call(
        functools.partial(_nsa_sample_slc_body, T=T, past=past), grid_spec=grid_spec,
        out_shape=jax.ShapeDtypeStruct((b, 8, 512), F32),
        compiler_params=_cparams(("arbitrary",)), name="nsa_sample_slc",
    )(idx_flat, tiles, pool_t, q8, kvnew8, sm8, oc, ow, gexp)


def _mix_out_body(ya_ref, yb_ref, yc_ref, x_ref, g_ref, w_ref, pool_ref, poolt_ref, gq_ref, wq_ref,
                  o_ref, q_ref):
    y = jnp.concatenate([ya_ref[...], yb_ref[...], yc_ref[...]], axis=-1)
    ms = _dot_split(y * y, pool_ref[...])
    rb = _dot_split(lax.rsqrt(ms + EPS), poolt_ref[...])
    yn = (y * rb * g_ref[...]).astype(BF16)
    x1 = x_ref[...] + _dot(yn, w_ref[...])
    o_ref[...] = x1
    q_ref[...] = _dot(_rms(x1, gq_ref[...]).astype(BF16), wq_ref[...]).astype(BF16)


def _mix_out(ya, yb, yc, x2, g, w_bf, pool, poolt, gq, wq_bf, tm):
    n = x2.shape[0]
    row = lambda w: pl.BlockSpec((tm, w), lambda i: (i, 0))
    full = lambda a: pl.BlockSpec(a.shape, lambda i: (0,) * a.ndim)
    return pl.pallas_call(
        _mix_out_body, grid=(n // tm,),
        in_specs=[row(256), row(256), row(512), row(1024), full(g), full(w_bf), full(pool), full(poolt),
                  full(gq), full(wq_bf)],
        out_specs=[row(1024), row(XA_INNER)],
        out_shape=[jax.ShapeDtypeStruct((n, 1024), F32), jax.ShapeDtypeStruct((n, XA_INNER), BF16)],
        compiler_params=_cparams(("parallel",)), name="mix_out",
    )(ya, yb, yc, x2, g, w_bf, pool, poolt, gq, wq_bf)


def _xattn_body(q_ref, kv_ref, o_ref):
    for h in range(XA_H):
        if len(kv_ref.shape) == 2:
            k = kv_ref[:, 128 * h:128 * h + 128].astype(BF16)
            v = kv_ref[:, XA_INNER + 128 * h:XA_INNER + 128 * h + 128].astype(BF16)
        else:
            k = kv_ref[:, 0, h, :].astype(BF16)
            v = kv_ref[:, 1, h, :].astype(BF16)
        s = _dot_nt(q_ref[:, 128 * h:128 * h + 128], k) * (XA_HD ** -0.5)
        e = jnp.exp(s - jnp.max(s, axis=-1, keepdims=True))
        a = e / jnp.sum(e, axis=-1, keepdims=True)
        o_ref[:, 128 * h:128 * h + 128] = _dot(a.astype(BF16), v).astype(BF16)


def _xattn(q, mkv, tq, layer=None):
    b, t, _ = q.shape
    if layer is None:
        kv_spec = pl.BlockSpec((None, N_MEM, 1024), lambda i, j: (i, 0, 0))
    else:
        kv_spec = pl.BlockSpec((None, None, N_MEM, 2, XA_H, XA_HD), lambda i, j: (layer, i, 0, 0, 0, 0))
    return pl.pallas_call(
        _xattn_body, grid=(b, t // tq),
        in_specs=[pl.BlockSpec((None, tq, 512), lambda i, j: (i, j, 0)), kv_spec],
        out_specs=pl.BlockSpec((None, tq, 512), lambda i, j: (i, j, 0)),
        out_shape=jax.ShapeDtypeStruct((b, t, 512), BF16),
        compiler_params=_cparams(("parallel", "arbitrary")), name="xattn",
    )(q, mkv)


def _moe_body(x_ref, a_ref, wo_ref, g_ref, gf_ref, rwh_ref, rwl_ref, rb_ref, wg_ref, wu_ref, wd_ref, o_ref,
              zn_s, comb_s, acc_s, *, final_norm):
    e = pl.program_id(1)
    lane = lax.broadcasted_iota(I32, (1, LANES), 1)

    @pl.when(e == 0)
    def _():
        x1 = x_ref[...] + _dot(a_ref[...], wo_ref[...])
        acc_s[...] = x1
        z = _rms(x1, g_ref[...])
        zh, zl = _split2(z)
        zn_s[...] = zh
        logits = _dot(zh, rwh_ref[...]) + _dot(zh, rwl_ref[...]) + _dot(zl, rwh_ref[...]) + rb_ref[...]
        is_g = (lane >= N_EXP) & (lane < N_EXP + N_GROUPS)
        gl = jnp.where(is_g, logits, -jnp.inf)
        gmax = jnp.max(gl, axis=-1, keepdims=True)
        gw = 1.0 / jnp.sum(jnp.exp(gl - gmax), axis=-1, keepdims=True)
        lanef = lane.astype(F32)
        grpf = (lane // EXP_PER_GROUP).astype(F32)
        gsel = jnp.min(jnp.where(gl == gmax, lanef, 1e6), axis=-1, keepdims=True) - N_EXP
        in_grp = (lane < N_EXP) & (grpf == gsel)
        le = jnp.where(in_grp, logits, -jnp.inf)
        v1 = jnp.max(le, axis=-1, keepdims=True)
        i1 = jnp.min(jnp.where(le == v1, lanef, 1e6), axis=-1, keepdims=True)
        le2 = jnp.where(lanef == i1, -jnp.inf, le)
        v2 = jnp.max(le2, axis=-1, keepdims=True)
        i2 = jnp.min(jnp.where(le2 == v2, lanef, 1e6), axis=-1, keepdims=True)
        e2 = jnp.exp(v2 - v1)
        w1 = 1.0 / (1.0 + e2)
        w2 = e2 / (1.0 + e2)
        comb_s[...] = gw * (jnp.where(lanef == i1, w1, 0.0) + jnp.where(lanef == i2, w2, 0.0))

    zn = zn_s[...]
    hg = _dot(zn, wg_ref[...])
    hu = _dot(zn, wu_ref[...])
    ce = jnp.sum(jnp.where(lane == e, comb_s[...], 0.0), axis=-1, keepdims=True)
    h = (hg * _sigmoid(hg)) * hu * ce
    acc_s[...] += _dot(h.astype(BF16), wd_ref[...])

    @pl.when(e == pl.num_programs(1) - 1)
    def _():
        o_ref[...] = _rms(acc_s[...], gf_ref[...]) if final_norm else acc_s[...]


def _moe(x2, a_bf, wo, g, rwh, rwl, rb, wg, wu, wd, layer, tm, final_gain):
    n = x2.shape[0]
    full = lambda a: pl.BlockSpec(a.shape, lambda i, e: (0,) * a.ndim)
    gf = g if final_gain is None else final_gain
    return pl.pallas_call(
        functools.partial(_moe_body, final_norm=final_gain is not None), grid=(n // tm, N_EXP),
        in_specs=[pl.BlockSpec((tm, 1024), lambda i, e: (i, 0)), pl.BlockSpec((tm, XA_INNER), lambda i, e: (i, 0)),
                  full(wo), full(g), full(gf), full(rwh), full(rwl), full(rb),
                  pl.BlockSpec((None, None, 1024, EXP_FF), lambda i, e: (layer, e, 0, 0)),
                  pl.BlockSpec((None, None, 1024, EXP_FF), lambda i, e: (layer, e, 0, 0)),
                  pl.BlockSpec((None, None, EXP_FF, 1024), lambda i, e: (layer, e, 0, 0))],
        out_specs=pl.BlockSpec((tm, 1024), lambda i, e: (i, 0)),
        out_shape=jax.ShapeDtypeStruct((n, 1024), F32),
        scratch_shapes=[pltpu.VMEM((tm, 1024), BF16), pltpu.VMEM((tm, 128), F32), pltpu.VMEM((tm, 1024), F32)],
        compiler_params=_cparams(("parallel", "arbitrary")), name="moe",
    )(x2, a_bf, wo, g, gf, rwh, rwl, rb, wg, wu, wd)


def _q_perm():
    idx = []
    for r in range(NSA_R):
        for g in range(NSA_G):
            h = g * NSA_R + r
            idx.extend(range(64 * h, 64 * h + 64))
    return np.asarray(idx)


def _prep_layer(w, l):
    p = {"layer": l}
    win = w["w_in"][l]
    qp = _q_perm()
    o_q = 1032 + 512
    cols = [win[:, 0:1024], win[:, 1032:1032 + 512], win[:, o_q:o_q + 512][:, qp],
            win[:, o_q + 512:o_q + 512 + 768], win[:, 1024:1032], win[:, o_q + 1280:o_q + 1304],
            jnp.zeros((D_MODEL, 128 - 32), F32)]
    p["w_in"] = jnp.concatenate(cols, axis=1).astype(BF16)
    p["w_st"] = win[:, 1024:1032].T.astype(BF16)
    p["norm_mix"] = w["norm_mix"][l][None]
    bias8 = jnp.concatenate([w["ml_i_bias"][l], w["ml_f_bias"][l]])
    p["ml_bcol"] = jnp.zeros((1, 128), F32).at[0, 0:8].set(bias8)
    p["ml_brow"] = bias8[:, None]
    p["conv_w"] = w["conv_w"][l]
    p["conv_b"] = w["conv_b"][l][None]
    bd = lambda m: jax.scipy.linalg.block_diag(*[m[i] for i in range(m.shape[0])])
    p["lru_wa"] = bd(w["lru_wa"][l]).astype(BF16)
    p["lru_wx"] = bd(w["lru_wx"][l]).astype(BF16)
    p["lru_ba"] = w["lru_ba"][l][None]
    p["lru_bx"] = w["lru_bx"][l][None]
    p["lru_lambda"] = w["lru_lambda"][l][None]
    cw = {}
    eye4 = jnp.eye(PAGE // CMP_BLK, dtype=F32)
    pw, ppos, pb1, pw2 = [], [], [], []
    for c, nm in ((0, "k"), (1, "v")):
        w1 = w["phi_w1"][l, c].reshape(CMP_BLK, HD, CMP_HID)
        z = jnp.zeros_like(w1)
        cw["bd" + nm] = jnp.concatenate([jnp.concatenate([w1, z], axis=2),
                                         jnp.concatenate([z, w1], axis=2)], axis=1).astype(BF16)
        cw["pos" + nm] = jnp.tile(w["cmp_pos"][l, c], (1, 2))
        cw["b1" + nm] = jnp.tile(w["phi_b1"][l, c], 2)[None]
        w2 = w["phi_w2"][l, c]
        z2 = jnp.zeros_like(w2)
        cw["w2" + nm] = jnp.concatenate([jnp.concatenate([w2, z2], axis=1),
                                         jnp.concatenate([z2, w2], axis=1)], axis=0).astype(BF16)
        w3 = w1.transpose(1, 0, 2).reshape(HD // 2, 2, CMP_BLK, CMP_HID)
        pw.append(jnp.einsum("pdrh,bc->pdbrch", w3, eye4).reshape(HD // 2, 256, 512).astype(BF16))
        pos3 = w["cmp_pos"][l, c].T.reshape(HD // 2, 2, 1, CMP_BLK)
        ppos.append(jnp.broadcast_to(pos3, (HD // 2, 2, 4, CMP_BLK)).reshape(HD // 2, 256))
        pb1.append(jnp.tile(w["phi_b1"][l, c], 4)[None])
        pw2.append(jnp.kron(eye4, w2).astype(BF16))
    cw["pw"] = jnp.stack(pw)
    cw["ppos"] = jnp.stack(ppos)
    cw["pb1"] = jnp.stack(pb1)
    cw["pw2"] = jnp.stack(pw2)
    p["cmp"] = cw
    perm = np.concatenate([np.arange(512), 512 + qp])
    p["mix_norm"] = w["mix_norm"][l][perm][None]
    p["w_out"] = w["w_out"][l][perm, :].astype(BF16)
    p["norm_xa"] = w["norm_xa"][l][None]
    p["norm_mem"] = w["norm_mem"][l][None]
    p["xa_wq"] = w["xa_wq"][l].astype(BF16)
    p["xa_wkv"] = w["xa_wkv"][l].astype(BF16)
    p["xa_wo"] = w["xa_wo"][l].astype(BF16)
    p["norm_ffn"] = w["norm_ffn"][l][None]
    rw = jnp.concatenate([w["router_ew"][l], w["router_gw"][l], jnp.zeros((D_MODEL, 128 - 20), F32)], axis=1)
    rwh = rw.astype(BF16)
    p["rwh"] = rwh
    p["rwl"] = (rw - rwh.astype(F32)).astype(BF16)
    p["rb"] = jnp.concatenate([w["router_eb"][l], w["router_gb"][l], jnp.zeros((128 - 20,), F32)])[None]
    return p


def _constants(T):
    c = {}
    head = np.arange(1024) // HD
    pool = np.zeros((1024, 128), np.float32)
    pool[np.arange(1024), head] = 1.0 / HD
    poolt = np.zeros((128, 1024), np.float32)
    poolt[head, np.arange(1024)] = 1.0
    c["pool"] = jnp.asarray(pool, BF16)
    c["poolt"] = jnp.asarray(poolt, BF16)
    gexp = np.zeros((3, 128, 512), np.float32)
    for cc in range(3):
        for g in range(NSA_G):
            for r in range(NSA_R):
                gexp[cc, 8 + cc * 8 + g * 4 + r, 128 * r + 64 * g:128 * r + 64 * g + 64] = 1.0
    c["gexp"] = jnp.asarray(gexp, BF16)
    kpos = np.arange(T)
    kaug = np.zeros((T, 128), np.float32)
    kaug[:, AUG_HI] = kpos // SLC_BLK
    kaug[:, AUG_LO] = kpos % SLC_BLK
    kaug[:, AUG_ONE] = 1.0
    kaug[kpos, AUG_SEL + kpos // SLC_BLK] = 1.0
    c["kaug"] = jnp.asarray(kaug, BF16)
    return c


def _even_odd(a):
    return jnp.concatenate([a[:, 0::2], a[:, 1::2]], axis=1)


def _pad_rows(a, rows):
    return jnp.pad(a, ((0, 0), (0, rows - a.shape[1]), (0, 0)))


def _dense_tail(x2, q2, p, ew, B, T, mkv, tq, tm_moe, cache_layer=None):
    n = x2.shape[0]
    q = q2.reshape(B, T, XA_INNER)
    if T < tq:
        o = _xattn(_pad_rows(q, tq), mkv, tq, cache_layer)[:, 0:T]
    else:
        o = _xattn(q, mkv, tq, cache_layer)
    return _moe(x2, o.reshape(n, XA_INNER), p["xa_wo"], p["norm_ffn"], p["rwh"], p["rwl"], p["rb"],
                ew[0], ew[1], ew[2], p["layer"], tm_moe, p["final_gain"])


def _layer_prompt(x2, p, c, ew, B, T, mem2):
    n = B * T
    ml, lru, q, kv, win, sm, smt, kvs_bf, win_bf = _in_proj(x2, p["norm_mix"], p["w_in"], p["w_st"], 256)
    smt_b = smt.reshape(8, B, T).transpose(1, 0, 2)
    ya, C, nn, mm = _mlstm(ml.reshape(B, T, 1024), sm.reshape(B, T, 128), smt_b, p["ml_bcol"], p["ml_brow"],
                           jnp.zeros((B, 4, 64, 64), F32), jnp.zeros((B, 4, 1, 64), F32),
                           jnp.zeros((B, 1, 128), F32), ML_CHUNK, ML_CHUNK)
    u_tm = lru.reshape(B, T, 512).transpose(1, 0, 2)
    yb_tm, tail_tm, h_last = _lru(u_tm, jnp.zeros((3, B, 256), F32), jnp.zeros((B, 256), F32), p["conv_w"],
                                  p["conv_b"], p["lru_wa"], p["lru_ba"], p["lru_wx"], p["lru_bx"],
                                  p["lru_lambda"], 256)
    yb = yb_tm.transpose(1, 0, 2).reshape(n, 256)
    n_pages = n // PAGE
    kc, vc = _compress_rows(jnp.arange(n_pages, dtype=I32), kv.reshape(n_pages, PAGE, 512), p["cmp"], 16)
    nc = T // CMP_BLK
    kc_eo = _even_odd(kc.reshape(B, nc, 128))
    vc_eo = _even_odd(vc.reshape(B, nc, 128))
    yc = _nsa_prompt(q.reshape(B, T, 512), sm.reshape(B, T, 128), kc_eo, vc_eo, kvs_bf.reshape(B, T, 256),
                     win_bf.reshape(B, T, 256), c["kaug"], c["gexp"])
    x2, xq = _mix_out(ya.reshape(n, 256), yb, yc.reshape(n, 512), x2, p["mix_norm"], p["w_out"], c["pool"],
                      c["poolt"], p["norm_xa"], p["xa_wq"], 256)
    mkv = _norm_mm(mem2, p["norm_mem"], p["xa_wkv"], 256, F32)
    x2 = _dense_tail(x2, xq, p, ew, B, T, mkv.reshape(B, N_MEM, 1024), 512, 1024)
    st = (kv.reshape(B, T, 4, NSA_G, HD), win.reshape(B, T, 2, NSA_G, HD)[:, T - WINDOW:],
          C, nn.reshape(B, 4, 64), mm[:, 0, 0:4], h_last, tail_tm.transpose(1, 0, 2))
    return x2, st, mkv.reshape(B, N_MEM, 2, XA_H, XA_HD)


def _layer_sample(x2, p, c, ew, B, T, pool_t, pt_flat, win_buf, win_t, C0, n0, m0, conv0, h0, mkv, past):
    n = B * T
    _pad_rows8 = lambda a: _pad_rows(a, 8)
    ml, lru, q, kv, win, sm, smt, _, _ = _in_proj(x2, p["norm_mix"], p["w_in"], p["w_st"], n)
    sm8 = _pad_rows8(sm.reshape(B, T, 128))
    smt_b = jnp.pad(smt.reshape(8, B, T).transpose(1, 0, 2), ((0, 0), (0, 0), (0, 8 - T)))
    m0p = jnp.pad(m0[:, None, :], ((0, 0), (0, 0), (0, 128 - ML_H)))
    ya, C, nn, mm = _mlstm(_pad_rows8(ml.reshape(B, T, 1024)), sm8, smt_b, p["ml_bcol"], p["ml_brow"],
                           C0, n0[:, :, None, :], m0p, 8, T)
    ya = ya[:, 0:T]
    u_tm = lru.reshape(B, T, 512).transpose(1, 0, 2)
    yb_tm, tail_tm, h_last = _lru(u_tm, conv0.transpose(1, 0, 2), h0, p["conv_w"], p["conv_b"], p["lru_wa"],
                                  p["lru_ba"], p["lru_wx"], p["lru_bx"], p["lru_lambda"], T)
    yb = yb_tm.transpose(1, 0, 2).reshape(n, 256)
    n_pages = past // PAGE
    kvc = _compress_pages(pt_flat, pool_t, p["cmp"], CMP_STEP_PAGES)
    wb = win_buf.shape[1]
    q8 = _pad_rows8(q.reshape(B, T, 512).astype(F32))
    win3 = win.reshape(B, T, 256)
    oc, ow, sc = _nsa_sample_sel(q8, kvc, win_t, p["layer"], _pad_rows8(win3), past)
    idx = _topk(sc.reshape(B * NSA_G * 8, 3 * LANES), n_pages)[:, 0:N_SEL].reshape(B, NSA_G, 8, N_SEL)
    kv3 = kv.reshape(B, T, 512)
    yc8 = _nsa_sample_slc(idx, pt_flat, pool_t, q8, _pad_rows8(kv3[:, :, 256:512]), sm8, oc, ow,
                          c["gexp"], T, past)
    yc = yc8[:, 0:T].reshape(n, 512)
    x2, xq = _mix_out(ya.reshape(n, 256), yb, yc, x2, p["mix_norm"], p["w_out"], c["pool"], c["poolt"],
                      p["norm_xa"], p["xa_wq"], n)
    x2 = _dense_tail(x2, xq, p, ew, B, T, mkv, 16, n, cache_layer=p["layer"])
    win_new = jnp.concatenate([win_buf.reshape(B, wb, 256), win3], axis=1)[:, T:]
    st = (kv.reshape(B, T, 4, NSA_G, HD), win_new.reshape(B, wb, 2, NSA_G, HD), C, nn.reshape(B, 4, 64),
          mm[:, 0, 0:4], h_last, tail_tm.transpose(1, 0, 2))
    return x2, st


def kernel(x_prompt, x_sample, cache_nsa_kv, state_nsa_win, state_mlstm_C, state_mlstm_n, state_mlstm_m, state_rglru_h, state_rglru_conv, cache_mem_kv, page_table, mem_prompt, norm_mix, w_in, ml_i_bias, ml_f_bias, conv_w, conv_b, lru_wa, lru_ba, lru_wx, lru_bx, lru_lambda, phi_w1, phi_b1, phi_w2, cmp_pos, mix_norm, w_out, norm_xa, norm_mem, xa_wq, xa_wkv, xa_wo, norm_ffn, router_gw, router_gb, router_ew, router_eb, exp_w_gate, exp_w_up, exp_w_down, final_norm):
    w = dict(norm_mix=norm_mix, w_in=w_in, ml_i_bias=ml_i_bias, ml_f_bias=ml_f_bias, conv_w=conv_w,
             conv_b=conv_b, lru_wa=lru_wa, lru_ba=lru_ba, lru_wx=lru_wx, lru_bx=lru_bx, lru_lambda=lru_lambda,
             phi_w1=phi_w1, phi_b1=phi_b1, phi_w2=phi_w2, cmp_pos=cmp_pos, mix_norm=mix_norm, w_out=w_out,
             norm_xa=norm_xa, norm_mem=norm_mem, xa_wq=xa_wq, xa_wkv=xa_wkv, xa_wo=xa_wo, norm_ffn=norm_ffn,
             router_gw=router_gw, router_gb=router_gb, router_ew=router_ew, router_eb=router_eb)
    depth = w_in.shape[0]
    B, T, _ = x_prompt.shape
    BS, TS, _ = x_sample.shape
    n_pages = page_table.shape[1]
    past = n_pages * PAGE
    n_phys = cache_nsa_kv.shape[1]
    consts = _constants(T)
    layers = [_prep_layer(w, l) for l in range(depth)]
    for l in range(depth):
        layers[l]["final_gain"] = final_norm[None] if l == depth - 1 else None
    ew = (exp_w_gate.astype(BF16), exp_w_up.astype(BF16), exp_w_down.astype(BF16))

    x2 = x_prompt.reshape(B * T, D_MODEL)
    mem2 = mem_prompt.reshape(B * N_MEM, D_MODEL)
    outs_p, mem_p = [], []
    for l in range(depth):
        x2, st, mkv = _layer_prompt(x2, layers[l], consts, ew, B, T, mem2)
        outs_p.append(st)
        mem_p.append(mkv)
    y_prompt = x2.reshape(B, T, D_MODEL)

    pool_t = jnp.transpose(cache_nsa_kv, (0, 1, 3, 4, 5, 2)).reshape(depth * n_phys * 4 * NSA_G, HD, PAGE)
    win_t = jnp.transpose(state_nsa_win, (0, 1, 3, 4, 5, 2))
    xs = x_sample.reshape(BS * TS, D_MODEL)
    outs_s = []
    for l in range(depth):
        pt_flat = page_table.reshape(-1) + l * n_phys
        xs, st = _layer_sample(xs, layers[l], consts, ew, BS, TS, pool_t, pt_flat, state_nsa_win[l], win_t,
                               state_mlstm_C[l], state_mlstm_n[l], state_mlstm_m[l], state_rglru_conv[l],
                               state_rglru_h[l], cache_mem_kv, past)
        outs_s.append(st)
    y_sample = xs.reshape(BS, TS, D_MODEL)

    sp = [jnp.stack(a) for a in zip(*outs_p)]
    ss = [jnp.stack(a) for a in zip(*outs_s)]
    return (y_prompt, y_sample, sp[0], ss[0], sp[1], ss[1], sp[2], ss[2], sp[3], ss[3], sp[4], ss[4],
            sp[5], ss[5], sp[6], ss[6], jnp.stack(mem_p))
```
